```python
import math
import jax
import jax.numpy as jnp
from jax import lax
import numpy as np

D_MODEL = 1024
BATCH = 8
SEQ = 4096
DEPTH = 4

CTX_LEN = 256
GRID_W = 64

DA_HEADS = 4
DA_HD = 64
DA_W = DA_HEADS * 2 * DA_HD
ROPE_BASE = 10000.0
Q_BLOCK = 128
RW_HEADS = 8
RW_HD = 64
RW_W = RW_HEADS * RW_HD
RW_DECAY_R = 64
RW_ICLR_R = 64
RW_GATE_R = 128
RW_GN_EPS = 64e-5
RW_COLS = 3 * RW_W + 2 * RW_DECAY_R + 2 * RW_ICLR_R + RW_GATE_R
SSM_HEADS = 8
SSM_HD = 64
SSM_W = SSM_HEADS * SSM_HD
SSM_GROUPS = 2
SSM_STATE = 128
SSM_CONV = 3
SSM_CHUNK = 128
SSM_XBC = SSM_W + 2 * SSM_GROUPS * SSM_STATE
FFN_HIDDEN = (8 * D_MODEL + 3 * 256 - 1) // (3 * 256) * 256
NORM_EPS = 1e-5
W_IN_SPLIT = (DA_W, DA_W, DA_W, RW_COLS, SSM_W, SSM_XBC, 2 * SSM_HEADS, D_MODEL, D_MODEL, D_MODEL)
W_IN_COLS = sum(W_IN_SPLIT)

kernel_name = 'hybrid_diff_rwkv7_ssd_deepnorm_dit'


def _split_cols(t, sizes):
    idx, acc = [], 0
    for s in sizes[:-1]:
        acc += s
        idx.append(acc)
    return jnp.split(t, idx, axis=-1)


def _layernorm(t, g, b):
    tf = t.astype(jnp.float32)
    mu = jnp.mean(tf, -1, keepdims=True)
    var = jnp.mean(jnp.square(tf - mu), -1, keepdims=True)
    return ((tf - mu) * lax.rsqrt(var + NORM_EPS) * g + b).astype(t.dtype)


def _rms(t, eps):
    t = t.astype(jnp.float32)
    return t * lax.rsqrt(jnp.mean(jnp.square(t), -1, keepdims=True) + eps)


def _swiglu(h, w1, w3, w2):
    return (jax.nn.silu(h @ w1) * (h @ w3)) @ w2


def _centred_shift(u):
    up = jnp.pad(u, ((0, 0), (1, 1), (0, 0)))
    return 0.5 * (up[:, :-2] + up[:, 2:])


def _centred_dwconv(u, w, b):
    pad = w.shape[0] // 2
    y = lax.conv_general_dilated(u, w[:, None, :], window_strides=(1,), padding=[(pad, pad)],
                                 dimension_numbers=('NWC', 'WIO', 'NWC'),
                                 feature_group_count=u.shape[-1])
    return y + b


def _axial_rope_angles(n_tok):
    rows = n_tok // GRID_W
    row = jnp.repeat(jnp.arange(rows), GRID_W).astype(jnp.float32)
    col = jnp.tile(jnp.arange(GRID_W), rows).astype(jnp.float32)
    nf = DA_HD // 4
    inv = ROPE_BASE ** (-jnp.arange(nf, dtype=jnp.float32) / nf)
    return row[:, None] * inv, col[:, None] * inv


def _apply_axial_rope(t, ang_r, ang_c):
    def rot(u, ang):
        cs = jnp.cos(ang)[:, None, None, :]
        sn = jnp.sin(ang)[:, None, None, :]
        u1, u2 = jnp.split(u, 2, axis=-1)
        return jnp.concatenate([u1 * cs - u2 * sn, u1 * sn + u2 * cs], -1)
    t_r, t_c = jnp.split(t, 2, axis=-1)
    return jnp.concatenate([rot(t_r, ang_r), rot(t_c, ang_c)], -1).astype(t.dtype)


def _diff_softmax_attend(q, k, v, lam):
    s = jnp.einsum('bqhmd,bkhmd->bhmqk', q, k, preferred_element_type=jnp.float32) * (DA_HD ** -0.5)
    p = jax.nn.softmax(s, axis=-1)
    a = p[:, :, 0] - lam * p[:, :, 1]
    return jnp.einsum('bhqk,bkhe->bqhe', a.astype(v.dtype), v)


def _diff_attn_latent(q, k, v, kc, vc, lam):
    b, L = q.shape[:2]
    k_all = jnp.concatenate([kc, k], axis=1)
    v_all = jnp.concatenate([vc, v], axis=1)
    nb = L // Q_BLOCK
    qb = q.reshape(b, nb, Q_BLOCK, DA_HEADS, 2, DA_HD).transpose(1, 0, 2, 3, 4, 5)
    out = lax.map(lambda qq: _diff_softmax_attend(qq, k_all, v_all, lam), qb)
    return out.transpose(1, 0, 2, 3, 4).reshape(b, L, DA_HEADS, 2 * DA_HD)


def _diff_attn_branch(q, k, v, lp, layer, side, want_out):
    b, L, _ = q.shape
    q = q.reshape(b, L, DA_HEADS, 2, DA_HD)
    k = k.reshape(b, L, DA_HEADS, 2, DA_HD)
    v = v.reshape(b, L, DA_HEADS, 2 * DA_HD)
    lam_init = 0.8 - 0.6 * math.exp(-0.3 * layer)
    f32 = jnp.float32
    lam = (jnp.exp(jnp.sum(lp['da_lq1'].astype(f32) * lp['da_lk1'].astype(f32)))
           - jnp.exp(jnp.sum(lp['da_lq2'].astype(f32) * lp['da_lk2'].astype(f32))) + lam_init)
    if side is None:
        new_side = (k, v)
        if not want_out:
            return None, new_side
        y = _diff_softmax_attend(q, k, v, lam)
    else:
        ang_r, ang_c = _axial_rope_angles(L)
        q = _apply_axial_rope(q, ang_r, ang_c)
        k = _apply_axial_rope(k, ang_r, ang_c)
        new_side = None
        y = _diff_attn_latent(q, k, v, side[0], side[1], lam)
    y = _rms(y, NORM_EPS) * lp['da_norm_g'] * (1.0 - lam_init)
    return y.reshape(b, L, DA_W), new_side


def _rwkv_scan(r, w, k, v, kk, a, s0, reverse):
    def step(S, inp):
        r_t, w_t, k_t, v_t, kk_t, a_t = inp
        s_kk = jnp.einsum('bhvk,bhk->bhv', S, kk_t)
        S = (S * w_t[:, :, None, :] - s_kk[..., None] * (kk_t * a_t)[:, :, None, :]
             + v_t[..., None] * k_t[:, :, None, :])
        return S, jnp.einsum('bhvk,bhk->bhv', S, r_t)
    xs = tuple(jnp.moveaxis(t, 1, 0) for t in (r, w, k, v, kk, a))
    s_fin, ys = lax.scan(step, s0, xs, reverse=reverse)
    return jnp.moveaxis(ys, 0, 1), s_fin


def _rwkv_branch(u, lp, side, want_out):
    b, L, _ = u.shape
    f32 = jnp.float32
    H, N = RW_HEADS, RW_HD
    u = u + lp['rw_mu'] * (_centred_shift(u) - u)
    r, k, v, wd, ad, gd = _split_cols(u, (RW_W, RW_W, RW_W, 2 * RW_DECAY_R, 2 * RW_ICLR_R, RW_GATE_R))
    r = r.astype(f32).reshape(b, L, H, N)
    k = k.astype(f32).reshape(b, L, H, N)
    v = v.astype(f32).reshape(b, L, H, N)
    wd = jnp.tanh(wd.astype(f32)).reshape(b, L, 2, RW_DECAY_R)
    w_raw = lp['rw_w0'] + jnp.einsum('bldr,drc->bldc', wd, lp['rw_w2'])
    decay = jnp.exp(-jnp.exp(-jax.nn.softplus(-w_raw) - 0.5)).reshape(b, L, 2, H, N)
    ad = ad.astype(f32).reshape(b, L, 2, RW_ICLR_R)
    a = jax.nn.sigmoid(lp['rw_a0'] + jnp.einsum('bldr,drc->bldc', ad, lp['rw_a2'])).reshape(b, L, 2, H, N)
    kk = k * lp['rw_kk'].reshape(H, N)
    kk = kk * lax.rsqrt(jnp.maximum(jnp.sum(jnp.square(kk), -1, keepdims=True), 1e-24))
    k_dir = k[:, :, None] * (1.0 + (a - 1.0) * lp['rw_ka'].reshape(H, N))
    if side is None:
        s0f = jnp.zeros((b, H, N, N), f32)
        s0b = jnp.zeros((b, H, N, N), f32)
    else:
        s0f, s0b = side
    y_f, s_f = _rwkv_scan(r, decay[:, :, 0], k_dir[:, :, 0], v, kk, a[:, :, 0], s0f, False)
    y_b, s_b = _rwkv_scan(r, decay[:, :, 1], k_dir[:, :, 1], v, kk, a[:, :, 1], s0b, True)
    new_side = (s_f, s_b)
    if not want_out:
        return None, new_side
    y = y_f + y_b
    mu = jnp.mean(y, -1, keepdims=True)
    var = jnp.mean(jnp.square(y - mu), -1, keepdims=True)
    y = ((y - mu) * lax.rsqrt(var + RW_GN_EPS)).reshape(b, L, RW_W) * lp['rw_ln_g'] + lp['rw_ln_b']
    k_bonus = 0.5 * (k_dir[:, :, 0] + k_dir[:, :, 1])
    bonus = (jnp.sum(r * k_bonus * lp['rw_rk'], -1, keepdims=True) * v).reshape(b, L, RW_W)
    g = jnp.einsum('blr,rc->blc', jax.nn.sigmoid(gd.astype(f32)), lp['rw_g2'])
    return (y + bonus) * g, new_side


def _segsum_exp(a):
    T = a.shape[-1]
    cs = jnp.cumsum(a, -1)
    diff = cs[..., :, None] - cs[..., None, :]
    mask = jnp.tril(jnp.ones((T, T), dtype=bool))
    return jnp.where(mask, jnp.exp(jnp.where(mask, diff, 0.0)), 0.0)


def _ssd(x, dA, bm, cm, s0):
    b, L, h, p = x.shape
    n = bm.shape[-1]
    c, l = L // SSM_CHUNK, SSM_CHUNK
    x = x.reshape(b, c, l, h, p)
    bm = bm.reshape(b, c, l, h, n)
    cm = cm.reshape(b, c, l, h, n)
    a = dA.reshape(b, c, l, h).transpose(0, 3, 1, 2)
    a_cs = jnp.cumsum(a, -1)
    scores = jnp.einsum('bclhn,bcshn->bhcls', cm, bm) * _segsum_exp(a)
    y_diag = jnp.einsum('bhcls,bcshp->bclhp', scores, x)
    decay_states = jnp.exp(a_cs[..., -1:] - a_cs)
    states = jnp.einsum('bclhn,bhcl,bclhp->bchpn', bm, decay_states, x)
    states = jnp.concatenate([s0[:, None], states], axis=1)
    chunk_decay = _segsum_exp(jnp.pad(a_cs[..., -1], ((0, 0), (0, 0), (1, 0))))
    states = jnp.einsum('bhzc,bchpn->bzhpn', chunk_decay, states)
    prev, final = states[:, :-1], states[:, -1]
    y_off = jnp.einsum('bclhn,bchpn,bhcl->bclhp', cm, prev, jnp.exp(a_cs))
    return (y_diag + y_off).reshape(b, L, h, p), final


def _ssm_branch(z, xbc, dt_raw, lp, side, want_out):
    b, L, _ = z.shape
    f32 = jnp.float32
    xbc = jax.nn.silu(_centred_dwconv(xbc, lp['ssm_conv_w'], lp['ssm_conv_b'])).astype(f32)
    xs, bm, cm = _split_cols(xbc, (SSM_W, SSM_GROUPS * SSM_STATE, SSM_GROUPS * SSM_STATE))
    xs = xs.reshape(b, L, SSM_HEADS, SSM_HD)
    rep = SSM_HEADS // SSM_GROUPS
    bm = jnp.repeat(bm.reshape(b, L, SSM_GROUPS, SSM_STATE), rep, axis=2)
    cm = jnp.repeat(cm.reshape(b, L, SSM_GROUPS, SSM_STATE), rep, axis=2)
    dt = jax.nn.softplus(dt_raw.astype(f32).reshape(b, L, 2, SSM_HEADS) + lp['ssm_dt_bias'])
    A = -jnp.exp(lp['ssm_a_log'].astype(f32))
    if side is None:
        s0f = jnp.zeros((b, SSM_HEADS, SSM_HD, SSM_STATE), f32)
        s0b = jnp.zeros((b, SSM_HEADS, SSM_HD, SSM_STATE), f32)
    else:
        s0f, s0b = side
    flip = lambda t: jnp.flip(t, axis=1)
    y_f, s_f = _ssd(xs * dt[:, :, 0, :, None], dt[:, :, 0] * A[0], bm, cm, s0f)
    y_b, s_b = _ssd(flip(xs * dt[:, :, 1, :, None]), flip(dt[:, :, 1] * A[1]), flip(bm), flip(cm), s0b)
    new_side = (s_f, s_b)
    if not want_out:
        return None, new_side
    y = (y_f + flip(y_b) + lp['ssm_d'][:, None] * xs).reshape(b, L, SSM_W) * jax.nn.silu(z.astype(f32))
    y = _rms(y.reshape(b, L, SSM_GROUPS, SSM_W // SSM_GROUPS), NORM_EPS).reshape(b, L, SSM_W)
    return y * lp['ssm_norm_g'], new_side


def _mixer(h, lp, layer, side, want_out):
    q, k, v, rw, z, xbc, dt_raw, ga, gr, gs = _split_cols(h @ lp['w_in'], W_IN_SPLIT)
    sa, sr, ss = (None, None, None) if side is None else side
    y_a, side_a = _diff_attn_branch(q, k, v, lp, layer, sa, want_out)
    y_r, side_r = _rwkv_branch(rw, lp, sr, want_out)
    y_s, side_s = _ssm_branch(z, xbc, dt_raw, lp, ss, want_out)
    new_side = (side_a, side_r, side_s)
    if not want_out:
        return None, new_side
    dt_ = h.dtype
    m = (jax.nn.sigmoid(ga) * (y_a.astype(dt_) @ lp['p_attn'])
         + jax.nn.sigmoid(gr) * (y_r.astype(dt_) @ lp['p_rwkv'])
         + jax.nn.sigmoid(gs) * (y_s.astype(dt_) @ lp['p_ssm']))
    return m @ lp['w_out'], new_side


def setup_inputs(seed: int = 0) -> dict:
    key = jax.random.key(seed)
    ks = iter(jax.random.split(key, 64))
    f32 = jnp.float32
    nrm = lambda shape, s: jax.random.normal(next(ks), shape, f32) * s
    beta = (8.0 * DEPTH) ** -0.25
    chan = jnp.arange(RW_W, dtype=f32) / (RW_W - 1)
    w0_base = -6.0 + 5.0 * chan ** 0.9 + 0.5
    dt0 = jnp.exp(jax.random.uniform(next(ks), (DEPTH, 2, SSM_HEADS), f32, math.log(1e-3), math.log(1e-1)))
    return {
        'x': nrm((BATCH, SEQ, D_MODEL), 1.0),
        'c': nrm((BATCH, D_MODEL), 1.0),
        'ctx': nrm((BATCH, CTX_LEN, D_MODEL), 1.0),
        'c_ctx': nrm((D_MODEL,), 1.0),
        'ada_w': nrm((DEPTH, D_MODEL, 6 * D_MODEL), 0.5 * D_MODEL ** -0.5),
        'ada_b': nrm((DEPTH, 6 * D_MODEL), 0.01),
        'w_in': nrm((DEPTH, D_MODEL, W_IN_COLS), D_MODEL ** -0.5),
        'da_lq1': nrm((DEPTH, DA_HD), 0.1),
        'da_lk1': nrm((DEPTH, DA_HD), 0.1),
        'da_lq2': nrm((DEPTH, DA_HD), 0.1),
        'da_lk2': nrm((DEPTH, DA_HD), 0.1),
        'da_norm_g': 1.0 + nrm((DEPTH, 2 * DA_HD), 0.01),
        'rw_mu': jax.random.uniform(next(ks), (DEPTH, RW_COLS), f32, 0.2, 0.8),
        'rw_w0': w0_base + nrm((DEPTH, 2, RW_W), 0.1),
        'rw_w2': nrm((DEPTH, 2, RW_DECAY_R, RW_W), 0.5 * RW_DECAY_R ** -0.5),
        'rw_a0': nrm((DEPTH, 2, RW_W), 0.1),
        'rw_a2': nrm((DEPTH, 2, RW_ICLR_R, RW_W), 0.5 * RW_ICLR_R ** -0.5),
        'rw_g2': nrm((DEPTH, RW_GATE_R, RW_W), RW_GATE_R ** -0.5),
        'rw_kk': 0.85 + nrm((DEPTH, RW_W), 0.02),
        'rw_ka': 1.0 + nrm((DEPTH, RW_W), 0.02),
        'rw_rk': nrm((DEPTH, RW_HEADS, RW_HD), 0.1),
        'rw_ln_g': 1.0 + nrm((DEPTH, RW_W), 0.01),
        'rw_ln_b': nrm((DEPTH, RW_W), 0.01),
        'ssm_conv_w': nrm((DEPTH, SSM_CONV, SSM_XBC), SSM_CONV ** -0.5),
        'ssm_conv_b': nrm((DEPTH, SSM_XBC), 0.01),
        'ssm_dt_bias': dt0 + jnp.log(-jnp.expm1(-dt0)),
        'ssm_a_log': jnp.log(jax.random.uniform(next(ks), (DEPTH, 2, SSM_HEADS), f32, 1.0, 16.0)),
        'ssm_d': 1.0 + nrm((DEPTH, SSM_HEADS), 0.1),
        'ssm_norm_g': 1.0 + nrm((DEPTH, SSM_W), 0.01),
        'p_attn': nrm((DEPTH, DA_W, D_MODEL), DA_W ** -0.5),
        'p_rwkv': nrm((DEPTH, RW_W, D_MODEL), RW_W ** -0.5),
        'p_ssm': nrm((DEPTH, SSM_W, D_MODEL), SSM_W ** -0.5),
        'w_out': nrm((DEPTH, D_MODEL, D_MODEL), beta * D_MODEL ** -0.5),
        'ln1_g': 1.0 + nrm((DEPTH, D_MODEL), 0.01),
        'ln1_b': nrm((DEPTH, D_MODEL), 0.01),
        'ffn_w1': nrm((DEPTH, D_MODEL, FFN_HIDDEN), D_MODEL ** -0.5),
        'ffn_w3': nrm((DEPTH, D_MODEL, FFN_HIDDEN), D_MODEL ** -0.5),
        'ffn_w2': nrm((DEPTH, FFN_HIDDEN, D_MODEL), beta * FFN_HIDDEN ** -0.5),
        'ln2_g': 1.0 + nrm((DEPTH, D_MODEL), 0.01),
        'ln2_b': nrm((DEPTH, D_MODEL), 0.01),
    }


def reference(x, c, ctx, c_ctx, ada_w, ada_b, w_in, da_lq1, da_lk1, da_lq2, da_lk2, da_norm_g,
              rw_mu, rw_w0, rw_w2, rw_a0, rw_a2, rw_g2, rw_kk, rw_ka, rw_rk, rw_ln_g, rw_ln_b,
              ssm_conv_w, ssm_conv_b, ssm_dt_bias, ssm_a_log, ssm_d, ssm_norm_g,
              p_attn, p_rwkv, p_ssm, w_out, ln1_g, ln1_b, ffn_w1, ffn_w3, ffn_w2, ln2_g, ln2_b):
    alpha = (2.0 * DEPTH) ** 0.25
    xc = ctx
    for l in range(DEPTH):
        lp = {
            'w_in': w_in[l], 'da_lq1': da_lq1[l], 'da_lk1': da_lk1[l], 'da_lq2': da_lq2[l],
            'da_lk2': da_lk2[l], 'da_norm_g': da_norm_g[l], 'rw_mu': rw_mu[l], 'rw_w0': rw_w0[l],
            'rw_w2': rw_w2[l], 'rw_a0': rw_a0[l], 'rw_a2': rw_a2[l], 'rw_g2': rw_g2[l],
            'rw_kk': rw_kk[l], 'rw_ka': rw_ka[l], 'rw_rk': rw_rk[l], 'rw_ln_g': rw_ln_g[l],
            'rw_ln_b': rw_ln_b[l], 'ssm_conv_w': ssm_conv_w[l], 'ssm_conv_b': ssm_conv_b[l],
            'ssm_dt_bias': ssm_dt_bias[l], 'ssm_a_log': ssm_a_log[l], 'ssm_d': ssm_d[l],
            'ssm_norm_g': ssm_norm_g[l], 'p_attn': p_attn[l], 'p_rwkv': p_rwkv[l],
            'p_ssm': p_ssm[l], 'w_out': w_out[l],
        }
        last = l == DEPTH - 1
        mod_x = (jax.nn.silu(c) @ ada_w[l] + ada_b[l])[:, None, :]
        mod_c = jax.nn.silu(c_ctx) @ ada_w[l] + ada_b[l]
        sh1, sc1, g1, sh2, sc2, g2 = jnp.split(mod_x, 6, axis=-1)
        csh1, csc1, cg1, csh2, csc2, cg2 = jnp.split(mod_c, 6, axis=-1)
        out_c, side = _mixer(xc * (1.0 + csc1) + csh1, lp, l, None, not last)
        out_x, _ = _mixer(x * (1.0 + sc1) + sh1, lp, l, side, True)
        x = _layernorm(alpha * x + g1 * out_x, ln1_g[l], ln1_b[l])
        x = _layernorm(alpha * x + g2 * _swiglu(x * (1.0 + sc2) + sh2, ffn_w1[l], ffn_w3[l], ffn_w2[l]),
                       ln2_g[l], ln2_b[l])
        if not last:
            xc = _layernorm(alpha * xc + cg1 * out_c, ln1_g[l], ln1_b[l])
            xc = _layernorm(alpha * xc + cg2 * _swiglu(xc * (1.0 + csc2) + csh2, ffn_w1[l], ffn_w3[l], ffn_w2[l]),
                            ln2_g[l], ln2_b[l])
    return x
```

```python
import functools
import math

import jax
import jax.numpy as jnp
from jax import lax
from jax.experimental import pallas as pl
from jax.experimental.pallas import tpu as pltpu

F32 = jnp.float32
BF16 = jnp.bfloat16

D_MODEL = 1024
DEPTH = 4
GRID_W = 64
DA_HEADS = 4
DA_HD = 64
DA_W = DA_HEADS * 2 * DA_HD
ROPE_BASE = 10000.0
RW_HEADS = 8
RW_HD = 64
RW_W = RW_HEADS * RW_HD
RW_DECAY_R = 64
RW_ICLR_R = 64
RW_GATE_R = 128
RW_GN_EPS = 64e-5
RW_COLS = 3 * RW_W + 2 * RW_DECAY_R + 2 * RW_ICLR_R + RW_GATE_R
SSM_HEADS = 8
SSM_HD = 64
SSM_W = SSM_HEADS * SSM_HD
SSM_GROUPS = 2
SSM_STATE = 128
SSM_XBC = SSM_W + 2 * SSM_GROUPS * SSM_STATE
FFN_HIDDEN = (8 * D_MODEL + 3 * 256 - 1) // (3 * 256) * 256
NORM_EPS = 1e-5
W_IN_SPLIT = (DA_W, DA_W, DA_W, RW_COLS, SSM_W, SSM_XBC, 2 * SSM_HEADS, D_MODEL, D_MODEL, D_MODEL)
ALPHA = (2.0 * DEPTH) ** 0.25

LANES = 128
VMEM_LIMIT_BYTES = 56 * 1024 * 1024

PROJ_COLS = 8192
C_GA, C_GR, C_GS = 0, D_MODEL, 2 * D_MODEL
C_Q = 3 * D_MODEL
C_K = C_Q + DA_W
C_V = C_K + DA_W
C_Z = C_V + DA_W
C_XBC = C_Z + SSM_W
C_RW = C_XBC + SSM_XBC
C_DT = C_RW + RW_COLS

RW_CHUNK = 64
SSD_CHUNK = 128
HI = lax.Precision.HIGHEST


def _cparams(sem):
    return pltpu.CompilerParams(dimension_semantics=sem, vmem_limit_bytes=VMEM_LIMIT_BYTES)


def _dot(a, b, prec=None):
    return jnp.dot(a, b, preferred_element_type=F32, precision=prec)


def _dot_nt(a, b, prec=None):
    return lax.dot_general(a, b, (((1,), (1,)), ((), ())), preferred_element_type=F32, precision=prec)


def _dot_tn(a, b, prec=None):
    return lax.dot_general(a, b, (((0,), (0,)), ((), ())), preferred_element_type=F32, precision=prec)


def _layernorm_rows(t, g, b):
    mu = jnp.mean(t, -1, keepdims=True)
    d = t - mu
    var = jnp.mean(d * d, -1, keepdims=True)
    return d * lax.rsqrt(var + NORM_EPS) * g + b


def _inproj_kernel(x_ref, sc_ref, sh_ref, w_ref, o_ref, h_scr):
    @pl.when(pl.program_id(1) == 0)
    def _():
        h = x_ref[...] * (1.0 + sc_ref[0]) + sh_ref[0]
        h_scr[...] = h.astype(BF16)

    o_ref[...] = _dot(h_scr[...], w_ref[...])


def _inproj(x2d, sc, sh, w, seq_len):
    n = x2d.shape[0]
    tm = min(512, seq_len)
    tn = 1024
    per_b = seq_len // tm
    return pl.pallas_call(
        _inproj_kernel,
        grid=(n // tm, PROJ_COLS // tn),
        in_specs=[
            pl.BlockSpec((tm, D_MODEL), lambda i, j: (i, 0)),
            pl.BlockSpec((1, 1, D_MODEL), lambda i, j: (i // per_b, 0, 0)),
            pl.BlockSpec((1, 1, D_MODEL), lambda i, j: (i // per_b, 0, 0)),
            pl.BlockSpec((D_MODEL, tn), lambda i, j: (0, j)),
        ],
        out_specs=pl.BlockSpec((tm, tn), lambda i, j: (i, j)),
        out_shape=jax.ShapeDtypeStruct((n, PROJ_COLS), F32),
        scratch_shapes=[pltpu.VMEM((tm, D_MODEL), BF16)],
        compiler_params=_cparams(("arbitrary", "arbitrary")),
        name="inproj",
    )(x2d, sc, sh, w)


def _attn_kernel(lam_ref, q_ref, k_ref, v_ref, g_ref, o_ref, *, out_scale):
    tq = q_ref.shape[1]
    q = q_ref[0]
    lane = lax.broadcasted_iota(jnp.int32, (1, LANES), 1)
    zero = jnp.zeros_like(q)
    qq = jnp.concatenate([jnp.where(lane < DA_HD, q, zero), jnp.where(lane >= DA_HD, q, zero)], axis=0)
    s = _dot_nt(qq, k_ref[0])
    p = jnp.exp(s - jnp.max(s, -1, keepdims=True))
    inv = 1.0 / jnp.sum(p, -1, keepdims=True)
    lam = lam_ref[0]
    a = p[:tq] * inv[:tq] - lam * (p[tq:] * inv[tq:])
    o = _dot(a.astype(BF16), v_ref[0])
    ms = jnp.mean(o * o, -1, keepdims=True)
    o_ref[0] = o * lax.rsqrt(ms + NORM_EPS) * (g_ref[...] * out_scale)


def _diff_attention(lam, q, k_all, v_all, g, out_scale):
    b, lq, _ = q.shape
    lk = k_all.shape[1]
    tq = 128
    return pl.pallas_call(
        functools.partial(_attn_kernel, out_scale=out_scale),
        grid=(b, DA_HEADS, lq // tq),
        in_specs=[
            pl.BlockSpec(memory_space=pltpu.SMEM),
            pl.BlockSpec((1, tq, LANES), lambda bi, h, i: (bi, i, h)),
            pl.BlockSpec((1, lk, LANES), lambda bi, h, i: (bi, 0, h)),
            pl.BlockSpec((1, lk, LANES), lambda bi, h, i: (bi, 0, h)),
            pl.BlockSpec((1, LANES), lambda bi, h, i: (0, 0)),
        ],
        out_specs=pl.BlockSpec((1, tq, LANES), lambda bi, h, i: (bi, i, h)),
        out_shape=jax.ShapeDtypeStruct((b, lq, DA_W), F32),
        compiler_params=_cparams(("arbitrary", "arbitrary", "arbitrary")),
        name="diff_attn",
    )(lam, q, k_all, v_all, g)


def _rwkv_kernel(r_ref, v_ref, kk_ref, kd_ref, a_ref, lw_ref, s0_ref, y_ref, sf_ref, s_scr):
    d = pl.program_id(0)
    c = pl.program_id(2)
    nc = pl.num_programs(2)
    t = RW_CHUNK

    @pl.when(c == 0)
    def _():
        s_scr[...] = s0_ref[0, 0]

    sign = 1 - 2 * d
    row = lax.broadcasted_iota(jnp.int32, (t, t), 0)
    col = lax.broadcasted_iota(jnp.int32, (t, t), 1)
    delta = (row - col) * sign
    tri_incl = (delta >= 0).astype(F32)
    row2 = lax.broadcasted_iota(jnp.int32, (t, 2 * t), 0)
    col2 = lax.broadcasted_iota(jnp.int32, (t, 2 * t), 1)
    col2 = jnp.where(col2 >= t, col2 - t, col2)
    delta2 = (row2 - col2) * sign
    strict2 = delta2 > 0
    incl2 = delta2 >= 0
    colh = lax.broadcasted_iota(jnp.int32, (1, 2 * t), 1)
    cm0 = (colh < t).astype(F32)
    cm1 = 1.0 - cm0
    lane = lax.broadcasted_iota(jnp.int32, (1, LANES), 1)
    m0 = (lane < RW_HD).astype(F32)
    m1 = 1.0 - m0
    rb = lax.broadcasted_iota(jnp.int32, (LANES, LANES), 0) // RW_HD
    cb = lax.broadcasted_iota(jnp.int32, (LANES, LANES), 1) // RW_HD
    bd = (rb == cb).astype(F32)
    eye = (lax.broadcasted_iota(jnp.int32, (2 * t, 2 * t), 0)
           == lax.broadcasted_iota(jnp.int32, (2 * t, 2 * t), 1)).astype(F32)

    def stack2(x):
        return jnp.concatenate([x * m0, x * m1], axis=0)

    for j in range(RW_W // LANES):
        sl = slice(j * LANES, (j + 1) * LANES)
        r = r_ref[0, :, sl]
        v = v_ref[0, :, sl]
        kk = kk_ref[0, :, sl]
        kd = kd_ref[0, 0, :, sl]
        a = a_ref[0, 0, :, sl]
        lw = lw_ref[0, 0, :, sl]
        cum = _dot(tri_incl, lw, HI)
        tot = jnp.sum(lw, axis=0, keepdims=True)
        g_cur = jnp.exp(cum)
        g_prev = jnp.exp(cum - lw)
        g_inv = jnp.exp(-cum)
        g_rem = jnp.exp(tot - cum)
        g_tot = jnp.exp(tot)
        kt = kk * g_prev
        rt = r * g_cur
        b = kk * a
        bh = b * g_inv
        kh = kd * g_inv
        ystack = jnp.concatenate([bh * m0, bh * m1, kh * m0, kh * m1], axis=0)
        gg = _dot_nt(jnp.concatenate([kt, rt], axis=0), ystack, HI)
        mcat = jnp.where(strict2, gg[:t, :2 * t], 0.0)
        ncat = jnp.where(strict2, gg[:t, 2 * t:], 0.0)
        qcat = jnp.where(incl2, gg[t:, :2 * t], 0.0)
        pcat = jnp.where(incl2, gg[t:, 2 * t:], 0.0)
        x = -jnp.concatenate([mcat * cm0, mcat * cm1], axis=0)
        w = eye + x
        n_sq = int(math.log2(t)) - 1
        for _ in range(n_sq):
            x = _dot(x, x, HI)
            w = w + _dot(w, x, HI)
        wcat = w[:t] + w[t:]
        vst = stack2(v)
        sbd = s_scr[j]
        rhs = _dot(ncat, vst, HI) + _dot_nt(kt, sbd, HI)
        u = _dot(wcat, stack2(rhs), HI)
        y = _dot_nt(rt, sbd, HI) + _dot(pcat, vst, HI) - _dot(qcat, stack2(u), HI)
        y_ref[0, 0, :, sl] = y
        upd = _dot_tn(jnp.concatenate([v, -u], axis=0),
                      jnp.concatenate([kd * g_rem, b * g_rem], axis=0), HI)
        s_scr[j] = sbd * g_tot + bd * upd

    @pl.when(c == nc - 1)
    def _():
        sf_ref[0, 0] = s_scr[...]


def _rwkv_scan(r, v, kk, kd, a, lw, s0):
    b, l, _ = r.shape
    t = RW_CHUNK
    nc = l // t
    nhp = RW_W // LANES

    def tmap(d, bi, c):
        return (bi, c + d * (nc - 1 - 2 * c), 0)

    def dmap(d, bi, c):
        return (d, bi, c + d * (nc - 1 - 2 * c), 0)

    return pl.pallas_call(
        _rwkv_kernel,
        grid=(2, b, nc),
        in_specs=[
            pl.BlockSpec((1, t, RW_W), tmap),
            pl.BlockSpec((1, t, RW_W), tmap),
            pl.BlockSpec((1, t, RW_W), tmap),
            pl.BlockSpec((1, 1, t, RW_W), dmap),
            pl.BlockSpec((1, 1, t, RW_W), dmap),
            pl.BlockSpec((1, 1, t, RW_W), dmap),
            pl.BlockSpec((1, 1, nhp, LANES, LANES), lambda d, bi, c: (d, bi, 0, 0, 0)),
        ],
        out_specs=[
            pl.BlockSpec((1, 1, t, RW_W), dmap),
            pl.BlockSpec((1, 1, nhp, LANES, LANES), lambda d, bi, c: (d, bi, 0, 0, 0)),
        ],
        out_shape=[
            jax.ShapeDtypeStruct((2, b, l, RW_W), F32),
            jax.ShapeDtypeStruct((2, b, nhp, LANES, LANES), F32),
        ],
        scratch_shapes=[pltpu.VMEM((nhp, LANES, LANES), F32)],
        compiler_params=_cparams(("arbitrary", "arbitrary", "arbitrary")),
        name="rwkv_scan",
    )(r, v, kk, kd, a, lw, s0)


def _ssd_kernel(x_ref, a_ref, bm_ref, cm_ref, s0_ref, y_ref, sf_ref, s_scr):
    d = pl.program_id(0)
    c = pl.program_id(2)
    nc = pl.num_programs(2)
    t = SSD_CHUNK
    gw = SSM_W // SSM_GROUPS
    hpg = SSM_HEADS // SSM_GROUPS

    @pl.when(c == 0)
    def _():
        s_scr[...] = s0_ref[0, 0]

    sign = 1 - 2 * d
    row = lax.broadcasted_iota(jnp.int32, (t, t), 0)
    col = lax.broadcasted_iota(jnp.int32, (t, t), 1)
    delta = (row - col) * sign
    incl = delta >= 0
    tri_incl = incl.astype(F32)
    tri_incl_t = (delta <= 0).astype(F32)
    lane = lax.broadcasted_iota(jnp.int32, (1, LANES), 1)
    m0 = (lane < SSM_HD).astype(F32)
    m1 = 1.0 - m0
    lane_g = lax.broadcasted_iota(jnp.int32, (1, gw), 1) // SSM_HD

    a = a_ref[0, 0]
    cs = _dot(tri_incl, a, HI)
    cs_t = _dot_tn(a, tri_incl_t, HI)
    tot = jnp.sum(a, axis=0, keepdims=True)
    e_cs = jnp.exp(cs)
    e_rem = jnp.exp(tot - cs)
    e_tot = jnp.exp(tot)

    def expand(z, g):
        out = jnp.zeros((z.shape[0], gw), F32)
        for hh in range(hpg):
            h = g * hpg + hh
            out = jnp.where(lane_g == hh, z[:, h:h + 1], out)
        return out

    for g in range(SSM_GROUPS):
        cg = cm_ref[0, :, g * SSM_STATE:(g + 1) * SSM_STATE]
        bg = bm_ref[0, :, g * SSM_STATE:(g + 1) * SSM_STATE]
        xg = x_ref[0, 0, :, g * gw:(g + 1) * gw]
        gm = _dot_nt(cg, bg, HI)
        st = s_scr[g]
        y_off = _dot(cg, st, HI) * expand(e_cs, g)
        ys = []
        for hp in range(hpg // 2):
            scs = []
            for hh in range(2):
                h = g * hpg + 2 * hp + hh
                diff = jnp.where(incl, cs[:, h:h + 1] - cs_t[h:h + 1, :], 0.0)
                scs.append(gm * jnp.where(incl, jnp.exp(diff), 0.0))
            xp = xg[:, hp * LANES:(hp + 1) * LANES]
            xst = jnp.concatenate([xp * m0, xp * m1], axis=0)
            ys.append(_dot(jnp.concatenate(scs, axis=1), xst, HI))
        y_ref[0, 0, :, g * gw:(g + 1) * gw] = jnp.concatenate(ys, axis=1) + y_off
        s_scr[g] = st * expand(e_tot, g) + _dot_tn(bg, xg * expand(e_rem, g), HI)

    @pl.when(c == nc - 1)
    def _():
        sf_ref[0, 0] = s_scr[...]


def _ssd_scan(xdt, a, bm, cm, s0):
    _, b, l, _ = xdt.shape
    t = SSD_CHUNK
    nc = l // t
    gw = SSM_W // SSM_GROUPS

    def tmap(d, bi, c):
        return (bi, c + d * (nc - 1 - 2 * c), 0)

    def dmap(d, bi, c):
        return (d, bi, c + d * (nc - 1 - 2 * c), 0)

    smap = lambda d, bi, c: (d, bi, 0, 0, 0)
    return pl.pallas_call(
        _ssd_kernel,
        grid=(2, b, nc),
        in_specs=[
            pl.BlockSpec((1, 1, t, SSM_W), dmap),
            pl.BlockSpec((1, 1, t, SSM_HEADS), dmap),
            pl.BlockSpec((1, t, SSM_GROUPS * SSM_STATE), tmap),
            pl.BlockSpec((1, t, SSM_GROUPS * SSM_STATE), tmap),
            pl.BlockSpec((1, 1, SSM_GROUPS, SSM_STATE, gw), smap),
        ],
        out_specs=[
            pl.BlockSpec((1, 1, t, SSM_W), dmap),
            pl.BlockSpec((1, 1, SSM_GROUPS, SSM_STATE, gw), smap),
        ],
        out_shape=[
            jax.ShapeDtypeStruct((2, b, l, SSM_W), F32),
            jax.ShapeDtypeStruct((2, b, SSM_GROUPS, SSM_STATE, gw), F32),
        ],
        scratch_shapes=[pltpu.VMEM((SSM_GROUPS, SSM_STATE, gw), F32)],
        compiler_params=_cparams(("arbitrary", "arbitrary", "arbitrary")),
        name="ssd_scan",
    )(xdt, a, bm, cm, s0)


def _merge_kernel(ya_ref, yr_ref, ys_ref, ga_ref, gr_ref, gs_ref, x_ref, g1_ref,
                  pa_ref, pr_ref, ps_ref, wo_ref, lg_ref, lb_ref, o_ref):
    m = (jax.nn.sigmoid(ga_ref[...]) * _dot(ya_ref[...].astype(BF16), pa_ref[...])
         + jax.nn.sigmoid(gr_ref[...]) * _dot(yr_ref[...].astype(BF16), pr_ref[...])
         + jax.nn.sigmoid(gs_ref[...]) * _dot(ys_ref[...].astype(BF16), ps_ref[...]))
    out = _dot(m.astype(BF16), wo_ref[...])
    o_ref[...] = _layernorm_rows(ALPHA * x_ref[...] + g1_ref[0] * out, lg_ref[...], lb_ref[...])


def _merge(ya, yr, ys, proj, x2d, g1, pa, pr, ps, wo, lg, lb, seq_len):
    n = x2d.shape[0]
    tm = min(256, seq_len)
    per_b = seq_len // tm
    row = lambda i: (i, 0)
    full = lambda i: (0, 0)
    gate = lambda cblk: pl.BlockSpec((tm, D_MODEL), lambda i: (i, cblk))
    return pl.pallas_call(
        _merge_kernel,
        grid=(n // tm,),
        in_specs=[
            pl.BlockSpec((tm, DA_W), row), pl.BlockSpec((tm, RW_W), row), pl.BlockSpec((tm, SSM_W), row),
            gate(C_GA // D_MODEL), gate(C_GR // D_MODEL), gate(C_GS // D_MODEL),
            pl.BlockSpec((tm, D_MODEL), row),
            pl.BlockSpec((1, 1, D_MODEL), lambda i: (i // per_b, 0, 0)),
            pl.BlockSpec((DA_W, D_MODEL), full), pl.BlockSpec((RW_W, D_MODEL), full),
            pl.BlockSpec((SSM_W, D_MODEL), full), pl.BlockSpec((D_MODEL, D_MODEL), full),
            pl.BlockSpec((1, D_MODEL), full), pl.BlockSpec((1, D_MODEL), full),
        ],
        out_specs=pl.BlockSpec((tm, D_MODEL), row),
        out_shape=jax.ShapeDtypeStruct((n, D_MODEL), F32),
        compiler_params=_cparams(("arbitrary",)),
        name="merge_ln",
    )(ya, yr, ys, proj, proj, proj, x2d, g1, pa, pr, ps, wo, lg, lb)


def _ffn_kernel(x_ref, sc_ref, sh_ref, g2_ref, w1_ref, w3_ref, w2_ref, lg_ref, lb_ref, o_ref, h_scr, acc_scr):
    j = pl.program_id(1)

    @pl.when(j == 0)
    def _():
        h_scr[...] = (x_ref[...] * (1.0 + sc_ref[0]) + sh_ref[0]).astype(BF16)
        acc_scr[...] = jnp.zeros_like(acc_scr)

    h = h_scr[...]
    u = jax.nn.silu(_dot(h, w1_ref[...])) * _dot(h, w3_ref[...])
    acc_scr[...] += _dot(u.astype(BF16), w2_ref[...])

    @pl.when(j == pl.num_programs(1) - 1)
    def _():
        o_ref[...] = _layernorm_rows(ALPHA * x_ref[...] + g2_ref[0] * acc_scr[...], lg_ref[...], lb_ref[...])


def _ffn(x2d, sc, sh, g2, w1, w3, w2, lg, lb, seq_len):
    n = x2d.shape[0]
    tm = min(512, seq_len)
    th = FFN_HIDDEN // 2
    per_b = seq_len // tm
    mod = pl.BlockSpec((1, 1, D_MODEL), lambda i, j: (i // per_b, 0, 0))
    return pl.pallas_call(
        _ffn_kernel,
        grid=(n // tm, FFN_HIDDEN // th),
        in_specs=[
            pl.BlockSpec((tm, D_MODEL), lambda i, j: (i, 0)),
            mod, mod, mod,
            pl.BlockSpec((D_MODEL, th), lambda i, j: (0, j)),
            pl.BlockSpec((D_MODEL, th), lambda i, j: (0, j)),
            pl.BlockSpec((th, D_MODEL), lambda i, j: (j, 0)),
            pl.BlockSpec((1, D_MODEL), lambda i, j: (0, 0)),
            pl.BlockSpec((1, D_MODEL), lambda i, j: (0, 0)),
        ],
        out_specs=pl.BlockSpec((tm, D_MODEL), lambda i, j: (i, 0)),
        out_shape=jax.ShapeDtypeStruct((n, D_MODEL), F32),
        scratch_shapes=[pltpu.VMEM((tm, D_MODEL), BF16), pltpu.VMEM((tm, D_MODEL), F32)],
        compiler_params=_cparams(("arbitrary", "arbitrary")),
        name="ffn_ln",
    )(x2d, sc, sh, g2, w1, w3, w2, lg, lb)


def _rope_tables(n_tok):
    rows = n_tok // GRID_W
    row = jnp.repeat(jnp.arange(rows), GRID_W).astype(F32)
    col = jnp.tile(jnp.arange(GRID_W), rows).astype(F32)
    nf = DA_HD // 4
    inv = ROPE_BASE ** (-jnp.arange(nf, dtype=F32) / nf)
    ang = jnp.concatenate([row[:, None] * inv] * 2 + [col[:, None] * inv] * 2, axis=-1)
    cos = jnp.cos(ang)
    sin = jnp.sin(ang)
    first = (jnp.arange(DA_HD) % (2 * nf)) < nf
    s_dn = jnp.where(first, -sin, 0.0)
    s_up = jnp.where(first, 0.0, sin)
    reps = DA_W // DA_HD
    return jnp.tile(cos, (1, reps)), jnp.tile(s_dn, (1, reps)), jnp.tile(s_up, (1, reps))


def _rope(t, tabs):
    cos, s_dn, s_up = tabs
    nf = DA_HD // 4
    return t * cos + jnp.roll(t, -nf, axis=-1) * s_dn + jnp.roll(t, nf, axis=-1) * s_up


def _centred_shift(u):
    up = jnp.pad(u, ((0, 0), (1, 1), (0, 0)))
    return 0.5 * (up[:, :-2] + up[:, 2:])


def _rwkv_pre(u, lp):
    b, l, _ = u.shape
    h, n = RW_HEADS, RW_HD
    u = u + lp['rw_mu'] * (_centred_shift(u) - u)
    r, k, v = u[..., :RW_W], u[..., RW_W:2 * RW_W], u[..., 2 * RW_W:3 * RW_W]
    o = 3 * RW_W
    wd = jnp.tanh(u[..., o:o + 2 * RW_DECAY_R]).reshape(b, l, 2, RW_DECAY_R)
    o += 2 * RW_DECAY_R
    ad = u[..., o:o + 2 * RW_ICLR_R].reshape(b, l, 2, RW_ICLR_R)
    o += 2 * RW_ICLR_R
    gd = u[..., o:]
    w_raw = lp['rw_w0'] + jnp.einsum('bldr,drc->bldc', wd, lp['rw_w2'])
    lw = -jnp.exp(-jax.nn.softplus(-w_raw) - 0.5)
    a = jax.nn.sigmoid(lp['rw_a0'] + jnp.einsum('bldr,drc->bldc', ad, lp['rw_a2']))
    kk = (k * lp['rw_kk']).reshape(b, l, h, n)
    kk = kk * lax.rsqrt(jnp.maximum(jnp.sum(jnp.square(kk), -1, keepdims=True), 1e-24))
    kk = kk.reshape(b, l, RW_W)
    kd = k[:, :, None] * (1.0 + (a - 1.0) * lp['rw_ka'])
    tr = lambda z: jnp.moveaxis(z, 2, 0)
    return r, k, v, kk, tr(kd), tr(a), tr(lw), gd


def _rwkv_post(y2, r, v, kd, gd, lp):
    b, l, _ = r.shape
    h, n = RW_HEADS, RW_HD
    y = (y2[0] + y2[1]).reshape(b, l, h, n)
    mu = jnp.mean(y, -1, keepdims=True)
    var = jnp.mean(jnp.square(y - mu), -1, keepdims=True)
    y = ((y - mu) * lax.rsqrt(var + RW_GN_EPS)).reshape(b, l, RW_W) * lp['rw_ln_g'] + lp['rw_ln_b']
    k_bonus = (0.5 * (kd[0] + kd[1])).reshape(b, l, h, n)
    bonus = (jnp.sum(r.reshape(b, l, h, n) * k_bonus * lp['rw_rk'], -1, keepdims=True)
             * v.reshape(b, l, h, n)).reshape(b, l, RW_W)
    g = jnp.einsum('blr,rc->blc', jax.nn.sigmoid(gd), lp['rw_g2'])
    return (y + bonus) * g


def _ssm_pre(xbc, dt_raw, lp):
    b, l, _ = xbc.shape
    w = lp['ssm_conv_w']
    xp = jnp.pad(xbc, ((0, 0), (1, 1), (0, 0)))
    conv = xp[:, :-2] * w[0] + xp[:, 1:-1] * w[1] + xp[:, 2:] * w[2] + lp['ssm_conv_b']
    xbc = jax.nn.silu(conv)
    xs = xbc[..., :SSM_W]
    bm = xbc[..., SSM_W:SSM_W + SSM_GROUPS * SSM_STATE]
    cm = xbc[..., SSM_W + SSM_GROUPS * SSM_STATE:]
    dt = jax.nn.softplus(dt_raw.reshape(b, l, 2, SSM_HEADS) + lp['ssm_dt_bias'])
    a_neg = -jnp.exp(lp['ssm_a_log'])
    da = jnp.moveaxis(dt * a_neg, 2, 0)
    dt = jnp.moveaxis(dt, 2, 0)
    xdt = (xs.reshape(b, l, SSM_HEADS, SSM_HD)[None] * dt[..., None]).reshape(2, b, l, SSM_W)
    return xs, bm, cm, xdt, da


def _ssm_post(y2, xs, z, lp):
    b, l, _ = xs.shape
    dskip = jnp.repeat(lp['ssm_d'], SSM_HD)
    y = (y2[0] + y2[1] + dskip * xs) * jax.nn.silu(z)
    y = y.reshape(b, l, SSM_GROUPS, SSM_W // SSM_GROUPS)
    y = y * lax.rsqrt(jnp.mean(jnp.square(y), -1, keepdims=True) + NORM_EPS)
    return y.reshape(b, l, SSM_W) * lp['ssm_norm_g']


def _permute_w_in(w_in):
    parts, acc = [], 0
    for s in W_IN_SPLIT:
        parts.append(w_in[:, acc:acc + s])
        acc += s
    q, k, v, rw, z, xbc, dt, ga, gr, gs = parts
    pad = jnp.zeros((w_in.shape[0], PROJ_COLS - C_DT - dt.shape[1]), w_in.dtype)
    return jnp.concatenate([ga, gr, gs, q, k, v, z, xbc, rw, dt, pad], axis=1).astype(BF16)


def _mixer(x, mod, lp, layer, side, want_out, rope_tabs):
    b, l, _ = x.shape
    sh1, sc1, g1 = mod
    x2d = x.reshape(b * l, D_MODEL)
    proj = _inproj(x2d, sc1, sh1, lp['w_in_p'], l)
    p3 = proj.reshape(b, l, PROJ_COLS)
    q, k, v = p3[..., C_Q:C_K], p3[..., C_K:C_V], p3[..., C_V:C_Z]
    lam_init = 0.8 - 0.6 * math.exp(-0.3 * layer)
    lam = (jnp.exp(jnp.sum(lp['da_lq1'] * lp['da_lk1'])) - jnp.exp(jnp.sum(lp['da_lq2'] * lp['da_lk2']))
           + lam_init).reshape(1)
    if side is None:
        k_all, v_all = k.astype(BF16), v.astype(BF16)
        side_a = (k_all, v_all)
        s0_r = jnp.zeros((2, b, RW_W // LANES, LANES, LANES), F32)
        s0_s = jnp.zeros((2, b, SSM_GROUPS, SSM_STATE, SSM_W // SSM_GROUPS), F32)
    else:
        (kc, vc), s0_r, s0_s = side
        q = _rope(q, rope_tabs)
        k_all = jnp.concatenate([kc, _rope(k, rope_tabs).astype(BF16)], axis=1)
        v_all = jnp.concatenate([vc, v.astype(BF16)], axis=1)
        side_a = None
    r, kr, vr, kk, kd, a, lw, gd = _rwkv_pre(p3[..., C_RW:C_DT], lp)
    y_r2, sf_r = _rwkv_scan(r, vr, kk, kd, a, lw, s0_r)
    xs, bm, cm, xdt, da = _ssm_pre(p3[..., C_XBC:C_RW], p3[..., C_DT:C_DT + 2 * SSM_HEADS], lp)
    y_s2, sf_s = _ssd_scan(xdt, da, bm, cm, s0_s)
    new_side = (side_a, sf_r, sf_s)
    if not want_out:
        return None, new_side
    q = (q * (DA_HD ** -0.5)).astype(BF16)
    y_a = _diff_attention(lam, q, k_all, v_all, lp['da_norm_g'].reshape(1, 2 * DA_HD), 1.0 - lam_init)
    y_r = _rwkv_post(y_r2, r, vr, kd, gd, lp)
    y_s = _ssm_post(y_s2, xs, p3[..., C_Z:C_XBC], lp)
    n = b * l
    x1 = _merge(y_a.reshape(n, DA_W), y_r.reshape(n, RW_W), y_s.reshape(n, SSM_W), proj, x2d, g1,
                lp['p_attn'], lp['p_rwkv'], lp['p_ssm'], lp['w_out'], lp['ln1_g'], lp['ln1_b'], l)
    return x1, new_side


def kernel(x, c, ctx, c_ctx, ada_w, ada_b, w_in, da_lq1, da_lk1, da_lq2, da_lk2, da_norm_g,
           rw_mu, rw_w0, rw_w2, rw_a0, rw_a2, rw_g2, rw_kk, rw_ka, rw_rk, rw_ln_g, rw_ln_b,
           ssm_conv_w, ssm_conv_b, ssm_dt_bias, ssm_a_log, ssm_d, ssm_norm_g,
           p_attn, p_rwkv, p_ssm, w_out, ln1_g, ln1_b, ffn_w1, ffn_w3, ffn_w2, ln2_g, ln2_b):
    b, l, _ = x.shape
    lc = ctx.shape[1]
    rope_tabs = _rope_tables(l)
    xc = ctx
    for i in range(DEPTH):
        lp = {
            'w_in_p': _permute_w_in(w_in[i]),
            'da_lq1': da_lq1[i], 'da_lk1': da_lk1[i], 'da_lq2': da_lq2[i], 'da_lk2': da_lk2[i],
            'da_norm_g': da_norm_g[i], 'rw_mu': rw_mu[i], 'rw_w0': rw_w0[i], 'rw_w2': rw_w2[i],
            'rw_a0': rw_a0[i], 'rw_a2': rw_a2[i], 'rw_g2': rw_g2[i], 'rw_kk': rw_kk[i],
            'rw_ka': rw_ka[i], 'rw_rk': rw_rk[i], 'rw_ln_g': rw_ln_g[i], 'rw_ln_b': rw_ln_b[i],
            'ssm_conv_w': ssm_conv_w[i], 'ssm_conv_b': ssm_conv_b[i], 'ssm_dt_bias': ssm_dt_bias[i],
            'ssm_a_log': ssm_a_log[i], 'ssm_d': ssm_d[i], 'ssm_norm_g': ssm_norm_g[i],
            'p_attn': p_attn[i].astype(BF16), 'p_rwkv': p_rwkv[i].astype(BF16),
            'p_ssm': p_ssm[i].astype(BF16), 'w_out': w_out[i].astype(BF16),
            'ln1_g': ln1_g[i].reshape(1, D_MODEL), 'ln1_b': ln1_b[i].reshape(1, D_MODEL),
        }
        w1, w3, w2 = ffn_w1[i].astype(BF16), ffn_w3[i].astype(BF16), ffn_w2[i].astype(BF16)
        l2g, l2b = ln2_g[i].reshape(1, D_MODEL), ln2_b[i].reshape(1, D_MODEL)
        last = i == DEPTH - 1
        mod_x = (jax.nn.silu(c) @ ada_w[i] + ada_b[i])[:, None, :]
        mod_c = jnp.broadcast_to((jax.nn.silu(c_ctx) @ ada_w[i] + ada_b[i])[None, None, :], (b, 1, 6 * D_MODEL))
        sh1, sc1, g1, sh2, sc2, g2 = jnp.split(mod_x, 6, axis=-1)
        csh1, csc1, cg1, csh2, csc2, cg2 = jnp.split(mod_c, 6, axis=-1)
        xc1, side = _mixer(xc, (csh1, csc1, cg1), lp, i, None, not last, None)
        x1, _ = _mixer(x, (sh1, sc1, g1), lp, i, side, True, rope_tabs)
        x = _ffn(x1, sc2, sh2, g2, w1, w3, w2, l2g, l2b, l).reshape(b, l, D_MODEL)
        if not last:
            xc = _ffn(xc1, csc2, csh2, cg2, w1, w3, w2, l2g, l2b, lc).reshape(b, lc, D_MODEL)
    return x
```

```python
import functools
import math

import jax
import jax.numpy as jnp
from jax import lax
from jax.experimental import pallas as pl
from jax.experimental.pallas import tpu as pltpu

F32 = jnp.float32
BF16 = jnp.bfloat16

D_MODEL = 1024
DEPTH = 4
GRID_W = 64
DA_HEADS = 4
DA_HD = 64
DA_W = DA_HEADS * 2 * DA_HD
ROPE_BASE = 10000.0
RW_HEADS = 8
RW_HD = 64
RW_W = RW_HEADS * RW_HD
RW_DECAY_R = 64
RW_ICLR_R = 64
RW_GATE_R = 128
RW_GN_EPS = 64e-5
RW_COLS = 3 * RW_W + 2 * RW_DECAY_R + 2 * RW_ICLR_R + RW_GATE_R
SSM_HEADS = 8
SSM_HD = 64
SSM_W = SSM_HEADS * SSM_HD
SSM_GROUPS = 2
SSM_STATE = 128
SSM_XBC = SSM_W + 2 * SSM_GROUPS * SSM_STATE
FFN_HIDDEN = (8 * D_MODEL + 3 * 256 - 1) // (3 * 256) * 256
NORM_EPS = 1e-5
W_IN_SPLIT = (DA_W, DA_W, DA_W, RW_COLS, SSM_W, SSM_XBC, 2 * SSM_HEADS, D_MODEL, D_MODEL, D_MODEL)
ALPHA = (2.0 * DEPTH) ** 0.25

LANES = 128
VMEM_LIMIT_BYTES = 56 * 1024 * 1024

PROJ_COLS = 8192
C_GA, C_GR, C_GS = 0, D_MODEL, 2 * D_MODEL
C_Q = 3 * D_MODEL
C_K = C_Q + DA_W
C_V = C_K + DA_W
C_Z = C_V + DA_W
C_XBC = C_Z + SSM_W
C_RW = C_XBC + SSM_XBC
C_DT = C_RW + RW_COLS

RW_CHUNK = 64
RW_BLOCK = 128
SSD_CHUNK = 128


def _cparams(sem):
    return pltpu.CompilerParams(dimension_semantics=sem, vmem_limit_bytes=VMEM_LIMIT_BYTES)


def _dot(a, b, prec=None):
    return jnp.dot(a, b, preferred_element_type=F32, precision=prec)


def _dot_nt(a, b, prec=None):
    return lax.dot_general(a, b, (((1,), (1,)), ((), ())), preferred_element_type=F32, precision=prec)


def _dot_tn(a, b, prec=None):
    return lax.dot_general(a, b, (((0,), (0,)), ((), ())), preferred_element_type=F32, precision=prec)


def _bf(x):
    return x.astype(BF16)


def _split3(x):
    hi = _bf(x)
    r1 = x - hi.astype(F32)
    mid = _bf(r1)
    lo = _bf(r1 - mid.astype(F32))
    return hi, mid, lo


def _dot3(a, b):
    ah, am, _ = _split3(a)
    bh, bm, _ = _split3(b)
    return _dot(ah, bh) + (_dot(ah, bm) + _dot(am, bh))


def _cumsum_rows(tri_b, x):
    n = x.shape[1]
    hi, mid, lo = _split3(x)
    o = _dot(tri_b, jnp.concatenate([hi, mid, lo], axis=1))
    return o[:, :n] + o[:, n:2 * n] + o[:, 2 * n:]


def _layernorm_rows(t, g, b):
    mu = jnp.mean(t, -1, keepdims=True)
    d = t - mu
    var = jnp.mean(d * d, -1, keepdims=True)
    return d * lax.rsqrt(var + NORM_EPS) * g + b


def _inproj_kernel(x_ref, sc_ref, sh_ref, w_ref, o_ref, h_scr):
    @pl.when(pl.program_id(1) == 0)
    def _():
        h = x_ref[...] * (1.0 + sc_ref[0]) + sh_ref[0]
        h_scr[...] = h.astype(BF16)

    o_ref[...] = _dot(h_scr[...], w_ref[...])


def _inproj(x2d, sc, sh, w, seq_len):
    n = x2d.shape[0]
    tm = min(512, seq_len)
    tn = 1024
    per_b = seq_len // tm
    return pl.pallas_call(
        _inproj_kernel,
        grid=(n // tm, PROJ_COLS // tn),
        in_specs=[
            pl.BlockSpec((tm, D_MODEL), lambda i, j: (i, 0)),
            pl.BlockSpec((1, 1, D_MODEL), lambda i, j: (i // per_b, 0, 0)),
            pl.BlockSpec((1, 1, D_MODEL), lambda i, j: (i // per_b, 0, 0)),
            pl.BlockSpec((D_MODEL, tn), lambda i, j: (0, j)),
        ],
        out_specs=pl.BlockSpec((tm, tn), lambda i, j: (i, j)),
        out_shape=jax.ShapeDtypeStruct((n, PROJ_COLS), F32),
        scratch_shapes=[pltpu.VMEM((tm, D_MODEL), BF16)],
        compiler_params=_cparams(("arbitrary", "arbitrary")),
        name="inproj",
    )(x2d, sc, sh, w)


def _attn_kernel(lam_ref, q_ref, k_ref, v_ref, g_ref, o_ref, *, out_scale):
    tq = q_ref.shape[1]
    q = q_ref[0]
    lane = lax.broadcasted_iota(jnp.int32, (1, LANES), 1)
    zero = jnp.zeros_like(q)
    qq = jnp.concatenate([jnp.where(lane < DA_HD, q, zero), jnp.where(lane >= DA_HD, q, zero)], axis=0)
    s = _dot_nt(qq, k_ref[0])
    p = jnp.exp(s - jnp.max(s, -1, keepdims=True))
    inv = 1.0 / jnp.sum(p, -1, keepdims=True)
    lam = lam_ref[0]
    a = p[:tq] * inv[:tq] - lam * (p[tq:] * inv[tq:])
    o = _dot(a.astype(BF16), v_ref[0])
    ms = jnp.mean(o * o, -1, keepdims=True)
    o_ref[0] = o * lax.rsqrt(ms + NORM_EPS) * (g_ref[...] * out_scale)


def _diff_attention(lam, q, k_all, v_all, g, out_scale):
    b, lq, _ = q.shape
    lk = k_all.shape[1]
    tq = 128
    return pl.pallas_call(
        functools.partial(_attn_kernel, out_scale=out_scale),
        grid=(b, DA_HEADS, lq // tq),
        in_specs=[
            pl.BlockSpec(memory_space=pltpu.SMEM),
            pl.BlockSpec((1, tq, LANES), lambda bi, h, i: (bi, i, h)),
            pl.BlockSpec((1, lk, LANES), lambda bi, h, i: (bi, 0, h)),
            pl.BlockSpec((1, lk, LANES), lambda bi, h, i: (bi, 0, h)),
            pl.BlockSpec((1, LANES), lambda bi, h, i: (0, 0)),
        ],
        out_specs=pl.BlockSpec((1, tq, LANES), lambda bi, h, i: (bi, i, h)),
        out_shape=jax.ShapeDtypeStruct((b, lq, DA_W), F32),
        compiler_params=_cparams(("arbitrary", "arbitrary", "arbitrary")),
        name="diff_attn",
    )(lam, q, k_all, v_all, g)


def _rwkv_kernel(r_ref, v_ref, kk_ref, kd_ref, a_ref, lw_ref, s0_ref, y_ref, sf_ref, s_scr):
    d = pl.program_id(0)
    c = pl.program_id(2)
    nc = pl.num_programs(2)
    t = RW_CHUNK
    nper = r_ref.shape[1] // t

    @pl.when(c == 0)
    def _():
        s_scr[...] = s0_ref[0, 0]

    sign = 1 - 2 * d
    row = lax.broadcasted_iota(jnp.int32, (t, t), 0)
    col = lax.broadcasted_iota(jnp.int32, (t, t), 1)
    tri_b = ((row - col) * sign >= 0).astype(BF16)
    row2 = lax.broadcasted_iota(jnp.int32, (t, 2 * t), 0)
    col2 = lax.broadcasted_iota(jnp.int32, (t, 2 * t), 1)
    col2 = jnp.where(col2 >= t, col2 - t, col2)
    delta2 = (row2 - col2) * sign
    strict2 = delta2 > 0
    incl2 = delta2 >= 0
    colh = lax.broadcasted_iota(jnp.int32, (1, 2 * t), 1)
    cm0 = (colh < t).astype(F32)
    cm1 = 1.0 - cm0
    lane = lax.broadcasted_iota(jnp.int32, (1, LANES), 1)
    m0 = (lane < RW_HD).astype(F32)
    m1 = 1.0 - m0
    rb = lax.broadcasted_iota(jnp.int32, (LANES, LANES), 0) // RW_HD
    cb = lax.broadcasted_iota(jnp.int32, (LANES, LANES), 1) // RW_HD
    bd = (rb == cb).astype(F32)
    eye = (lax.broadcasted_iota(jnp.int32, (2 * t, 2 * t), 0)
           == lax.broadcasted_iota(jnp.int32, (2 * t, 2 * t), 1)).astype(F32)

    def stack2(x):
        return _bf(jnp.concatenate([x * m0, x * m1], axis=0))

    nhp = RW_W // LANES
    inst = [(j, i) for i in range(nper) for j in range(nhp)]

    def rows_of(i):
        return pl.ds(pl.multiple_of((i + d * (nper - 1 - 2 * i)) * t, t), t)

    def sl_of(j):
        return slice(j * LANES, (j + 1) * LANES)

    lws = [lw_ref[0, 0, rows_of(i), sl_of(j)] for j, i in inst]
    cums = [_cumsum_rows(tri_b, lw) for lw in lws]
    pre = []
    for (j, i), lw, cum in zip(inst, lws, cums):
        rows, sl = rows_of(i), sl_of(j)
        r = r_ref[0, rows, sl]
        v = v_ref[0, rows, sl]
        kk = kk_ref[0, rows, sl]
        kd = kd_ref[0, 0, rows, sl]
        a = a_ref[0, 0, rows, sl]
        tot = jnp.sum(lw, axis=0, keepdims=True)
        g_inv = jnp.exp(-cum)
        g_rem = jnp.exp(tot - cum)
        rt = r * jnp.exp(cum)
        kt = _bf(kk * jnp.exp(cum - lw))
        b = kk * a
        bh = b * g_inv
        kh = kd * g_inv
        ystack = _bf(jnp.concatenate([bh * m0, bh * m1, kh * m0, kh * m1], axis=0))
        pre.append(dict(v=v, rt=rt, kt=kt, ystack=ystack, g_tot=jnp.exp(tot), vst=stack2(v),
                        rhs_r=_bf(jnp.concatenate([kd * g_rem, b * g_rem], axis=0))))
    ggs = [_dot_nt(jnp.concatenate([p['kt'], _bf(p['rt'])], axis=0), p['ystack']) for p in pre]
    for p, gg in zip(pre, ggs):
        mcat = jnp.where(strict2, gg[:t, :2 * t], 0.0)
        p['ncat'] = _bf(jnp.where(strict2, gg[:t, 2 * t:], 0.0))
        p['qcat'] = _bf(jnp.where(incl2, gg[t:, :2 * t], 0.0))
        p['pcat'] = _bf(jnp.where(incl2, gg[t:, 2 * t:], 0.0))
        p['x0'] = -jnp.concatenate([mcat * cm0, mcat * cm1], axis=0)
    n_dbl = int(math.log2(t))
    ws = [eye + p['x0'] for p in pre]
    xs = [_dot(_bf(p['x0']), _bf(p['x0'])) for p in pre]
    nvs = [_dot(p['ncat'], p['vst']) for p in pre]
    for step in range(1, n_dbl):
        last = step == n_dbl - 1
        outs = [_dot(_bf(w if last else jnp.concatenate([w, x], axis=0)), _bf(x)) for w, x in zip(ws, xs)]
        ws = [w + o[:2 * t] for w, o in zip(ws, outs)]
        if not last:
            xs = [o[2 * t:] for o in outs]
    errs = [eye - _dot3(eye - p['x0'], w) for p, w in zip(pre, ws)]
    ws = [w + _dot(_bf(w), _bf(e)) for w, e in zip(ws, errs)]
    wcats = [_bf(w[:t] + w[t:]) for w in ws]
    uks = [_dot(wc, jnp.concatenate([stack2(nv), stack2(p['kt'].astype(F32))], axis=1))
           for wc, nv, p in zip(wcats, nvs, pre)]
    qus = [_dot(p['qcat'], jnp.concatenate([stack2(uk[:, :LANES]), stack2(uk[:, LANES:])], axis=1))
           for p, uk in zip(pre, uks)]
    pvs = [_dot(p['pcat'], p['vst']) for p in pre]
    zeros_t = jnp.zeros((t, LANES), F32)
    dcs = [_dot_tn(_bf(jnp.concatenate([jnp.concatenate([p['v'], -uk[:, :LANES]], axis=0),
                                        jnp.concatenate([zeros_t, uk[:, LANES:]], axis=0)], axis=1)), p['rhs_r'])
           for p, uk in zip(pre, uks)]
    for p, qu, pv, dc in zip(pre, qus, pvs, dcs):
        p['y0'] = pv - qu[:, :LANES]
        p['rp'] = _bf(p['rt'] - qu[:, LANES:])
        p['d0'] = bd * dc[:LANES]
        p['c0'] = _bf(bd * dc[LANES:])
    states = [s_scr[j] for j in range(nhp)]
    for i in range(nper):
        ps = [pre[i * nhp + j] for j in range(nhp)]
        sbs = [_bf(sbd) for sbd in states]
        ys = [_dot_nt(p['rp'], sb) for p, sb in zip(ps, sbs)]
        scs = [_dot(sb, p['c0']) for p, sb in zip(ps, sbs)]
        for j in range(nhp):
            y_ref[0, 0, rows_of(i), sl_of(j)] = ps[j]['y0'] + ys[j]
        states = [sbd * p['g_tot'] - sc + p['d0'] for sbd, p, sc in zip(states, ps, scs)]
    for j in range(nhp):
        s_scr[j] = states[j]

    @pl.when(c == nc - 1)
    def _():
        sf_ref[0, 0] = s_scr[...]


def _rwkv_scan(r, v, kk, kd, a, lw, s0):
    b, l, _ = r.shape
    t = min(RW_BLOCK, l)
    nc = l // t
    nhp = RW_W // LANES

    def tmap(d, bi, c):
        return (bi, c + d * (nc - 1 - 2 * c), 0)

    def dmap(d, bi, c):
        return (d, bi, c + d * (nc - 1 - 2 * c), 0)

    return pl.pallas_call(
        _rwkv_kernel,
        grid=(2, b, nc),
        in_specs=[
            pl.BlockSpec((1, t, RW_W), tmap),
            pl.BlockSpec((1, t, RW_W), tmap),
            pl.BlockSpec((1, t, RW_W), tmap),
            pl.BlockSpec((1, 1, t, RW_W), dmap),
            pl.BlockSpec((1, 1, t, RW_W), dmap),
            pl.BlockSpec((1, 1, t, RW_W), dmap),
            pl.BlockSpec((1, 1, nhp, LANES, LANES), lambda d, bi, c: (d, bi, 0, 0, 0)),
        ],
        out_specs=[
            pl.BlockSpec((1, 1, t, RW_W), dmap),
            pl.BlockSpec((1, 1, nhp, LANES, LANES), lambda d, bi, c: (d, bi, 0, 0, 0)),
        ],
        out_shape=[
            jax.ShapeDtypeStruct((2, b, l, RW_W), F32),
            jax.ShapeDtypeStruct((2, b, nhp, LANES, LANES), F32),
        ],
        scratch_shapes=[pltpu.VMEM((nhp, LANES, LANES), F32)],
        compiler_params=_cparams(("arbitrary", "arbitrary", "arbitrary")),
        name="rwkv_scan",
    )(r, v, kk, kd, a, lw, s0)


def _ssd_kernel(x_ref, a_ref, bm_ref, cm_ref, s0_ref, y_ref, sf_ref, s_scr):
    d = pl.program_id(0)
    c = pl.program_id(2)
    nc = pl.num_programs(2)
    t = SSD_CHUNK
    gw = SSM_W // SSM_GROUPS
    hpg = SSM_HEADS // SSM_GROUPS

    @pl.when(c == 0)
    def _():
        s_scr[...] = s0_ref[0, 0]

    sign = 1 - 2 * d
    row = lax.broadcasted_iota(jnp.int32, (t, t), 0)
    col = lax.broadcasted_iota(jnp.int32, (t, t), 1)
    delta = (row - col) * sign
    incl = delta >= 0
    tri_b = incl.astype(BF16)
    tri_t_b = (delta <= 0).astype(BF16)
    lane = lax.broadcasted_iota(jnp.int32, (1, LANES), 1)
    m0 = (lane < SSM_HD).astype(F32)
    m1 = 1.0 - m0
    lane_g = lax.broadcasted_iota(jnp.int32, (1, gw), 1) // SSM_HD

    a = a_ref[0, 0]
    nh = SSM_HEADS
    a3 = jnp.concatenate(_split3(a), axis=1)
    cs3 = _dot(tri_b, a3)
    cs = cs3[:, :nh] + cs3[:, nh:2 * nh] + cs3[:, 2 * nh:]
    cs3_t = _dot_tn(a3, tri_t_b)
    cs_t = cs3_t[:nh] + cs3_t[nh:2 * nh] + cs3_t[2 * nh:]
    tot = jnp.sum(a, axis=0, keepdims=True)
    e_cs = jnp.exp(cs)
    e_rem = jnp.exp(tot - cs)
    e_tot = jnp.exp(tot)

    def expand(z, g):
        out = jnp.zeros((z.shape[0], gw), F32)
        for hh in range(hpg):
            h = g * hpg + hh
            out = jnp.where(lane_g == hh, z[:, h:h + 1], out)
        return out

    for g in range(SSM_GROUPS):
        cg = cm_ref[0, :, g * SSM_STATE:(g + 1) * SSM_STATE]
        bg = bm_ref[0, :, g * SSM_STATE:(g + 1) * SSM_STATE]
        xg = x_ref[0, 0, :, g * gw:(g + 1) * gw]
        cgb = _bf(cg)
        bgb = _bf(bg)
        gm = _dot_nt(cgb, bgb)
        st = s_scr[g]
        y_off = _dot(cgb, _bf(st)) * expand(e_cs, g)
        ys = []
        for hp in range(hpg // 2):
            scs = []
            for hh in range(2):
                h = g * hpg + 2 * hp + hh
                diff = jnp.where(incl, cs[:, h:h + 1] - cs_t[h:h + 1, :], 0.0)
                scs.append(gm * jnp.where(incl, jnp.exp(diff), 0.0))
            xp = xg[:, hp * LANES:(hp + 1) * LANES]
            xst = _bf(jnp.concatenate([xp * m0, xp * m1], axis=0))
            ys.append(_dot(_bf(jnp.concatenate(scs, axis=1)), xst))
        y_ref[0, 0, :, g * gw:(g + 1) * gw] = jnp.concatenate(ys, axis=1) + y_off
        s_scr[g] = st * expand(e_tot, g) + _dot_tn(bgb, _bf(xg * expand(e_rem, g)))

    @pl.when(c == nc - 1)
    def _():
        sf_ref[0, 0] = s_scr[...]


def _ssd_scan(xdt, a, bm, cm, s0):
    _, b, l, _ = xdt.shape
    t = SSD_CHUNK
    nc = l // t
    gw = SSM_W // SSM_GROUPS

    def tmap(d, bi, c):
        return (bi, c + d * (nc - 1 - 2 * c), 0)

    def dmap(d, bi, c):
        return (d, bi, c + d * (nc - 1 - 2 * c), 0)

    smap = lambda d, bi, c: (d, bi, 0, 0, 0)
    return pl.pallas_call(
        _ssd_kernel,
        grid=(2, b, nc),
        in_specs=[
            pl.BlockSpec((1, 1, t, SSM_W), dmap),
            pl.BlockSpec((1, 1, t, SSM_HEADS), dmap),
            pl.BlockSpec((1, t, SSM_GROUPS * SSM_STATE), tmap),
            pl.BlockSpec((1, t, SSM_GROUPS * SSM_STATE), tmap),
            pl.BlockSpec((1, 1, SSM_GROUPS, SSM_STATE, gw), smap),
        ],
        out_specs=[
            pl.BlockSpec((1, 1, t, SSM_W), dmap),
            pl.BlockSpec((1, 1, SSM_GROUPS, SSM_STATE, gw), smap),
        ],
        out_shape=[
            jax.ShapeDtypeStruct((2, b, l, SSM_W), F32),
            jax.ShapeDtypeStruct((2, b, SSM_GROUPS, SSM_STATE, gw), F32),
        ],
        scratch_shapes=[pltpu.VMEM((SSM_GROUPS, SSM_STATE, gw), F32)],
        compiler_params=_cparams(("arbitrary", "arbitrary", "arbitrary")),
        name="ssd_scan",
    )(xdt, a, bm, cm, s0)


def _merge_kernel(ya_ref, yr_ref, ys_ref, ga_ref, gr_ref, gs_ref, x_ref, g1_ref,
                  pa_ref, pr_ref, ps_ref, wo_ref, lg_ref, lb_ref, o_ref):
    m = (jax.nn.sigmoid(ga_ref[...]) * _dot(ya_ref[...].astype(BF16), pa_ref[...])
         + jax.nn.sigmoid(gr_ref[...]) * _dot(yr_ref[...].astype(BF16), pr_ref[...])
         + jax.nn.sigmoid(gs_ref[...]) * _dot(ys_ref[...].astype(BF16), ps_ref[...]))
    out = _dot(m.astype(BF16), wo_ref[...])
    o_ref[...] = _layernorm_rows(ALPHA * x_ref[...] + g1_ref[0] * out, lg_ref[...], lb_ref[...])


def _merge(ya, yr, ys, proj, x2d, g1, pa, pr, ps, wo, lg, lb, seq_len):
    n = x2d.shape[0]
    tm = min(256, seq_len)
    per_b = seq_len // tm
    row = lambda i: (i, 0)
    full = lambda i: (0, 0)
    gate = lambda cblk: pl.BlockSpec((tm, D_MODEL), lambda i: (i, cblk))
    return pl.pallas_call(
        _merge_kernel,
        grid=(n // tm,),
        in_specs=[
            pl.BlockSpec((tm, DA_W), row), pl.BlockSpec((tm, RW_W), row), pl.BlockSpec((tm, SSM_W), row),
            gate(C_GA // D_MODEL), gate(C_GR // D_MODEL), gate(C_GS // D_MODEL),
            pl.BlockSpec((tm, D_MODEL), row),
            pl.BlockSpec((1, 1, D_MODEL), lambda i: (i // per_b, 0, 0)),
            pl.BlockSpec((DA_W, D_MODEL), full), pl.BlockSpec((RW_W, D_MODEL), full),
            pl.BlockSpec((SSM_W, D_MODEL), full), pl.BlockSpec((D_MODEL, D_MODEL), full),
            pl.BlockSpec((1, D_MODEL), full), pl.BlockSpec((1, D_MODEL), full),
        ],
        out_specs=pl.BlockSpec((tm, D_MODEL), row),
        out_shape=jax.ShapeDtypeStruct((n, D_MODEL), F32),
        compiler_params=_cparams(("arbitrary",)),
        name="merge_ln",
    )(ya, yr, ys, proj, proj, proj, x2d, g1, pa, pr, ps, wo, lg, lb)


def _ffn_kernel(x_ref, sc_ref, sh_ref, g2_ref, w1_ref, w3_ref, w2_ref, lg_ref, lb_ref, o_ref, h_scr, acc_scr):
    j = pl.program_id(1)

    @pl.when(j == 0)
    def _():
        h_scr[...] = (x_ref[...] * (1.0 + sc_ref[0]) + sh_ref[0]).astype(BF16)
        acc_scr[...] = jnp.zeros_like(acc_scr)

    h = h_scr[...]
    u = jax.nn.silu(_dot(h, w1_ref[...])) * _dot(h, w3_ref[...])
    acc_scr[...] += _dot(u.astype(BF16), w2_ref[...])

    @pl.when(j == pl.num_programs(1) - 1)
    def _():
        o_ref[...] = _layernorm_rows(ALPHA * x_ref[...] + g2_ref[0] * acc_scr[...], lg_ref[...], lb_ref[...])


def _ffn(x2d, sc, sh, g2, w1, w3, w2, lg, lb, seq_len):
    n = x2d.shape[0]
    tm = min(512, seq_len)
    th = FFN_HIDDEN // 2
    per_b = seq_len // tm
    mod = pl.BlockSpec((1, 1, D_MODEL), lambda i, j: (i // per_b, 0, 0))
    return pl.pallas_call(
        _ffn_kernel,
        grid=(n // tm, FFN_HIDDEN // th),
        in_specs=[
            pl.BlockSpec((tm, D_MODEL), lambda i, j: (i, 0)),
            mod, mod, mod,
            pl.BlockSpec((D_MODEL, th), lambda i, j: (0, j)),
            pl.BlockSpec((D_MODEL, th), lambda i, j: (0, j)),
            pl.BlockSpec((th, D_MODEL), lambda i, j: (j, 0)),
            pl.BlockSpec((1, D_MODEL), lambda i, j: (0, 0)),
            pl.BlockSpec((1, D_MODEL), lambda i, j: (0, 0)),
        ],
        out_specs=pl.BlockSpec((tm, D_MODEL), lambda i, j: (i, 0)),
        out_shape=jax.ShapeDtypeStruct((n, D_MODEL), F32),
        scratch_shapes=[pltpu.VMEM((tm, D_MODEL), BF16), pltpu.VMEM((tm, D_MODEL), F32)],
        compiler_params=_cparams(("arbitrary", "arbitrary")),
        name="ffn_ln",
    )(x2d, sc, sh, g2, w1, w3, w2, lg, lb)


def _rope_tables(n_tok):
    rows = n_tok // GRID_W
    row = jnp.repeat(jnp.arange(rows), GRID_W).astype(F32)
    col = jnp.tile(jnp.arange(GRID_W), rows).astype(F32)
    nf = DA_HD // 4
    inv = ROPE_BASE ** (-jnp.arange(nf, dtype=F32) / nf)
    ang = jnp.concatenate([row[:, None] * inv] * 2 + [col[:, None] * inv] * 2, axis=-1)
    cos = jnp.cos(ang)
    sin = jnp.sin(ang)
    first = (jnp.arange(DA_HD) % (2 * nf)) < nf
    s_dn = jnp.where(first, -sin, 0.0)
    s_up = jnp.where(first, 0.0, sin)
    reps = DA_W // DA_HD
    return jnp.tile(cos, (1, reps)), jnp.tile(s_dn, (1, reps)), jnp.tile(s_up, (1, reps))


def _rope(t, tabs):
    cos, s_dn, s_up = tabs
    nf = DA_HD // 4
    return t * cos + jnp.roll(t, -nf, axis=-1) * s_dn + jnp.roll(t, nf, axis=-1) * s_up


def _centred_shift(u):
    up = jnp.pad(u, ((0, 0), (1, 1), (0, 0)))
    return 0.5 * (up[:, :-2] + up[:, 2:])


def _rwkv_pre(u, lp):
    b, l, _ = u.shape
    h, n = RW_HEADS, RW_HD
    u = u + lp['rw_mu'] * (_centred_shift(u) - u)
    r, k, v = u[..., :RW_W], u[..., RW_W:2 * RW_W], u[..., 2 * RW_W:3 * RW_W]
    o = 3 * RW_W
    wd = jnp.tanh(u[..., o:o + 2 * RW_DECAY_R]).reshape(b, l, 2, RW_DECAY_R)
    o += 2 * RW_DECAY_R
    ad = u[..., o:o + 2 * RW_ICLR_R].reshape(b, l, 2, RW_ICLR_R)
    o += 2 * RW_ICLR_R
    gd = u[..., o:]
    w_raw = lp['rw_w0'] + jnp.einsum('bldr,drc->bldc', wd, lp['rw_w2'])
    lw = -jnp.exp(-jax.nn.softplus(-w_raw) - 0.5)
    a = jax.nn.sigmoid(lp['rw_a0'] + jnp.einsum('bldr,drc->bldc', ad, lp['rw_a2']))
    kk = (k * lp['rw_kk']).reshape(b, l, h, n)
    kk = kk * lax.rsqrt(jnp.maximum(jnp.sum(jnp.square(kk), -1, keepdims=True), 1e-24))
    kk = kk.reshape(b, l, RW_W)
    kd = k[:, :, None] * (1.0 + (a - 1.0) * lp['rw_ka'])
    tr = lambda z: jnp.moveaxis(z, 2, 0)
    return r, k, v, kk, tr(kd), tr(a), tr(lw), gd


def _rwkv_post(y2, r, v, kd, gd, lp):
    b, l, _ = r.shape
    h, n = RW_HEADS, RW_HD
    y = (y2[0] + y2[1]).reshape(b, l, h, n)
    mu = jnp.mean(y, -1, keepdims=True)
    var = jnp.mean(jnp.square(y - mu), -1, keepdims=True)
    y = ((y - mu) * lax.rsqrt(var + RW_GN_EPS)).reshape(b, l, RW_W) * lp['rw_ln_g'] + lp['rw_ln_b']
    k_bonus = (0.5 * (kd[0] + kd[1])).reshape(b, l, h, n)
    bonus = (jnp.sum(r.reshape(b, l, h, n) * k_bonus * lp['rw_rk'], -1, keepdims=True)
             * v.reshape(b, l, h, n)).reshape(b, l, RW_W)
    g = jnp.einsum('blr,rc->blc', jax.nn.sigmoid(gd), lp['rw_g2'])
    return (y + bonus) * g


def _ssm_pre(xbc, dt_raw, lp):
    b, l, _ = xbc.shape
    w = lp['ssm_conv_w']
    xp = jnp.pad(xbc, ((0, 0), (1, 1), (0, 0)))
    conv = xp[:, :-2] * w[0] + xp[:, 1:-1] * w[1] + xp[:, 2:] * w[2] + lp['ssm_conv_b']
    xbc = jax.nn.silu(conv)
    xs = xbc[..., :SSM_W]
    bm = xbc[..., SSM_W:SSM_W + SSM_GROUPS * SSM_STATE]
    cm = xbc[..., SSM_W + SSM_GROUPS * SSM_STATE:]
    dt = jax.nn.softplus(dt_raw.reshape(b, l, 2, SSM_HEADS) + lp['ssm_dt_bias'])
    a_neg = -jnp.exp(lp['ssm_a_log'])
    da = jnp.moveaxis(dt * a_neg, 2, 0)
    dt = jnp.moveaxis(dt, 2, 0)
    xdt = (xs.reshape(b, l, SSM_HEADS, SSM_HD)[None] * dt[..., None]).reshape(2, b, l, SSM_W)
    return xs, bm, cm, xdt, da


def _ssm_post(y2, xs, z, lp):
    b, l, _ = xs.shape
    dskip = jnp.repeat(lp['ssm_d'], SSM_HD)
    y = (y2[0] + y2[1] + dskip * xs) * jax.nn.silu(z)
    y = y.reshape(b, l, SSM_GROUPS, SSM_W // SSM_GROUPS)
    y = y * lax.rsqrt(jnp.mean(jnp.square(y), -1, keepdims=True) + NORM_EPS)
    return y.reshape(b, l, SSM_W) * lp['ssm_norm_g']


def _permute_w_in(w_in):
    parts, acc = [], 0
    for s in W_IN_SPLIT:
        parts.append(w_in[:, acc:acc + s])
        acc += s
    q, k, v, rw, z, xbc, dt, ga, gr, gs = parts
    pad = jnp.zeros((w_in.shape[0], PROJ_COLS - C_DT - dt.shape[1]), w_in.dtype)
    return jnp.concatenate([ga, gr, gs, q, k, v, z, xbc, rw, dt, pad], axis=1).astype(BF16)


def _mixer(x, mod, lp, layer, side, want_out, rope_tabs):
    b, l, _ = x.shape
    sh1, sc1, g1 = mod
    x2d = x.reshape(b * l, D_MODEL)
    proj = _inproj(x2d, sc1, sh1, lp['w_in_p'], l)
    p3 = proj.reshape(b, l, PROJ_COLS)
    q, k, v = p3[..., C_Q:C_K], p3[..., C_K:C_V], p3[..., C_V:C_Z]
    lam_init = 0.8 - 0.6 * math.exp(-0.3 * layer)
    lam = (jnp.exp(jnp.sum(lp['da_lq1'] * lp['da_lk1'])) - jnp.exp(jnp.sum(lp['da_lq2'] * lp['da_lk2']))
           + lam_init).reshape(1)
    if side is None:
        k_all, v_all = k.astype(BF16), v.astype(BF16)
        side_a = (k_all, v_all)
        s0_r = jnp.zeros((2, b, RW_W // LANES, LANES, LANES), F32)
        s0_s = jnp.zeros((2, b, SSM_GROUPS, SSM_STATE, SSM_W // SSM_GROUPS), F32)
    else:
        (kc, vc), s0_r, s0_s = side
        q = _rope(q, rope_tabs)
        k_all = jnp.concatenate([kc, _rope(k, rope_tabs).astype(BF16)], axis=1)
        v_all = jnp.concatenate([vc, v.astype(BF16)], axis=1)
        side_a = None
    r, kr, vr, kk, kd, a, lw, gd = _rwkv_pre(p3[..., C_RW:C_DT], lp)
    y_r2, sf_r = _rwkv_scan(r, vr, kk, kd, a, lw, s0_r)
    xs, bm, cm, xdt, da = _ssm_pre(p3[..., C_XBC:C_RW], p3[..., C_DT:C_DT + 2 * SSM_HEADS], lp)
    y_s2, sf_s = _ssd_scan(xdt, da, bm, cm, s0_s)
    new_side = (side_a, sf_r, sf_s)
    if not want_out:
        return None, new_side
    q = (q * (DA_HD ** -0.5)).astype(BF16)
    y_a = _diff_attention(lam, q, k_all, v_all, lp['da_norm_g'].reshape(1, 2 * DA_HD), 1.0 - lam_init)
    y_r = _rwkv_post(y_r2, r, vr, kd, gd, lp)
    y_s = _ssm_post(y_s2, xs, p3[..., C_Z:C_XBC], lp)
    n = b * l
    x1 = _merge(y_a.reshape(n, DA_W), y_r.reshape(n, RW_W), y_s.reshape(n, SSM_W), proj, x2d, g1,
                lp['p_attn'], lp['p_rwkv'], lp['p_ssm'], lp['w_out'], lp['ln1_g'], lp['ln1_b'], l)
    return x1, new_side


def kernel(x, c, ctx, c_ctx, ada_w, ada_b, w_in, da_lq1, da_lk1, da_lq2, da_lk2, da_norm_g,
           rw_mu, rw_w0, rw_w2, rw_a0, rw_a2, rw_g2, rw_kk, rw_ka, rw_rk, rw_ln_g, rw_ln_b,
           ssm_conv_w, ssm_conv_b, ssm_dt_bias, ssm_a_log, ssm_d, ssm_norm_g,
           p_attn, p_rwkv, p_ssm, w_out, ln1_g, ln1_b, ffn_w1, ffn_w3, ffn_w2, ln2_g, ln2_b):
    b, l, _ = x.shape
    lc = ctx.shape[1]
    rope_tabs = _rope_tables(l)
    xc = ctx
    for i in range(DEPTH):
        lp = {
            'w_in_p': _permute_w_in(w_in[i]),
            'da_lq1': da_lq1[i], 'da_lk1': da_lk1[i], 'da_lq2': da_lq2[i], 'da_lk2': da_lk2[i],
            'da_norm_g': da_norm_g[i], 'rw_mu': rw_mu[i], 'rw_w0': rw_w0[i], 'rw_w2': rw_w2[i],
            'rw_a0': rw_a0[i], 'rw_a2': rw_a2[i], 'rw_g2': rw_g2[i], 'rw_kk': rw_kk[i],
            'rw_ka': rw_ka[i], 'rw_rk': rw_rk[i], 'rw_ln_g': rw_ln_g[i], 'rw_ln_b': rw_ln_b[i],
            'ssm_conv_w': ssm_conv_w[i], 'ssm_conv_b': ssm_conv_b[i], 'ssm_dt_bias': ssm_dt_bias[i],
            'ssm_a_log': ssm_a_log[i], 'ssm_d': ssm_d[i], 'ssm_norm_g': ssm_norm_g[i],
            'p_attn': p_attn[i].astype(BF16), 'p_rwkv': p_rwkv[i].astype(BF16),
            'p_ssm': p_ssm[i].astype(BF16), 'w_out': w_out[i].astype(BF16),
            'ln1_g': ln1_g[i].reshape(1, D_MODEL), 'ln1_b': ln1_b[i].reshape(1, D_MODEL),
        }
        w1, w3, w2 = ffn_w1[i].astype(BF16), ffn_w3[i].astype(BF16), ffn_w2[i].astype(BF16)
        l2g, l2b = ln2_g[i].reshape(1, D_MODEL), ln2_b[i].reshape(1, D_MODEL)
        last = i == DEPTH - 1
        mod_x = (jax.nn.silu(c) @ ada_w[i] + ada_b[i])[:, None, :]
        mod_c = jnp.broadcast_to((jax.nn.silu(c_ctx) @ ada_w[i] + ada_b[i])[None, None, :], (b, 1, 6 * D_MODEL))
        sh1, sc1, g1, sh2, sc2, g2 = jnp.split(mod_x, 6, axis=-1)
        csh1, csc1, cg1, csh2, csc2, cg2 = jnp.split(mod_c, 6, axis=-1)
        xc1, side = _mixer(xc, (csh1, csc1, cg1), lp, i, None, not last, None)
        x1, _ = _mixer(x, (sh1, sc1, g1), lp, i, side, True, rope_tabs)
        x = _ffn(x1, sc2, sh2, g2, w1, w3, w2, l2g, l2b, l).reshape(b, l, D_MODEL)
        if not last:
            xc = _ffn(xc1, csc2, csh2, cg2, w1, w3, w2, l2g, l2b, lc).reshape(b, lc, D_MODEL)
    return x
```

```python
import functools
import math

import jax
import jax.numpy as jnp
from jax import lax
from jax.experimental import pallas as pl
from jax.experimental.pallas import tpu as pltpu

F32 = jnp.float32
BF16 = jnp.bfloat16

D_MODEL = 1024
DEPTH = 4
GRID_W = 64
DA_HEADS = 4
DA_HD = 64
DA_W = DA_HEADS * 2 * DA_HD
ROPE_BASE = 10000.0
RW_HEADS = 8
RW_HD = 64
RW_W = RW_HEADS * RW_HD
RW_DECAY_R = 64
RW_ICLR_R = 64
RW_GATE_R = 128
RW_GN_EPS = 64e-5
RW_COLS = 3 * RW_W + 2 * RW_DECAY_R + 2 * RW_ICLR_R + RW_GATE_R
SSM_HEADS = 8
SSM_HD = 64
SSM_W = SSM_HEADS * SSM_HD
SSM_GROUPS = 2
SSM_STATE = 128
SSM_XBC = SSM_W + 2 * SSM_GROUPS * SSM_STATE
FFN_HIDDEN = (8 * D_MODEL + 3 * 256 - 1) // (3 * 256) * 256
NORM_EPS = 1e-5
W_IN_SPLIT = (DA_W, DA_W, DA_W, RW_COLS, SSM_W, SSM_XBC, 2 * SSM_HEADS, D_MODEL, D_MODEL, D_MODEL)
ALPHA = (2.0 * DEPTH) ** 0.25

LANES = 128
SUBLANES = 8
VMEM_LIMIT_BYTES = 56 * 1024 * 1024

PROJ_COLS = 8192
RW_BLK = 2048
C_RW = 0
C_DT = RW_COLS
C_GA, C_GR, C_GS = RW_BLK, RW_BLK + D_MODEL, RW_BLK + 2 * D_MODEL
C_Q = RW_BLK + 3 * D_MODEL
C_K = C_Q + DA_W
C_V = C_K + DA_W
C_Z = C_V + DA_W
C_XBC = C_Z + SSM_W

RW_CHUNK = 64
RW_BLOCK = 128
SSD_CHUNK = 128


def _cparams(sem):
    return pltpu.CompilerParams(dimension_semantics=sem, vmem_limit_bytes=VMEM_LIMIT_BYTES)


def _dot(a, b):
    return jnp.dot(a, b, preferred_element_type=F32)


def _dot_nt(a, b):
    return lax.dot_general(a, b, (((1,), (1,)), ((), ())), preferred_element_type=F32)


def _dot_tn(a, b):
    return lax.dot_general(a, b, (((0,), (0,)), ((), ())), preferred_element_type=F32)


def _bf(x):
    return x.astype(BF16)


def _split3(x):
    hi = _bf(x)
    r1 = x - hi.astype(F32)
    mid = _bf(r1)
    lo = _bf(r1 - mid.astype(F32))
    return hi, mid, lo


def _dot3(a, b):
    ah, am, _ = _split3(a)
    bh, bm, _ = _split3(b)
    return _dot(ah, bh) + (_dot(ah, bm) + _dot(am, bh))


def _cumsum_rows(tri_b, x):
    n = x.shape[1]
    hi, mid, lo = _split3(x)
    o = _dot(tri_b, jnp.concatenate([hi, mid, lo], axis=1))
    return o[:, :n] + o[:, n:2 * n] + o[:, 2 * n:]


def _seg_sum(x, seg_b):
    m = x.shape[0]
    hi = _bf(x)
    lo = _bf(x - hi.astype(F32))
    o = _dot(jnp.concatenate([hi, lo], axis=0), seg_b)
    return o[:m] + o[m:]


def _sigmoid(x):
    return 1.0 / (1.0 + jnp.exp(-x))


def _softplus(x):
    return jnp.maximum(x, 0.0) + jnp.log(1.0 + jnp.exp(-jnp.abs(x)))


def _layernorm_rows(t, g, b):
    mu = jnp.mean(t, -1, keepdims=True)
    d = t - mu
    var = jnp.mean(d * d, -1, keepdims=True)
    return d * lax.rsqrt(var + NORM_EPS) * g + b


def _neighbour_rows(x, prev_row, next_row):
    t = x.shape[0]
    rowi = lax.broadcasted_iota(jnp.int32, (t, 1), 0)
    x_m1 = jnp.where(rowi == 0, prev_row, pltpu.roll(x, 1, 0))
    x_p1 = jnp.where(rowi == t - 1, next_row, pltpu.roll(x, t - 1, 0))
    return x_m1, x_p1


def _halo_specs(width, col_blk, seq_len, tb, nc, n_rows):
    per8 = tb // SUBLANES
    last8 = n_rows // SUBLANES - 1

    def blk(d, bi, c):
        return bi * nc + c + d * (nc - 1 - 2 * c)

    prev = pl.BlockSpec((SUBLANES, width), lambda d, bi, c: (jnp.maximum(blk(d, bi, c) * per8 - 1, 0), col_blk))
    nxt = pl.BlockSpec((SUBLANES, width),
                       lambda d, bi, c: (jnp.minimum((blk(d, bi, c) + 1) * per8, last8), col_blk))
    cur = pl.BlockSpec((tb, width), lambda d, bi, c: (blk(d, bi, c), col_blk))
    return cur, prev, nxt


def _inproj_kernel(x_ref, sc_ref, sh_ref, w_ref, o_ref, h_scr):
    @pl.when(pl.program_id(1) == 0)
    def _():
        h = x_ref[...] * (1.0 + sc_ref[0]) + sh_ref[0]
        h_scr[...] = h.astype(BF16)

    o_ref[...] = _dot(h_scr[...], w_ref[...])


def _inproj(x2d, sc, sh, w, seq_len):
    n = x2d.shape[0]
    tm = min(512, seq_len)
    tn = 1024
    per_b = seq_len // tm
    return pl.pallas_call(
        _inproj_kernel,
        grid=(n // tm, PROJ_COLS // tn),
        in_specs=[
            pl.BlockSpec((tm, D_MODEL), lambda i, j: (i, 0)),
            pl.BlockSpec((1, 1, D_MODEL), lambda i, j: (i // per_b, 0, 0)),
            pl.BlockSpec((1, 1, D_MODEL), lambda i, j: (i // per_b, 0, 0)),
            pl.BlockSpec((D_MODEL, tn), lambda i, j: (0, j)),
        ],
        out_specs=pl.BlockSpec((tm, tn), lambda i, j: (i, j)),
        out_shape=jax.ShapeDtypeStruct((n, PROJ_COLS), F32),
        scratch_shapes=[pltpu.VMEM((tm, D_MODEL), BF16)],
        compiler_params=_cparams(("arbitrary", "arbitrary")),
        name="inproj",
    )(x2d, sc, sh, w)


def _attn_kernel(lam_ref, q_ref, k_ref, v_ref, g_ref, o_ref, *, out_scale):
    tq = q_ref.shape[1]
    q = q_ref[0]
    lane = lax.broadcasted_iota(jnp.int32, (1, LANES), 1)
    zero = jnp.zeros_like(q)
    qq = jnp.concatenate([jnp.where(lane < DA_HD, q, zero), jnp.where(lane >= DA_HD, q, zero)], axis=0)
    s = _dot_nt(qq, k_ref[0])
    p = jnp.exp(s - jnp.max(s, -1, keepdims=True))
    inv = 1.0 / jnp.sum(p, -1, keepdims=True)
    lam = lam_ref[0]
    a = p[:tq] * inv[:tq] - lam * (p[tq:] * inv[tq:])
    o = _dot(a.astype(BF16), v_ref[0])
    ms = jnp.mean(o * o, -1, keepdims=True)
    o_ref[0] = o * lax.rsqrt(ms + NORM_EPS) * (g_ref[...] * out_scale)


def _diff_attention(lam, q, k_all, v_all, g, out_scale):
    b, lq, _ = q.shape
    lk = k_all.shape[1]
    tq = 128
    return pl.pallas_call(
        functools.partial(_attn_kernel, out_scale=out_scale),
        grid=(b, DA_HEADS, lq // tq),
        in_specs=[
            pl.BlockSpec(memory_space=pltpu.SMEM),
            pl.BlockSpec((1, tq, LANES), lambda bi, h, i: (bi, i, h)),
            pl.BlockSpec((1, lk, LANES), lambda bi, h, i: (bi, 0, h)),
            pl.BlockSpec((1, lk, LANES), lambda bi, h, i: (bi, 0, h)),
            pl.BlockSpec((1, LANES), lambda bi, h, i: (0, 0)),
        ],
        out_specs=pl.BlockSpec((1, tq, LANES), lambda bi, h, i: (bi, i, h)),
        out_shape=jax.ShapeDtypeStruct((b, lq, DA_W), F32),
        compiler_params=_cparams(("arbitrary", "arbitrary", "arbitrary")),
        name="diff_attn",
    )(lam, q, k_all, v_all, g)


def _rwkv_kernel(u_ref, up_ref, un_ref, mu_ref, w0_ref, w2_ref, a0_ref, a2_ref, g2_ref, vec_ref, seg_ref,
                 s0_ref, y_ref, aux_ref, sf_ref, s_scr, op_scr):
    d = pl.program_id(0)
    c = pl.program_id(2)
    nc = pl.num_programs(2)
    t = RW_CHUNK
    tb = u_ref.shape[0]
    nper = tb // t
    cc = c + d * (nc - 1 - 2 * c)

    @pl.when(c == 0)
    def _():
        s_scr[...] = s0_ref[0, 0]

    u = u_ref[...]
    prev = up_ref[SUBLANES - 1:SUBLANES, :] * (cc > 0).astype(F32)
    nxt = un_ref[0:1, :] * (cc < nc - 1).astype(F32)
    u_m1, u_p1 = _neighbour_rows(u, prev, nxt)
    u = u + mu_ref[...] * (0.5 * (u_m1 + u_p1) - u)
    r = u[:, :RW_W]
    k = u[:, RW_W:2 * RW_W]
    v = u[:, 2 * RW_W:3 * RW_W]
    o = 3 * RW_W
    wd = _bf(jnp.tanh(u[:, o:o + 2 * RW_DECAY_R]))
    o += 2 * RW_DECAY_R
    ad = _bf(u[:, o:o + 2 * RW_ICLR_R])
    o += 2 * RW_ICLR_R
    gd = u[:, o:o + RW_GATE_R]
    kk_gain, ka, rk = vec_ref[0:1, :], vec_ref[1:2, :], vec_ref[2:3, :]
    seg = seg_ref[...]
    lw_all = -math.exp(-0.5) * _sigmoid(w0_ref[0] + _dot(wd, w2_ref[0]))
    a_all = _sigmoid(a0_ref[pl.ds(d, 1), :] + _dot(ad, a2_ref[d]))
    kk_all = k * kk_gain
    kk_all = kk_all * lax.rsqrt(jnp.maximum(_seg_sum(kk_all * kk_all, seg), 1e-24))
    op_scr[0] = r
    op_scr[1] = v
    op_scr[2] = kk_all
    op_scr[3] = k * (1.0 + (a_all - 1.0) * ka)
    op_scr[4] = a_all
    op_scr[5] = lw_all

    @pl.when(d == 0)
    def _():
        a_other = _sigmoid(a0_ref[1:2, :] + _dot(ad, a2_ref[1]))
        k_bonus = k * (1.0 + (0.5 * (a_all + a_other) - 1.0) * ka)
        aux_ref[0, 0] = _seg_sum(r * k_bonus * rk, seg) * v

    @pl.when(d == 1)
    def _():
        aux_ref[0, 0] = _dot(_bf(_sigmoid(gd)), g2_ref[...])

    sign = 1 - 2 * d
    row = lax.broadcasted_iota(jnp.int32, (t, t), 0)
    col = lax.broadcasted_iota(jnp.int32, (t, t), 1)
    tri_b = ((row - col) * sign >= 0).astype(BF16)
    row2 = lax.broadcasted_iota(jnp.int32, (t, 2 * t), 0)
    col2 = lax.broadcasted_iota(jnp.int32, (t, 2 * t), 1)
    col2 = jnp.where(col2 >= t, col2 - t, col2)
    delta2 = (row2 - col2) * sign
    strict2 = delta2 > 0
    incl2 = delta2 >= 0
    colh = lax.broadcasted_iota(jnp.int32, (1, 2 * t), 1)
    cm0 = (colh < t).astype(F32)
    cm1 = 1.0 - cm0
    lane = lax.broadcasted_iota(jnp.int32, (1, LANES), 1)
    m0 = (lane < RW_HD).astype(F32)
    m1 = 1.0 - m0
    rb = lax.broadcasted_iota(jnp.int32, (LANES, LANES), 0) // RW_HD
    cb = lax.broadcasted_iota(jnp.int32, (LANES, LANES), 1) // RW_HD
    bd = (rb == cb).astype(F32)
    eye = (lax.broadcasted_iota(jnp.int32, (2 * t, 2 * t), 0)
           == lax.broadcasted_iota(jnp.int32, (2 * t, 2 * t), 1)).astype(F32)

    def stack2(x):
        return _bf(jnp.concatenate([x * m0, x * m1], axis=0))

    nhp = RW_W // LANES
    inst = [(j, i) for i in range(nper) for j in range(nhp)]

    def rows_of(i):
        return pl.ds(pl.multiple_of((i + d * (nper - 1 - 2 * i)) * t, t), t)

    def sl_of(j):
        return slice(j * LANES, (j + 1) * LANES)

    lws = [op_scr[5, rows_of(i), sl_of(j)] for j, i in inst]
    cums = [_cumsum_rows(tri_b, lw) for lw in lws]
    pre = []
    for (j, i), lw, cum in zip(inst, lws, cums):
        rows, sl = rows_of(i), sl_of(j)
        r = op_scr[0, rows, sl]
        v = op_scr[1, rows, sl]
        kk = op_scr[2, rows, sl]
        kd = op_scr[3, rows, sl]
        a = op_scr[4, rows, sl]
        tot = jnp.sum(lw, axis=0, keepdims=True)
        g_inv = jnp.exp(-cum)
        g_rem = jnp.exp(tot - cum)
        rt = r * jnp.exp(cum)
        kt = _bf(kk * jnp.exp(cum - lw))
        b = kk * a
        bh = b * g_inv
        kh = kd * g_inv
        ystack = _bf(jnp.concatenate([bh * m0, bh * m1, kh * m0, kh * m1], axis=0))
        pre.append(dict(v=v, rt=rt, kt=kt, ystack=ystack, g_tot=jnp.exp(tot), vst=stack2(v),
                        rhs_r=_bf(jnp.concatenate([kd * g_rem, b * g_rem], axis=0))))
    ggs = [_dot_nt(jnp.concatenate([p['kt'], _bf(p['rt'])], axis=0), p['ystack']) for p in pre]
    for p, gg in zip(pre, ggs):
        mcat = jnp.where(strict2, gg[:t, :2 * t], 0.0)
        p['ncat'] = _bf(jnp.where(strict2, gg[:t, 2 * t:], 0.0))
        p['qcat'] = _bf(jnp.where(incl2, gg[t:, :2 * t], 0.0))
        p['pcat'] = _bf(jnp.where(incl2, gg[t:, 2 * t:], 0.0))
        p['x0'] = -jnp.concatenate([mcat * cm0, mcat * cm1], axis=0)
    n_dbl = int(math.log2(t))
    ws = [eye + p['x0'] for p in pre]
    xs = [_dot(_bf(p['x0']), _bf(p['x0'])) for p in pre]
    nvs = [_dot(p['ncat'], p['vst']) for p in pre]
    for step in range(1, n_dbl):
        last = step == n_dbl - 1
        outs = [_dot(_bf(w if last else jnp.concatenate([w, x], axis=0)), _bf(x)) for w, x in zip(ws, xs)]
        ws = [w + o[:2 * t] for w, o in zip(ws, outs)]
        if not last:
            xs = [o[2 * t:] for o in outs]
    errs = [eye - _dot3(eye - p['x0'], w) for p, w in zip(pre, ws)]
    ws = [w + _dot(_bf(w), _bf(e)) for w, e in zip(ws, errs)]
    wcats = [_bf(w[:t] + w[t:]) for w in ws]
    uks = [_dot(wc, jnp.concatenate([stack2(nv), stack2(p['kt'].astype(F32))], axis=1))
           for wc, nv, p in zip(wcats, nvs, pre)]
    qus = [_dot(p['qcat'], jnp.concatenate([stack2(uk[:, :LANES]), stack2(uk[:, LANES:])], axis=1))
           for p, uk in zip(pre, uks)]
    pvs = [_dot(p['pcat'], p['vst']) for p in pre]
    zeros_t = jnp.zeros((t, LANES), F32)
    dcs = [_dot_tn(_bf(jnp.concatenate([jnp.concatenate([p['v'], -uk[:, :LANES]], axis=0),
                                        jnp.concatenate([zeros_t, uk[:, LANES:]], axis=0)], axis=1)), p['rhs_r'])
           for p, uk in zip(pre, uks)]
    for p, qu, pv, dc in zip(pre, qus, pvs, dcs):
        p['y0'] = pv - qu[:, :LANES]
        p['rp'] = _bf(p['rt'] - qu[:, LANES:])
        p['d0'] = bd * dc[:LANES]
        p['c0'] = _bf(bd * dc[LANES:])
    states = [s_scr[j] for j in range(nhp)]
    for i in range(nper):
        ps = [pre[i * nhp + j] for j in range(nhp)]
        sbs = [_bf(sbd) for sbd in states]
        ys = [_dot_nt(p['rp'], sb) for p, sb in zip(ps, sbs)]
        scs = [_dot(sb, p['c0']) for p, sb in zip(ps, sbs)]
        for j in range(nhp):
            y_ref[0, 0, rows_of(i), sl_of(j)] = ps[j]['y0'] + ys[j]
        states = [sbd * p['g_tot'] - sc + p['d0'] for sbd, p, sc in zip(states, ps, scs)]
    for j in range(nhp):
        s_scr[j] = states[j]

    @pl.when(c == nc - 1)
    def _():
        sf_ref[0, 0] = s_scr[...]


def _rwkv_scan(proj, batch, seq_len, rwp, s0):
    n = proj.shape[0]
    tb = min(RW_BLOCK, seq_len)
    nc = seq_len // tb
    nhp = RW_W // LANES
    cur, prev, nxt = _halo_specs(RW_BLK, C_RW // RW_BLK, seq_len, tb, nc, n)
    full2 = lambda d, bi, c: (0, 0)
    full3 = lambda d, bi, c: (0, 0, 0)
    dmap = lambda d, bi, c: (d, bi, c + d * (nc - 1 - 2 * c), 0)
    smap = lambda d, bi, c: (d, bi, 0, 0, 0)
    return pl.pallas_call(
        _rwkv_kernel,
        grid=(2, batch, nc),
        in_specs=[
            cur, prev, nxt,
            pl.BlockSpec((1, RW_BLK), full2),
            pl.BlockSpec((1, 1, RW_W), lambda d, bi, c: (d, 0, 0)),
            pl.BlockSpec((1, 2 * RW_DECAY_R, RW_W), lambda d, bi, c: (d, 0, 0)),
            pl.BlockSpec((2, RW_W), full2),
            pl.BlockSpec((2, 2 * RW_ICLR_R, RW_W), full3),
            pl.BlockSpec((RW_GATE_R, RW_W), full2),
            pl.BlockSpec((3, RW_W), full2),
            pl.BlockSpec((RW_W, RW_W), full2),
            pl.BlockSpec((1, 1, nhp, LANES, LANES), smap),
        ],
        out_specs=[
            pl.BlockSpec((1, 1, tb, RW_W), dmap),
            pl.BlockSpec((1, 1, tb, RW_W), dmap),
            pl.BlockSpec((1, 1, nhp, LANES, LANES), smap),
        ],
        out_shape=[
            jax.ShapeDtypeStruct((2, batch, seq_len, RW_W), F32),
            jax.ShapeDtypeStruct((2, batch, seq_len, RW_W), F32),
            jax.ShapeDtypeStruct((2, batch, nhp, LANES, LANES), F32),
        ],
        scratch_shapes=[pltpu.VMEM((nhp, LANES, LANES), F32), pltpu.VMEM((6, tb, RW_W), F32)],
        compiler_params=_cparams(("arbitrary", "arbitrary", "arbitrary")),
        name="rwkv_scan",
    )(proj, proj, proj, rwp['mu'], rwp['w0'], rwp['w2'], rwp['a0'], rwp['a2'], rwp['g2'], rwp['vec'],
      rwp['seg'], s0)


def _ssd_kernel(x_ref, xp_ref, xn_ref, dt_ref, cw_ref, cb_ref, dtb_ref, alog_ref, dskip_ref, s0_ref,
                y_ref, sf_ref, s_scr):
    d = pl.program_id(0)
    c = pl.program_id(2)
    nc = pl.num_programs(2)
    t = SSD_CHUNK
    gw = SSM_W // SSM_GROUPS
    hpg = SSM_HEADS // SSM_GROUPS
    nh = SSM_HEADS
    cc = c + d * (nc - 1 - 2 * c)

    @pl.when(c == 0)
    def _():
        s_scr[...] = s0_ref[0, 0]

    x = x_ref[...]
    prev = xp_ref[SUBLANES - 1:SUBLANES, :] * (cc > 0).astype(F32)
    nxt = xn_ref[0:1, :] * (cc < nc - 1).astype(F32)
    x_m1, x_p1 = _neighbour_rows(x, prev, nxt)
    xc = x_m1 * cw_ref[0:1, :] + x * cw_ref[1:2, :] + x_p1 * cw_ref[2:3, :] + cb_ref[...]
    xc = xc * _sigmoid(xc)
    xs = xc[:, :SSM_W]
    bm = xc[:, SSM_W:SSM_W + SSM_GROUPS * SSM_STATE]
    cm = xc[:, SSM_W + SSM_GROUPS * SSM_STATE:]
    dt2 = _softplus(dt_ref[:, :2 * nh] + dtb_ref[...])
    a2 = dt2 * (-jnp.exp(alog_ref[...]))
    fwd = d == 0
    dt = jnp.where(fwd, dt2[:, :nh], dt2[:, nh:])
    a = jnp.where(fwd, a2[:, :nh], a2[:, nh:])

    sign = 1 - 2 * d
    row = lax.broadcasted_iota(jnp.int32, (t, t), 0)
    col = lax.broadcasted_iota(jnp.int32, (t, t), 1)
    delta = (row - col) * sign
    incl = delta >= 0
    tri_b = incl.astype(BF16)
    tri_t_b = (delta <= 0).astype(BF16)
    lane = lax.broadcasted_iota(jnp.int32, (1, LANES), 1)
    m0 = (lane < SSM_HD).astype(F32)
    m1 = 1.0 - m0
    lane_g = lax.broadcasted_iota(jnp.int32, (1, gw), 1) // SSM_HD

    a3 = jnp.concatenate(_split3(a), axis=1)
    cs3 = _dot(tri_b, a3)
    cs = cs3[:, :nh] + cs3[:, nh:2 * nh] + cs3[:, 2 * nh:]
    cs3_t = _dot_tn(a3, tri_t_b)
    cs_t = cs3_t[:nh] + cs3_t[nh:2 * nh] + cs3_t[2 * nh:]
    tot = jnp.sum(a, axis=0, keepdims=True)
    e_cs = jnp.exp(cs)
    e_rem = jnp.exp(tot - cs)
    e_tot = jnp.exp(tot)

    def expand(z, g):
        out = jnp.zeros((z.shape[0], gw), F32)
        for hh in range(hpg):
            h = g * hpg + hh
            out = jnp.where(lane_g == hh, z[:, h:h + 1], out)
        return out

    skip = fwd.astype(F32)
    for g in range(SSM_GROUPS):
        cg = cm[:, g * SSM_STATE:(g + 1) * SSM_STATE]
        bg = bm[:, g * SSM_STATE:(g + 1) * SSM_STATE]
        xsg = xs[:, g * gw:(g + 1) * gw]
        xg = xsg * expand(dt, g)
        cgb = _bf(cg)
        bgb = _bf(bg)
        gm = _dot_nt(cgb, bgb)
        st = s_scr[g]
        y_off = _dot(cgb, _bf(st)) * expand(e_cs, g)
        ys = []
        for hp in range(hpg // 2):
            scs = []
            for hh in range(2):
                h = g * hpg + 2 * hp + hh
                diff = jnp.where(incl, cs[:, h:h + 1] - cs_t[h:h + 1, :], 0.0)
                scs.append(gm * jnp.where(incl, jnp.exp(diff), 0.0))
            xp = xg[:, hp * LANES:(hp + 1) * LANES]
            xst = _bf(jnp.concatenate([xp * m0, xp * m1], axis=0))
            ys.append(_dot(_bf(jnp.concatenate(scs, axis=1)), xst))
        y_ref[0, 0, :, g * gw:(g + 1) * gw] = (jnp.concatenate(ys, axis=1) + y_off
                                               + skip * dskip_ref[:, g * gw:(g + 1) * gw] * xsg)
        s_scr[g] = st * expand(e_tot, g) + _dot_tn(bgb, _bf(xg * expand(e_rem, g)))

    @pl.when(c == nc - 1)
    def _():
        sf_ref[0, 0] = s_scr[...]


def _ssd_scan(proj, batch, seq_len, ssp, s0):
    n = proj.shape[0]
    t = min(SSD_CHUNK, seq_len)
    nc = seq_len // t
    gw = SSM_W // SSM_GROUPS
    cur, prev, nxt = _halo_specs(SSM_XBC, C_XBC // SSM_XBC, seq_len, t, nc, n)
    dtspec, _, _ = _halo_specs(LANES, C_DT // LANES, seq_len, t, nc, n)
    full2 = lambda d, bi, c: (0, 0)
    dmap = lambda d, bi, c: (d, bi, c + d * (nc - 1 - 2 * c), 0)
    smap = lambda d, bi, c: (d, bi, 0, 0, 0)
    return pl.pallas_call(
        _ssd_kernel,
        grid=(2, batch, nc),
        in_specs=[
            cur, prev, nxt, dtspec,
            pl.BlockSpec((3, SSM_XBC), full2),
            pl.BlockSpec((1, SSM_XBC), full2),
            pl.BlockSpec((1, 2 * SSM_HEADS), full2),
            pl.BlockSpec((1, 2 * SSM_HEADS), full2),
            pl.BlockSpec((1, SSM_W), full2),
            pl.BlockSpec((1, 1, SSM_GROUPS, SSM_STATE, gw), smap),
        ],
        out_specs=[
            pl.BlockSpec((1, 1, t, SSM_W), dmap),
            pl.BlockSpec((1, 1, SSM_GROUPS, SSM_STATE, gw), smap),
        ],
        out_shape=[
            jax.ShapeDtypeStruct((2, batch, seq_len, SSM_W), F32),
            jax.ShapeDtypeStruct((2, batch, SSM_GROUPS, SSM_STATE, gw), F32),
        ],
        scratch_shapes=[pltpu.VMEM((SSM_GROUPS, SSM_STATE, gw), F32)],
        compiler_params=_cparams(("arbitrary", "arbitrary", "arbitrary")),
        name="ssd_scan",
    )(proj, proj, proj, proj, ssp['conv_w'], ssp['conv_b'], ssp['dt_bias'], ssp['a_log'], ssp['dskip'], s0)


def _merge_kernel(ya_ref, yr_ref, aux_ref, ys_ref, z_ref, ga_ref, gr_ref, gs_ref, x_ref, g1_ref,
                  seg_ref, rg_ref, rb_ref, sg_ref, pa_ref, pr_ref, ps_ref, wo_ref, lg_ref, lb_ref, o_ref):
    seg = seg_ref[...]
    yr = yr_ref[0] + yr_ref[1]
    mu = _seg_sum(yr, seg) * (1.0 / RW_HD)
    dv = yr - mu
    var = _seg_sum(dv * dv, seg) * (1.0 / RW_HD)
    yr = (dv * lax.rsqrt(var + RW_GN_EPS) * rg_ref[...] + rb_ref[...] + aux_ref[0]) * aux_ref[1]
    z = z_ref[...]
    ys = (ys_ref[0] + ys_ref[1]) * (z * _sigmoid(z))
    gw = SSM_W // SSM_GROUPS
    parts = []
    for g in range(SSM_GROUPS):
        yg = ys[:, g * gw:(g + 1) * gw]
        parts.append(yg * lax.rsqrt(jnp.mean(yg * yg, -1, keepdims=True) + NORM_EPS))
    ys = jnp.concatenate(parts, axis=1) * sg_ref[...]
    m = (_sigmoid(ga_ref[...]) * _dot(_bf(ya_ref[...]), pa_ref[...])
         + _sigmoid(gr_ref[...]) * _dot(_bf(yr), pr_ref[...])
         + _sigmoid(gs_ref[...]) * _dot(_bf(ys), ps_ref[...]))
    out = _dot(_bf(m), wo_ref[...])
    o_ref[...] = _layernorm_rows(ALPHA * x_ref[...] + g1_ref[0] * out, lg_ref[...], lb_ref[...])


def _merge(ya, yr2, aux, ys2, proj, x2d, g1, lp, seq_len):
    n = x2d.shape[0]
    tm = min(256, seq_len)
    per_b = seq_len // tm
    row = lambda i: (i, 0)
    row3 = lambda i: (0, i, 0)
    full = lambda i: (0, 0)
    gate = lambda col: pl.BlockSpec((tm, D_MODEL), lambda i: (i, col // D_MODEL))
    vec512 = pl.BlockSpec((1, RW_W), full)
    vec = pl.BlockSpec((1, D_MODEL), full)
    return pl.pallas_call(
        _merge_kernel,
        grid=(n // tm,),
        in_specs=[
            pl.BlockSpec((tm, DA_W), row),
            pl.BlockSpec((2, tm, RW_W), row3), pl.BlockSpec((2, tm, RW_W), row3),
            pl.BlockSpec((2, tm, SSM_W), row3),
            pl.BlockSpec((tm, SSM_W), lambda i: (i, C_Z // SSM_W)),
            gate(C_GA), gate(C_GR), gate(C_GS),
            pl.BlockSpec((tm, D_MODEL), row),
            pl.BlockSpec((1, 1, D_MODEL), lambda i: (i // per_b, 0, 0)),
            pl.BlockSpec((RW_W, RW_W), full), vec512, vec512, vec512,
            pl.BlockSpec((DA_W, D_MODEL), full), pl.BlockSpec((RW_W, D_MODEL), full),
            pl.BlockSpec((SSM_W, D_MODEL), full), pl.BlockSpec((D_MODEL, D_MODEL), full),
            vec, vec,
        ],
        out_specs=pl.BlockSpec((tm, D_MODEL), row),
        out_shape=jax.ShapeDtypeStruct((n, D_MODEL), F32),
        compiler_params=_cparams(("arbitrary",)),
        name="merge_ln",
    )(ya, yr2, aux, ys2, proj, proj, proj, proj, x2d, g1, lp['rw']['seg'], lp['rw_ln_g'], lp['rw_ln_b'],
      lp['ssm_norm_g'], lp['p_attn'], lp['p_rwkv'], lp['p_ssm'], lp['w_out'], lp['ln1_g'], lp['ln1_b'])


def _ffn_kernel(x_ref, sc_ref, sh_ref, g2_ref, w1_ref, w3_ref, w2_ref, lg_ref, lb_ref, o_ref, h_scr, acc_scr):
    j = pl.program_id(1)

    @pl.when(j == 0)
    def _():
        h_scr[...] = (x_ref[...] * (1.0 + sc_ref[0]) + sh_ref[0]).astype(BF16)
        acc_scr[...] = jnp.zeros_like(acc_scr)

    h = h_scr[...]
    u = jax.nn.silu(_dot(h, w1_ref[...])) * _dot(h, w3_ref[...])
    acc_scr[...] += _dot(u.astype(BF16), w2_ref[...])

    @pl.when(j == pl.num_programs(1) - 1)
    def _():
        o_ref[...] = _layernorm_rows(ALPHA * x_ref[...] + g2_ref[0] * acc_scr[...], lg_ref[...], lb_ref[...])


def _ffn(x2d, sc, sh, g2, w1, w3, w2, lg, lb, seq_len):
    n = x2d.shape[0]
    tm = min(512, seq_len)
    th = FFN_HIDDEN // 2
    per_b = seq_len // tm
    mod = pl.BlockSpec((1, 1, D_MODEL), lambda i, j: (i // per_b, 0, 0))
    return pl.pallas_call(
        _ffn_kernel,
        grid=(n // tm, FFN_HIDDEN // th),
        in_specs=[
            pl.BlockSpec((tm, D_MODEL), lambda i, j: (i, 0)),
            mod, mod, mod,
            pl.BlockSpec((D_MODEL, th), lambda i, j: (0, j)),
            pl.BlockSpec((D_MODEL, th), lambda i, j: (0, j)),
            pl.BlockSpec((th, D_MODEL), lambda i, j: (j, 0)),
            pl.BlockSpec((1, D_MODEL), lambda i, j: (0, 0)),
            pl.BlockSpec((1, D_MODEL), lambda i, j: (0, 0)),
        ],
        out_specs=pl.BlockSpec((tm, D_MODEL), lambda i, j: (i, 0)),
        out_shape=jax.ShapeDtypeStruct((n, D_MODEL), F32),
        scratch_shapes=[pltpu.VMEM((tm, D_MODEL), BF16), pltpu.VMEM((tm, D_MODEL), F32)],
        compiler_params=_cparams(("arbitrary", "arbitrary")),
        name="ffn_ln",
    )(x2d, sc, sh, g2, w1, w3, w2, lg, lb)


def _rope_tables(n_tok):
    rows = n_tok // GRID_W
    row = jnp.repeat(jnp.arange(rows), GRID_W).astype(F32)
    col = jnp.tile(jnp.arange(GRID_W), rows).astype(F32)
    nf = DA_HD // 4
    inv = ROPE_BASE ** (-jnp.arange(nf, dtype=F32) / nf)
    ang = jnp.concatenate([row[:, None] * inv] * 2 + [col[:, None] * inv] * 2, axis=-1)
    cos = jnp.cos(ang)
    sin = jnp.sin(ang)
    first = (jnp.arange(DA_HD) % (2 * nf)) < nf
    s_dn = jnp.where(first, -sin, 0.0)
    s_up = jnp.where(first, 0.0, sin)
    reps = DA_W // DA_HD
    return jnp.tile(cos, (1, reps)), jnp.tile(s_dn, (1, reps)), jnp.tile(s_up, (1, reps))


def _rope(t, tabs):
    cos, s_dn, s_up = tabs
    nf = DA_HD // 4
    return t * cos + jnp.roll(t, -nf, axis=-1) * s_dn + jnp.roll(t, nf, axis=-1) * s_up


def _permute_w_in(w_in):
    parts, acc = [], 0
    for s in W_IN_SPLIT:
        parts.append(w_in[:, acc:acc + s])
        acc += s
    q, k, v, rw, z, xbc, dt, ga, gr, gs = parts
    pad = jnp.zeros((w_in.shape[0], RW_BLK - RW_COLS - dt.shape[1]), w_in.dtype)
    return jnp.concatenate([rw, dt, pad, ga, gr, gs, q, k, v, z, xbc], axis=1).astype(BF16)


def _pad_lora(w, rank):
    z = jnp.zeros_like(w[0])
    return jnp.stack([jnp.concatenate([w[0], z], axis=0), jnp.concatenate([z, w[1]], axis=0)]).astype(BF16)


def _mixer(x, mod, lp, layer, side, want_out, rope_tabs):
    b, l, _ = x.shape
    sh1, sc1, g1 = mod
    n = b * l
    x2d = x.reshape(n, D_MODEL)
    proj = _inproj(x2d, sc1, sh1, lp['w_in_p'], l)
    p3 = proj.reshape(b, l, PROJ_COLS)
    q, k, v = p3[..., C_Q:C_K], p3[..., C_K:C_V], p3[..., C_V:C_Z]
    lam_init = 0.8 - 0.6 * math.exp(-0.3 * layer)
    lam = (jnp.exp(jnp.sum(lp['da_lq1'] * lp['da_lk1'])) - jnp.exp(jnp.sum(lp['da_lq2'] * lp['da_lk2']))
           + lam_init).reshape(1)
    if side is None:
        k_all, v_all = k.astype(BF16), v.astype(BF16)
        side_a = (k_all, v_all)
        s0_r = jnp.zeros((2, b, RW_W // LANES, LANES, LANES), F32)
        s0_s = jnp.zeros((2, b, SSM_GROUPS, SSM_STATE, SSM_W // SSM_GROUPS), F32)
    else:
        (kc, vc), s0_r, s0_s = side
        q = _rope(q, rope_tabs)
        k_all = jnp.concatenate([kc, _rope(k, rope_tabs).astype(BF16)], axis=1)
        v_all = jnp.concatenate([vc, v.astype(BF16)], axis=1)
        side_a = None
    y_r2, aux, sf_r = _rwkv_scan(proj, b, l, lp['rw'], s0_r)
    y_s2, sf_s = _ssd_scan(proj, b, l, lp['ssm'], s0_s)
    new_side = (side_a, sf_r, sf_s)
    if not want_out:
        return None, new_side
    q = (q * (DA_HD ** -0.5)).astype(BF16)
    y_a = _diff_attention(lam, q, k_all, v_all, lp['da_norm_g'].reshape(1, 2 * DA_HD), 1.0 - lam_init)
    x1 = _merge(y_a.reshape(n, DA_W), y_r2.reshape(2, n, RW_W), aux.reshape(2, n, RW_W),
                y_s2.reshape(2, n, SSM_W), proj, x2d, g1, lp, l)
    return x1, new_side


def kernel(x, c, ctx, c_ctx, ada_w, ada_b, w_in, da_lq1, da_lk1, da_lq2, da_lk2, da_norm_g,
           rw_mu, rw_w0, rw_w2, rw_a0, rw_a2, rw_g2, rw_kk, rw_ka, rw_rk, rw_ln_g, rw_ln_b,
           ssm_conv_w, ssm_conv_b, ssm_dt_bias, ssm_a_log, ssm_d, ssm_norm_g,
           p_attn, p_rwkv, p_ssm, w_out, ln1_g, ln1_b, ffn_w1, ffn_w3, ffn_w2, ln2_g, ln2_b):
    b, l, _ = x.shape
    lc = ctx.shape[1]
    rope_tabs = _rope_tables(l)
    seg = (jnp.arange(RW_W)[:, None] // RW_HD == jnp.arange(RW_W)[None, :] // RW_HD).astype(BF16)
    xc = ctx
    for i in range(DEPTH):
        rw = {
            'mu': jnp.pad(rw_mu[i], (0, RW_BLK - RW_COLS)).reshape(1, RW_BLK),
            'w0': rw_w0[i].reshape(2, 1, RW_W),
            'w2': _pad_lora(rw_w2[i], RW_DECAY_R),
            'a0': rw_a0[i],
            'a2': _pad_lora(rw_a2[i], RW_ICLR_R),
            'g2': rw_g2[i].astype(BF16),
            'vec': jnp.stack([rw_kk[i], rw_ka[i], rw_rk[i].reshape(RW_W)]),
            'seg': seg,
        }
        ssm = {
            'conv_w': ssm_conv_w[i], 'conv_b': ssm_conv_b[i].reshape(1, SSM_XBC),
            'dt_bias': ssm_dt_bias[i].reshape(1, 2 * SSM_HEADS), 'a_log': ssm_a_log[i].reshape(1, 2 * SSM_HEADS),
            'dskip': jnp.repeat(ssm_d[i], SSM_HD).reshape(1, SSM_W),
        }
        lp = {
            'w_in_p': _permute_w_in(w_in[i]), 'rw': rw, 'ssm': ssm,
            'da_lq1': da_lq1[i], 'da_lk1': da_lk1[i], 'da_lq2': da_lq2[i], 'da_lk2': da_lk2[i],
            'da_norm_g': da_norm_g[i],
            'rw_ln_g': rw_ln_g[i].reshape(1, RW_W), 'rw_ln_b': rw_ln_b[i].reshape(1, RW_W),
            'ssm_norm_g': ssm_norm_g[i].reshape(1, SSM_W),
            'p_attn': p_attn[i].astype(BF16), 'p_rwkv': p_rwkv[i].astype(BF16),
            'p_ssm': p_ssm[i].astype(BF16), 'w_out': w_out[i].astype(BF16),
            'ln1_g': ln1_g[i].reshape(1, D_MODEL), 'ln1_b': ln1_b[i].reshape(1, D_MODEL),
        }
        w1, w3, w2 = ffn_w1[i].astype(BF16), ffn_w3[i].astype(BF16), ffn_w2[i].astype(BF16)
        l2g, l2b = ln2_g[i].reshape(1, D_MODEL), ln2_b[i].reshape(1, D_MODEL)
        last = i == DEPTH - 1
        mod_x = (jax.nn.silu(c) @ ada_w[i] + ada_b[i])[:, None, :]
        mod_c = jnp.broadcast_to((jax.nn.silu(c_ctx) @ ada_w[i] + ada_b[i])[None, None, :], (b, 1, 6 * D_MODEL))
        sh1, sc1, g1, sh2, sc2, g2 = jnp.split(mod_x, 6, axis=-1)
        csh1, csc1, cg1, csh2, csc2, cg2 = jnp.split(mod_c, 6, axis=-1)
        xc1, side = _mixer(xc, (csh1, csc1, cg1), lp, i, None, not last, None)
        x1, _ = _mixer(x, (sh1, sc1, g1), lp, i, side, True, rope_tabs)
        x = _ffn(x1, sc2, sh2, g2, w1, w3, w2, l2g, l2b, l).reshape(b, l, D_MODEL)
        if not last:
            xc = _ffn(xc1, csc2, csh2, cg2, w1, w3, w2, l2g, l2b, lc).reshape(b, lc, D_MODEL)
    return x
```

```python
import functools
import math

import jax
import jax.numpy as jnp
from jax import lax
from jax.experimental import pallas as pl
from jax.experimental.pallas import tpu as pltpu

F32 = jnp.float32
BF16 = jnp.bfloat16

D_MODEL = 1024
DEPTH = 4
GRID_W = 64
DA_HEADS = 4
DA_HD = 64
DA_W = DA_HEADS * 2 * DA_HD
ROPE_BASE = 10000.0
RW_HEADS = 8
RW_HD = 64
RW_W = RW_HEADS * RW_HD
RW_DECAY_R = 64
RW_ICLR_R = 64
RW_GATE_R = 128
RW_GN_EPS = 64e-5
RW_COLS = 3 * RW_W + 2 * RW_DECAY_R + 2 * RW_ICLR_R + RW_GATE_R
SSM_HEADS = 8
SSM_HD = 64
SSM_W = SSM_HEADS * SSM_HD
SSM_GROUPS = 2
SSM_STATE = 128
SSM_XBC = SSM_W + 2 * SSM_GROUPS * SSM_STATE
FFN_HIDDEN = (8 * D_MODEL + 3 * 256 - 1) // (3 * 256) * 256
NORM_EPS = 1e-5
W_IN_SPLIT = (DA_W, DA_W, DA_W, RW_COLS, SSM_W, SSM_XBC, 2 * SSM_HEADS, D_MODEL, D_MODEL, D_MODEL)
ALPHA = (2.0 * DEPTH) ** 0.25

LANES = 128
SUBLANES = 8
VMEM_LIMIT_BYTES = 56 * 1024 * 1024

PROJ_COLS = 8192
RW_BLK = 2048
C_RW = 0
C_DT = RW_COLS
C_GA, C_GR, C_GS = RW_BLK, RW_BLK + D_MODEL, RW_BLK + 2 * D_MODEL
C_Q = RW_BLK + 3 * D_MODEL
C_K = C_Q + DA_W
C_V = C_K + DA_W
C_Z = C_V + DA_W
C_XBC = C_Z + SSM_W

Q_SCALE = math.log2(math.e) * DA_HD ** -0.5
ATTN_SUBTILE = 128
ATTN_BLOCK = 512
RW_CHUNK = 64
RW_BLOCK = 256
SSD_CHUNK = 128


def _cparams(sem):
    return pltpu.CompilerParams(dimension_semantics=sem, vmem_limit_bytes=VMEM_LIMIT_BYTES)


def _dot(a, b):
    return jnp.dot(a, b, preferred_element_type=F32)


def _dot_nt(a, b):
    return lax.dot_general(a, b, (((1,), (1,)), ((), ())), preferred_element_type=F32)


def _dot_tn(a, b):
    return lax.dot_general(a, b, (((0,), (0,)), ((), ())), preferred_element_type=F32)


def _bf(x):
    return x.astype(BF16)


def _split3(x):
    hi = _bf(x)
    r1 = x - hi.astype(F32)
    mid = _bf(r1)
    lo = _bf(r1 - mid.astype(F32))
    return hi, mid, lo


def _dot3(a, b):
    ah, am, _ = _split3(a)
    bh, bm, _ = _split3(b)
    return _dot(ah, bh) + (_dot(ah, bm) + _dot(am, bh))


def _cumsum_rows(tri_b, x):
    n = x.shape[1]
    hi, mid, lo = _split3(x)
    o = _dot(tri_b, jnp.concatenate([hi, mid, lo], axis=1))
    return o[:, :n] + o[:, n:2 * n] + o[:, 2 * n:]


def _seg_sum(x, seg_b):
    m = x.shape[0]
    hi = _bf(x)
    lo = _bf(x - hi.astype(F32))
    o = _dot(jnp.concatenate([hi, lo], axis=0), seg_b)
    return o[:m] + o[m:]


def _sigmoid(x):
    return 1.0 / (1.0 + jnp.exp(-x))


def _softplus(x):
    return jnp.maximum(x, 0.0) + jnp.log(1.0 + jnp.exp(-jnp.abs(x)))


def _layernorm_rows(t, g, b):
    mu = jnp.mean(t, -1, keepdims=True)
    d = t - mu
    var = jnp.mean(d * d, -1, keepdims=True)
    return d * lax.rsqrt(var + NORM_EPS) * g + b


def _neighbour_rows(x, prev_row, next_row):
    t = x.shape[0]
    rowi = lax.broadcasted_iota(jnp.int32, (t, 1), 0)
    x_m1 = jnp.where(rowi == 0, prev_row, pltpu.roll(x, 1, 0))
    x_p1 = jnp.where(rowi == t - 1, next_row, pltpu.roll(x, t - 1, 0))
    return x_m1, x_p1


def _halo_specs(width, col_blk, seq_len, tb, nc, n_rows):
    per8 = tb // SUBLANES
    last8 = n_rows // SUBLANES - 1

    def blk(d, bi, c):
        return bi * nc + c + d * (nc - 1 - 2 * c)

    prev = pl.BlockSpec((SUBLANES, width), lambda d, bi, c: (jnp.maximum(blk(d, bi, c) * per8 - 1, 0), col_blk))
    nxt = pl.BlockSpec((SUBLANES, width),
                       lambda d, bi, c: (jnp.minimum((blk(d, bi, c) + 1) * per8, last8), col_blk))
    cur = pl.BlockSpec((tb, width), lambda d, bi, c: (blk(d, bi, c), col_blk))
    return cur, prev, nxt


def _inproj_kernel(x_ref, sc_ref, sh_ref, w_ref, o_ref, h_scr):
    @pl.when(pl.program_id(1) == 0)
    def _():
        h = x_ref[...] * (1.0 + sc_ref[0]) + sh_ref[0]
        h_scr[...] = h.astype(BF16)

    o_ref[...] = _dot(h_scr[...], w_ref[...])


def _inproj(x2d, sc, sh, w, seq_len):
    n = x2d.shape[0]
    tm = min(1024, seq_len)
    tn = 1024
    per_b = seq_len // tm
    return pl.pallas_call(
        _inproj_kernel,
        grid=(n // tm, PROJ_COLS // tn),
        in_specs=[
            pl.BlockSpec((tm, D_MODEL), lambda i, j: (i, 0)),
            pl.BlockSpec((1, 1, D_MODEL), lambda i, j: (i // per_b, 0, 0)),
            pl.BlockSpec((1, 1, D_MODEL), lambda i, j: (i // per_b, 0, 0)),
            pl.BlockSpec((D_MODEL, tn), lambda i, j: (0, j)),
        ],
        out_specs=pl.BlockSpec((tm, tn), lambda i, j: (i, j)),
        out_shape=jax.ShapeDtypeStruct((n, PROJ_COLS), F32),
        scratch_shapes=[pltpu.VMEM((tm, D_MODEL), BF16)],
        compiler_params=_cparams(("arbitrary", "arbitrary")),
        name="inproj",
    )(x2d, sc, sh, w)


def _rope_lanes(x, cos, s_dn, s_up):
    nf = DA_HD // 4
    return x * cos + pltpu.roll(x, LANES - nf, 1) * s_dn + pltpu.roll(x, nf, 1) * s_up


def _attn_core(lam, q, k_scr, v_scr, g_ref, o_ref, out_scale):
    tq = ATTN_SUBTILE
    nsub = q.shape[0] // tq
    lk = k_scr.shape[0]
    lane = lax.broadcasted_iota(jnp.int32, (1, LANES), 1)
    mid = (lk // 2) // LANES * LANES
    cuts = (0, mid, lk) if mid else (0, lk)
    spans = list(zip(cuts[:-1], cuts[1:]))
    scores = []
    for t in range(nsub):
        qt = q[t * tq:(t + 1) * tq]
        zero = jnp.zeros_like(qt)
        qq = jnp.concatenate([jnp.where(lane < DA_HD, qt, zero), jnp.where(lane >= DA_HD, qt, zero)], axis=0)
        scores.append([_dot_nt(qq, k_scr[lo:hi, :]) for lo, hi in spans])
    for t, ss in enumerate(scores):
        m = functools.reduce(jnp.maximum, [jnp.max(s, -1, keepdims=True) for s in ss])
        ov = functools.reduce(jnp.add, [_dot(_bf(jnp.exp2(s - m)), v_scr[lo:hi, :]) for s, (lo, hi) in zip(ss, spans)])
        on = ov[:, :LANES] / ov[:, LANES:]
        o = on[:tq] - lam * on[tq:]
        ms = jnp.mean(o * o, -1, keepdims=True)
        o_ref[t * tq:(t + 1) * tq, :] = o * lax.rsqrt(ms + NORM_EPS) * (g_ref[...] * out_scale)


def _attn_latent_kernel(lam_ref, q_ref, kl_ref, vl_ref, kc_ref, vc_ref, cos_ref, sdn_ref, sup_ref, g_ref,
                        o_ref, k_scr, v_scr, *, out_scale):
    i = pl.program_id(2)
    tq = q_ref.shape[0]
    lc = kc_ref.shape[0]

    @pl.when(i == 0)
    def _():
        k_scr[:lc, :] = _bf(kc_ref[...])
        k_scr[lc:, :] = _bf(_rope_lanes(kl_ref[...], cos_ref[...], sdn_ref[...], sup_ref[...]))
        v_scr[:lc, :LANES] = _bf(vc_ref[...])
        v_scr[lc:, :LANES] = _bf(vl_ref[...])
        v_scr[:, LANES:] = jnp.ones((v_scr.shape[0], LANES), BF16)

    rows = pl.ds(pl.multiple_of(i * tq, tq), tq)
    q = _rope_lanes(q_ref[...], cos_ref[rows, :], sdn_ref[rows, :], sup_ref[rows, :])
    _attn_core(lam_ref[0], _bf(q * Q_SCALE), k_scr, v_scr, g_ref, o_ref, out_scale)


def _attn_ctx_kernel(lam_ref, q_ref, kl_ref, vl_ref, g_ref, o_ref, k_scr, v_scr, *, out_scale):
    @pl.when(pl.program_id(2) == 0)
    def _():
        k_scr[...] = _bf(kl_ref[...])
        v_scr[:, :LANES] = _bf(vl_ref[...])
        v_scr[:, LANES:] = jnp.ones((v_scr.shape[0], LANES), BF16)

    _attn_core(lam_ref[0], _bf(q_ref[...] * Q_SCALE), k_scr, v_scr, g_ref, o_ref, out_scale)


def _diff_attention(lam, proj, batch, seq_len, g, out_scale, proj_ctx=None, ctx_len=0, rope_tabs=None):
    tq = min(ATTN_BLOCK, seq_len)
    nq = seq_len // tq
    lk = seq_len + ctx_len
    cq, ck, cv = C_Q // LANES, C_K // LANES, C_V // LANES
    smem = pl.BlockSpec(memory_space=pltpu.SMEM)
    qspec = pl.BlockSpec((tq, LANES), lambda bi, h, i: (bi * nq + i, cq + h))
    kspec = pl.BlockSpec((seq_len, LANES), lambda bi, h, i: (bi, ck + h))
    vspec = pl.BlockSpec((seq_len, LANES), lambda bi, h, i: (bi, cv + h))
    gspec = pl.BlockSpec((1, LANES), lambda bi, h, i: (0, 0))
    common = dict(
        grid=(batch, DA_HEADS, nq),
        out_specs=pl.BlockSpec((tq, LANES), lambda bi, h, i: (bi * nq + i, h)),
        out_shape=jax.ShapeDtypeStruct((batch * seq_len, DA_W), F32),
        scratch_shapes=[pltpu.VMEM((lk, LANES), BF16), pltpu.VMEM((lk, 2 * LANES), BF16)],
        compiler_params=_cparams(("arbitrary", "arbitrary", "arbitrary")),
    )
    if proj_ctx is None:
        return pl.pallas_call(
            functools.partial(_attn_ctx_kernel, out_scale=out_scale),
            in_specs=[smem, qspec, kspec, vspec, gspec], name="diff_attn_ctx", **common,
        )(lam, proj, proj, proj, g)
    tab = pl.BlockSpec((seq_len, LANES), lambda bi, h, i: (0, 0))
    return pl.pallas_call(
        functools.partial(_attn_latent_kernel, out_scale=out_scale),
        in_specs=[smem, qspec, kspec, vspec,
                  pl.BlockSpec((ctx_len, LANES), lambda bi, h, i: (bi, ck + h)),
                  pl.BlockSpec((ctx_len, LANES), lambda bi, h, i: (bi, cv + h)),
                  tab, tab, tab, gspec],
        name="diff_attn", **common,
    )(lam, proj, proj, proj, proj_ctx, proj_ctx, *rope_tabs, g)


def _rwkv_kernel(u_ref, up_ref, un_ref, mu_ref, w0_ref, w2_ref, a0_ref, a2_ref, g2_ref, vec_ref, seg_ref,
                 s0_ref, y_ref, aux_ref, sf_ref, s_scr, op_scr):
    d = pl.program_id(0)
    c = pl.program_id(2)
    nc = pl.num_programs(2)
    t = RW_CHUNK
    tb = u_ref.shape[0]
    nper = tb // t
    cc = c + d * (nc - 1 - 2 * c)

    @pl.when(c == 0)
    def _():
        s_scr[...] = s0_ref[0, 0]

    u = u_ref[...]
    prev = up_ref[SUBLANES - 1:SUBLANES, :] * (cc > 0).astype(F32)
    nxt = un_ref[0:1, :] * (cc < nc - 1).astype(F32)
    u_m1, u_p1 = _neighbour_rows(u, prev, nxt)
    u = u + mu_ref[...] * (0.5 * (u_m1 + u_p1) - u)
    r = u[:, :RW_W]
    k = u[:, RW_W:2 * RW_W]
    v = u[:, 2 * RW_W:3 * RW_W]
    o = 3 * RW_W
    wd = _bf(jnp.tanh(u[:, o:o + 2 * RW_DECAY_R]))
    o += 2 * RW_DECAY_R
    ad = _bf(u[:, o:o + 2 * RW_ICLR_R])
    o += 2 * RW_ICLR_R
    gd = u[:, o:o + RW_GATE_R]
    kk_gain, ka, rk = vec_ref[0:1, :], vec_ref[1:2, :], vec_ref[2:3, :]
    seg = seg_ref[...]
    lw_all = -math.exp(-0.5) * _sigmoid(w0_ref[0] + _dot(wd, w2_ref[0]))
    a_all = _sigmoid(a0_ref[pl.ds(d, 1), :] + _dot(ad, a2_ref[d]))
    kk_all = k * kk_gain
    kk_all = kk_all * lax.rsqrt(jnp.maximum(_seg_sum(kk_all * kk_all, seg), 1e-24))
    op_scr[0] = r
    op_scr[1] = v
    op_scr[2] = kk_all
    op_scr[3] = k * (1.0 + (a_all - 1.0) * ka)
    op_scr[4] = a_all
    op_scr[5] = lw_all

    @pl.when(d == 0)
    def _():
        a_other = _sigmoid(a0_ref[1:2, :] + _dot(ad, a2_ref[1]))
        k_bonus = k * (1.0 + (0.5 * (a_all + a_other) - 1.0) * ka)
        aux_ref[0, 0] = _seg_sum(r * k_bonus * rk, seg) * v

    @pl.when(d == 1)
    def _():
        aux_ref[0, 0] = _dot(_bf(_sigmoid(gd)), g2_ref[...])

    sign = 1 - 2 * d
    row = lax.broadcasted_iota(jnp.int32, (t, t), 0)
    col = lax.broadcasted_iota(jnp.int32, (t, t), 1)
    tri_b = ((row - col) * sign >= 0).astype(BF16)
    row2 = lax.broadcasted_iota(jnp.int32, (t, 2 * t), 0)
    col2 = lax.broadcasted_iota(jnp.int32, (t, 2 * t), 1)
    col2 = jnp.where(col2 >= t, col2 - t, col2)
    delta2 = (row2 - col2) * sign
    strict2 = delta2 > 0
    incl2 = delta2 >= 0
    colh = lax.broadcasted_iota(jnp.int32, (1, 2 * t), 1)
    cm0 = (colh < t).astype(F32)
    cm1 = 1.0 - cm0
    lane = lax.broadcasted_iota(jnp.int32, (1, LANES), 1)
    m0 = (lane < RW_HD).astype(F32)
    m1 = 1.0 - m0
    rb = lax.broadcasted_iota(jnp.int32, (LANES, LANES), 0) // RW_HD
    cb = lax.broadcasted_iota(jnp.int32, (LANES, LANES), 1) // RW_HD
    bd = (rb == cb).astype(F32)
    eye = (lax.broadcasted_iota(jnp.int32, (2 * t, 2 * t), 0)
           == lax.broadcasted_iota(jnp.int32, (2 * t, 2 * t), 1)).astype(F32)

    def stack2(x):
        return _bf(jnp.concatenate([x * m0, x * m1], axis=0))

    nhp = RW_W // LANES
    inst = [(j, i) for i in range(nper) for j in range(nhp)]

    def rows_of(i):
        return pl.ds(pl.multiple_of((i + d * (nper - 1 - 2 * i)) * t, t), t)

    def sl_of(j):
        return slice(j * LANES, (j + 1) * LANES)

    lws = [op_scr[5, rows_of(i), sl_of(j)] for j, i in inst]
    cums = [_cumsum_rows(tri_b, lw) for lw in lws]
    pre = []
    for (j, i), lw, cum in zip(inst, lws, cums):
        rows, sl = rows_of(i), sl_of(j)
        r = op_scr[0, rows, sl]
        v = op_scr[1, rows, sl]
        kk = op_scr[2, rows, sl]
        kd = op_scr[3, rows, sl]
        a = op_scr[4, rows, sl]
        tot = jnp.sum(lw, axis=0, keepdims=True)
        g_inv = jnp.exp(-cum)
        g_rem = jnp.exp(tot - cum)
        rt = r * jnp.exp(cum)
        kt = _bf(kk * jnp.exp(cum - lw))
        b = kk * a
        bh = b * g_inv
        kh = kd * g_inv
        ystack = _bf(jnp.concatenate([bh * m0, bh * m1, kh * m0, kh * m1], axis=0))
        pre.append(dict(v=v, rt=rt, kt=kt, ystack=ystack, g_tot=jnp.exp(tot), vst=stack2(v),
                        rhs_r=_bf(jnp.concatenate([kd * g_rem, b * g_rem], axis=0))))
    ggs = [_dot_nt(jnp.concatenate([p['kt'], _bf(p['rt'])], axis=0), p['ystack']) for p in pre]
    for p, gg in zip(pre, ggs):
        mcat = jnp.where(strict2, gg[:t, :2 * t], 0.0)
        p['ncat'] = _bf(jnp.where(strict2, gg[:t, 2 * t:], 0.0))
        p['qcat'] = _bf(jnp.where(incl2, gg[t:, :2 * t], 0.0))
        p['pcat'] = _bf(jnp.where(incl2, gg[t:, 2 * t:], 0.0))
        p['x0'] = -jnp.concatenate([mcat * cm0, mcat * cm1], axis=0)
    n_dbl = int(math.log2(t))
    ws = [eye + p['x0'] for p in pre]
    xs = [_dot(_bf(p['x0']), _bf(p['x0'])) for p in pre]
    nvs = [_dot(p['ncat'], p['vst']) for p in pre]
    for step in range(1, n_dbl):
        last = step == n_dbl - 1
        outs = [_dot(_bf(w if last else jnp.concatenate([w, x], axis=0)), _bf(x)) for w, x in zip(ws, xs)]
        ws = [w + o[:2 * t] for w, o in zip(ws, outs)]
        if not last:
            xs = [o[2 * t:] for o in outs]
    errs = [eye - _dot3(eye - p['x0'], w) for p, w in zip(pre, ws)]
    ws = [w + _dot(_bf(w), _bf(e)) for w, e in zip(ws, errs)]
    wcats = [_bf(w[:t] + w[t:]) for w in ws]
    uks = [_dot(wc, jnp.concatenate([stack2(nv), stack2(p['kt'].astype(F32))], axis=1))
           for wc, nv, p in zip(wcats, nvs, pre)]
    qus = [_dot(p['qcat'], jnp.concatenate([stack2(uk[:, :LANES]), stack2(uk[:, LANES:])], axis=1))
           for p, uk in zip(pre, uks)]
    pvs = [_dot(p['pcat'], p['vst']) for p in pre]
    zeros_t = jnp.zeros((t, LANES), F32)
    dcs = [_dot_tn(_bf(jnp.concatenate([jnp.concatenate([p['v'], -uk[:, :LANES]], axis=0),
                                        jnp.concatenate([zeros_t, uk[:, LANES:]], axis=0)], axis=1)), p['rhs_r'])
           for p, uk in zip(pre, uks)]
    for p, qu, pv, dc in zip(pre, qus, pvs, dcs):
        p['y0'] = pv - qu[:, :LANES]
        p['rp'] = _bf(p['rt'] - qu[:, LANES:])
        p['d0'] = bd * dc[:LANES]
        p['c0'] = _bf(bd * dc[LANES:])
    states = [s_scr[j] for j in range(nhp)]
    for i in range(nper):
        ps = [pre[i * nhp + j] for j in range(nhp)]
        sbs = [_bf(sbd) for sbd in states]
        ys = [_dot_nt(p['rp'], sb) for p, sb in zip(ps, sbs)]
        scs = [_dot(sb, p['c0']) for p, sb in zip(ps, sbs)]
        for j in range(nhp):
            y_ref[0, 0, rows_of(i), sl_of(j)] = ps[j]['y0'] + ys[j]
        states = [sbd * p['g_tot'] - sc + p['d0'] for sbd, p, sc in zip(states, ps, scs)]
    for j in range(nhp):
        s_scr[j] = states[j]

    @pl.when(c == nc - 1)
    def _():
        sf_ref[0, 0] = s_scr[...]


def _rwkv_scan(proj, batch, seq_len, rwp, s0):
    n = proj.shape[0]
    tb = min(RW_BLOCK, seq_len)
    nc = seq_len // tb
    nhp = RW_W // LANES
    cur, prev, nxt = _halo_specs(RW_BLK, C_RW // RW_BLK, seq_len, tb, nc, n)
    full2 = lambda d, bi, c: (0, 0)
    full3 = lambda d, bi, c: (0, 0, 0)
    dmap = lambda d, bi, c: (d, bi, c + d * (nc - 1 - 2 * c), 0)
    smap = lambda d, bi, c: (d, bi, 0, 0, 0)
    return pl.pallas_call(
        _rwkv_kernel,
        grid=(2, batch, nc),
        in_specs=[
            cur, prev, nxt,
            pl.BlockSpec((1, RW_BLK), full2),
            pl.BlockSpec((1, 1, RW_W), lambda d, bi, c: (d, 0, 0)),
            pl.BlockSpec((1, 2 * RW_DECAY_R, RW_W), lambda d, bi, c: (d, 0, 0)),
            pl.BlockSpec((2, RW_W), full2),
            pl.BlockSpec((2, 2 * RW_ICLR_R, RW_W), full3),
            pl.BlockSpec((RW_GATE_R, RW_W), full2),
            pl.BlockSpec((3, RW_W), full2),
            pl.BlockSpec((RW_W, RW_W), full2),
            pl.BlockSpec((1, 1, nhp, LANES, LANES), smap),
        ],
        out_specs=[
            pl.BlockSpec((1, 1, tb, RW_W), dmap),
            pl.BlockSpec((1, 1, tb, RW_W), dmap),
            pl.BlockSpec((1, 1, nhp, LANES, LANES), smap),
        ],
        out_shape=[
            jax.ShapeDtypeStruct((2, batch, seq_len, RW_W), F32),
            jax.ShapeDtypeStruct((2, batch, seq_len, RW_W), F32),
            jax.ShapeDtypeStruct((2, batch, nhp, LANES, LANES), F32),
        ],
        scratch_shapes=[pltpu.VMEM((nhp, LANES, LANES), F32), pltpu.VMEM((6, tb, RW_W), F32)],
        compiler_params=_cparams(("arbitrary", "arbitrary", "arbitrary")),
        name="rwkv_scan",
    )(proj, proj, proj, rwp['mu'], rwp['w0'], rwp['w2'], rwp['a0'], rwp['a2'], rwp['g2'], rwp['vec'],
      rwp['seg'], s0)


def _ssd_kernel(x_ref, xp_ref, xn_ref, dt_ref, cw_ref, cb_ref, dtb_ref, alog_ref, dskip_ref, s0_ref,
                y_ref, sf_ref, s_scr):
    d = pl.program_id(0)
    c = pl.program_id(2)
    nc = pl.num_programs(2)
    t = SSD_CHUNK
    gw = SSM_W // SSM_GROUPS
    hpg = SSM_HEADS // SSM_GROUPS
    nh = SSM_HEADS
    cc = c + d * (nc - 1 - 2 * c)

    @pl.when(c == 0)
    def _():
        s_scr[...] = s0_ref[0, 0]

    x = x_ref[...]
    prev = xp_ref[SUBLANES - 1:SUBLANES, :] * (cc > 0).astype(F32)
    nxt = xn_ref[0:1, :] * (cc < nc - 1).astype(F32)
    x_m1, x_p1 = _neighbour_rows(x, prev, nxt)
    xc = x_m1 * cw_ref[0:1, :] + x * cw_ref[1:2, :] + x_p1 * cw_ref[2:3, :] + cb_ref[...]
    xc = xc * _sigmoid(xc)
    xs = xc[:, :SSM_W]
    bm = xc[:, SSM_W:SSM_W + SSM_GROUPS * SSM_STATE]
    cm = xc[:, SSM_W + SSM_GROUPS * SSM_STATE:]
    dt2 = _softplus(dt_ref[:, :2 * nh] + dtb_ref[...])
    a2 = dt2 * (-jnp.exp(alog_ref[...]))
    fwd = d == 0
    dt = jnp.where(fwd, dt2[:, :nh], dt2[:, nh:])
    a = jnp.where(fwd, a2[:, :nh], a2[:, nh:])

    sign = 1 - 2 * d
    row = lax.broadcasted_iota(jnp.int32, (t, t), 0)
    col = lax.broadcasted_iota(jnp.int32, (t, t), 1)
    delta = (row - col) * sign
    incl = delta >= 0
    tri_b = incl.astype(BF16)
    tri_t_b = (delta <= 0).astype(BF16)
    lane = lax.broadcasted_iota(jnp.int32, (1, LANES), 1)
    m0 = (lane < SSM_HD).astype(F32)
    m1 = 1.0 - m0
    lane_g = lax.broadcasted_iota(jnp.int32, (1, gw), 1) // SSM_HD

    a3 = jnp.concatenate(_split3(a), axis=1)
    cs3 = _dot(tri_b, a3)
    cs = cs3[:, :nh] + cs3[:, nh:2 * nh] + cs3[:, 2 * nh:]
    cs3_t = _dot_tn(a3, tri_t_b)
    cs_t = cs3_t[:nh] + cs3_t[nh:2 * nh] + cs3_t[2 * nh:]
    tot = jnp.sum(a, axis=0, keepdims=True)
    e_cs = jnp.exp(cs)
    e_rem = jnp.exp(tot - cs)
    e_tot = jnp.exp(tot)

    def expand(z, g):
        out = jnp.zeros((z.shape[0], gw), F32)
        for hh in range(hpg):
            h = g * hpg + hh
            out = jnp.where(lane_g == hh, z[:, h:h + 1], out)
        return out

    skip = fwd.astype(F32)
    for g in range(SSM_GROUPS):
        cg = cm[:, g * SSM_STATE:(g + 1) * SSM_STATE]
        bg = bm[:, g * SSM_STATE:(g + 1) * SSM_STATE]
        xsg = xs[:, g * gw:(g + 1) * gw]
        xg = xsg * expand(dt, g)
        cgb = _bf(cg)
        bgb = _bf(bg)
        gm = _dot_nt(cgb, bgb)
        st = s_scr[g]
        y_off = _dot(cgb, _bf(st)) * expand(e_cs, g)
        ys = []
        for hp in range(hpg // 2):
            scs = []
            for hh in range(2):
                h = g * hpg + 2 * hp + hh
                diff = jnp.where(incl, cs[:, h:h + 1] - cs_t[h:h + 1, :], 0.0)
                scs.append(gm * jnp.where(incl, jnp.exp(diff), 0.0))
            xp = xg[:, hp * LANES:(hp + 1) * LANES]
            xst = _bf(jnp.concatenate([xp * m0, xp * m1], axis=0))
            ys.append(_dot(_bf(jnp.concatenate(scs, axis=1)), xst))
        y_ref[0, 0, :, g * gw:(g + 1) * gw] = (jnp.concatenate(ys, axis=1) + y_off
                                               + skip * dskip_ref[:, g * gw:(g + 1) * gw] * xsg)
        s_scr[g] = st * expand(e_tot, g) + _dot_tn(bgb, _bf(xg * expand(e_rem, g)))

    @pl.when(c == nc - 1)
    def _():
        sf_ref[0, 0] = s_scr[...]


def _ssd_scan(proj, batch, seq_len, ssp, s0):
    n = proj.shape[0]
    t = min(SSD_CHUNK, seq_len)
    nc = seq_len // t
    gw = SSM_W // SSM_GROUPS
    cur, prev, nxt = _halo_specs(SSM_XBC, C_XBC // SSM_XBC, seq_len, t, nc, n)
    dtspec, _, _ = _halo_specs(LANES, C_DT // LANES, seq_len, t, nc, n)
    full2 = lambda d, bi, c: (0, 0)
    dmap = lambda d, bi, c: (d, bi, c + d * (nc - 1 - 2 * c), 0)
    smap = lambda d, bi, c: (d, bi, 0, 0, 0)
    return pl.pallas_call(
        _ssd_kernel,
        grid=(2, batch, nc),
        in_specs=[
            cur, prev, nxt, dtspec,
            pl.BlockSpec((3, SSM_XBC), full2),
            pl.BlockSpec((1, SSM_XBC), full2),
            pl.BlockSpec((1, 2 * SSM_HEADS), full2),
            pl.BlockSpec((1, 2 * SSM_HEADS), full2),
            pl.BlockSpec((1, SSM_W), full2),
            pl.BlockSpec((1, 1, SSM_GROUPS, SSM_STATE, gw), smap),
        ],
        out_specs=[
            pl.BlockSpec((1, 1, t, SSM_W), dmap),
            pl.BlockSpec((1, 1, SSM_GROUPS, SSM_STATE, gw), smap),
        ],
        out_shape=[
            jax.ShapeDtypeStruct((2, batch, seq_len, SSM_W), F32),
            jax.ShapeDtypeStruct((2, batch, SSM_GROUPS, SSM_STATE, gw), F32),
        ],
        scratch_shapes=[pltpu.VMEM((SSM_GROUPS, SSM_STATE, gw), F32)],
        compiler_params=_cparams(("arbitrary", "arbitrary", "arbitrary")),
        name="ssd_scan",
    )(proj, proj, proj, proj, ssp['conv_w'], ssp['conv_b'], ssp['dt_bias'], ssp['a_log'], ssp['dskip'], s0)


def _merge_kernel(ya_ref, yr_ref, aux_ref, ys_ref, z_ref, ga_ref, gr_ref, gs_ref, x_ref, g1_ref,
                  seg_ref, rg_ref, rb_ref, sg_ref, pa_ref, pr_ref, ps_ref, wo_ref, lg_ref, lb_ref, o_ref):
    seg = seg_ref[...]
    yr = yr_ref[0] + yr_ref[1]
    mu = _seg_sum(yr, seg) * (1.0 / RW_HD)
    dv = yr - mu
    var = _seg_sum(dv * dv, seg) * (1.0 / RW_HD)
    yr = (dv * lax.rsqrt(var + RW_GN_EPS) * rg_ref[...] + rb_ref[...] + aux_ref[0]) * aux_ref[1]
    z = z_ref[...]
    ys = (ys_ref[0] + ys_ref[1]) * (z * _sigmoid(z))
    gw = SSM_W // SSM_GROUPS
    parts = []
    for g in range(SSM_GROUPS):
        yg = ys[:, g * gw:(g + 1) * gw]
        parts.append(yg * lax.rsqrt(jnp.mean(yg * yg, -1, keepdims=True) + NORM_EPS))
    ys = jnp.concatenate(parts, axis=1) * sg_ref[...]
    m = (_sigmoid(ga_ref[...]) * _dot(_bf(ya_ref[...]), pa_ref[...])
         + _sigmoid(gr_ref[...]) * _dot(_bf(yr), pr_ref[...])
         + _sigmoid(gs_ref[...]) * _dot(_bf(ys), ps_ref[...]))
    out = _dot(_bf(m), wo_ref[...])
    o_ref[...] = _layernorm_rows(ALPHA * x_ref[...] + g1_ref[0] * out, lg_ref[...], lb_ref[...])


def _merge(ya, yr2, aux, ys2, proj, x2d, g1, lp, seq_len):
    n = x2d.shape[0]
    tm = min(256, seq_len)
    per_b = seq_len // tm
    row = lambda i: (i, 0)
    row3 = lambda i: (0, i, 0)
    full = lambda i: (0, 0)
    gate = lambda col: pl.BlockSpec((tm, D_MODEL), lambda i: (i, col // D_MODEL))
    vec512 = pl.BlockSpec((1, RW_W), full)
    vec = pl.BlockSpec((1, D_MODEL), full)
    return pl.pallas_call(
        _merge_kernel,
        grid=(n // tm,),
        in_specs=[
            pl.BlockSpec((tm, DA_W), row),
            pl.BlockSpec((2, tm, RW_W), row3), pl.BlockSpec((2, tm, RW_W), row3),
            pl.BlockSpec((2, tm, SSM_W), row3),
            pl.BlockSpec((tm, SSM_W), lambda i: (i, C_Z // SSM_W)),
            gate(C_GA), gate(C_GR), gate(C_GS),
            pl.BlockSpec((tm, D_MODEL), row),
            pl.BlockSpec((1, 1, D_MODEL), lambda i: (i // per_b, 0, 0)),
            pl.BlockSpec((RW_W, RW_W), full), vec512, vec512, vec512,
            pl.BlockSpec((DA_W, D_MODEL), full), pl.BlockSpec((RW_W, D_MODEL), full),
            pl.BlockSpec((SSM_W, D_MODEL), full), pl.BlockSpec((D_MODEL, D_MODEL), full),
            vec, vec,
        ],
        out_specs=pl.BlockSpec((tm, D_MODEL), row),
        out_shape=jax.ShapeDtypeStruct((n, D_MODEL), F32),
        compiler_params=_cparams(("arbitrary",)),
        name="merge_ln",
    )(ya, yr2, aux, ys2, proj, proj, proj, proj, x2d, g1, lp['rw']['seg'], lp['rw_ln_g'], lp['rw_ln_b'],
      lp['ssm_norm_g'], lp['p_attn'], lp['p_rwkv'], lp['p_ssm'], lp['w_out'], lp['ln1_g'], lp['ln1_b'])


def _ffn_kernel(x_ref, sc_ref, sh_ref, g2_ref, w1_ref, w3_ref, w2_ref, lg_ref, lb_ref, o_ref, h_scr, acc_scr):
    j = pl.program_id(1)

    @pl.when(j == 0)
    def _():
        h_scr[...] = (x_ref[...] * (1.0 + sc_ref[0]) + sh_ref[0]).astype(BF16)
        acc_scr[...] = jnp.zeros_like(acc_scr)

    h = h_scr[...]
    u = jax.nn.silu(_dot(h, w1_ref[...])) * _dot(h, w3_ref[...])
    acc_scr[...] += _dot(u.astype(BF16), w2_ref[...])

    @pl.when(j == pl.num_programs(1) - 1)
    def _():
        o_ref[...] = _layernorm_rows(ALPHA * x_ref[...] + g2_ref[0] * acc_scr[...], lg_ref[...], lb_ref[...])


def _ffn(x2d, sc, sh, g2, w1, w3, w2, lg, lb, seq_len):
    n = x2d.shape[0]
    tm = min(512, seq_len)
    th = FFN_HIDDEN // 2
    per_b = seq_len // tm
    mod = pl.BlockSpec((1, 1, D_MODEL), lambda i, j: (i // per_b, 0, 0))
    return pl.pallas_call(
        _ffn_kernel,
        grid=(n // tm, FFN_HIDDEN // th),
        in_specs=[
            pl.BlockSpec((tm, D_MODEL), lambda i, j: (i, 0)),
            mod, mod, mod,
            pl.BlockSpec((D_MODEL, th), lambda i, j: (0, j)),
            pl.BlockSpec((D_MODEL, th), lambda i, j: (0, j)),
            pl.BlockSpec((th, D_MODEL), lambda i, j: (j, 0)),
            pl.BlockSpec((1, D_MODEL), lambda i, j: (0, 0)),
            pl.BlockSpec((1, D_MODEL), lambda i, j: (0, 0)),
        ],
        out_specs=pl.BlockSpec((tm, D_MODEL), lambda i, j: (i, 0)),
        out_shape=jax.ShapeDtypeStruct((n, D_MODEL), F32),
        scratch_shapes=[pltpu.VMEM((tm, D_MODEL), BF16), pltpu.VMEM((tm, D_MODEL), F32)],
        compiler_params=_cparams(("arbitrary", "arbitrary")),
        name="ffn_ln",
    )(x2d, sc, sh, g2, w1, w3, w2, lg, lb)


def _rope_tables(n_tok):
    rows = n_tok // GRID_W
    row = jnp.repeat(jnp.arange(rows), GRID_W).astype(F32)
    col = jnp.tile(jnp.arange(GRID_W), rows).astype(F32)
    nf = DA_HD // 4
    inv = ROPE_BASE ** (-jnp.arange(nf, dtype=F32) / nf)
    ang = jnp.concatenate([row[:, None] * inv] * 2 + [col[:, None] * inv] * 2, axis=-1)
    cos = jnp.cos(ang)
    sin = jnp.sin(ang)
    first = (jnp.arange(DA_HD) % (2 * nf)) < nf
    s_dn = jnp.where(first, -sin, 0.0)
    s_up = jnp.where(first, 0.0, sin)
    reps = LANES // DA_HD
    return jnp.tile(cos, (1, reps)), jnp.tile(s_dn, (1, reps)), jnp.tile(s_up, (1, reps))


def _permute_w_in(w_in):
    parts, acc = [], 0
    for s in W_IN_SPLIT:
        parts.append(w_in[:, acc:acc + s])
        acc += s
    q, k, v, rw, z, xbc, dt, ga, gr, gs = parts
    pad = jnp.zeros((w_in.shape[0], RW_BLK - RW_COLS - dt.shape[1]), w_in.dtype)
    return jnp.concatenate([rw, dt, pad, ga, gr, gs, q, k, v, z, xbc], axis=1).astype(BF16)


def _pad_lora(w, rank):
    z = jnp.zeros_like(w[0])
    return jnp.stack([jnp.concatenate([w[0], z], axis=0), jnp.concatenate([z, w[1]], axis=0)]).astype(BF16)


def _mixer(x, mod, lp, layer, side, want_out, rope_tabs):
    b, l, _ = x.shape
    sh1, sc1, g1 = mod
    n = b * l
    x2d = x.reshape(n, D_MODEL)
    proj = _inproj(x2d, sc1, sh1, lp['w_in_p'], l)
    if side is None:
        proj_ctx, ctx_len = None, 0
        s0_r = jnp.zeros((2, b, RW_W // LANES, LANES, LANES), F32)
        s0_s = jnp.zeros((2, b, SSM_GROUPS, SSM_STATE, SSM_W // SSM_GROUPS), F32)
    else:
        (proj_ctx, ctx_len), s0_r, s0_s = side
    y_r2, aux, sf_r = _rwkv_scan(proj, b, l, lp['rw'], s0_r)
    y_s2, sf_s = _ssd_scan(proj, b, l, lp['ssm'], s0_s)
    new_side = ((proj, l), sf_r, sf_s)
    if not want_out:
        return None, new_side
    lam_init = 0.8 - 0.6 * math.exp(-0.3 * layer)
    lam = (jnp.exp(jnp.sum(lp['da_lq1'] * lp['da_lk1'])) - jnp.exp(jnp.sum(lp['da_lq2'] * lp['da_lk2']))
           + lam_init).reshape(1)
    y_a = _diff_attention(lam, proj, b, l, lp['da_norm_g'].reshape(1, 2 * DA_HD), 1.0 - lam_init,
                          proj_ctx, ctx_len, rope_tabs)
    x1 = _merge(y_a, y_r2.reshape(2, n, RW_W), aux.reshape(2, n, RW_W),
                y_s2.reshape(2, n, SSM_W), proj, x2d, g1, lp, l)
    return x1, new_side


def kernel(x, c, ctx, c_ctx, ada_w, ada_b, w_in, da_lq1, da_lk1, da_lq2, da_lk2, da_norm_g,
           rw_mu, rw_w0, rw_w2, rw_a0, rw_a2, rw_g2, rw_kk, rw_ka, rw_rk, rw_ln_g, rw_ln_b,
           ssm_conv_w, ssm_conv_b, ssm_dt_bias, ssm_a_log, ssm_d, ssm_norm_g,
           p_attn, p_rwkv, p_ssm, w_out, ln1_g, ln1_b, ffn_w1, ffn_w3, ffn_w2, ln2_g, ln2_b):
    b, l, _ = x.shape
    lc = ctx.shape[1]
    rope_tabs = _rope_tables(l)
    seg = (jnp.arange(RW_W)[:, None] // RW_HD == jnp.arange(RW_W)[None, :] // RW_HD).astype(BF16)
    xc = ctx
    for i in range(DEPTH):
        rw = {
            'mu': jnp.pad(rw_mu[i], (0, RW_BLK - RW_COLS)).reshape(1, RW_BLK),
            'w0': rw_w0[i].reshape(2, 1, RW_W),
            'w2': _pad_lora(rw_w2[i], RW_DECAY_R),
            'a0': rw_a0[i],
            'a2': _pad_lora(rw_a2[i], RW_ICLR_R),
            'g2': rw_g2[i].astype(BF16),
            'vec': jnp.stack([rw_kk[i], rw_ka[i], rw_rk[i].reshape(RW_W)]),
            'seg': seg,
        }
        ssm = {
            'conv_w': ssm_conv_w[i], 'conv_b': ssm_conv_b[i].reshape(1, SSM_XBC),
            'dt_bias': ssm_dt_bias[i].reshape(1, 2 * SSM_HEADS), 'a_log': ssm_a_log[i].reshape(1, 2 * SSM_HEADS),
            'dskip': jnp.repeat(ssm_d[i], SSM_HD).reshape(1, SSM_W),
        }
        lp = {
            'w_in_p': _permute_w_in(w_in[i]), 'rw': rw, 'ssm': ssm,
            'da_lq1': da_lq1[i], 'da_lk1': da_lk1[i], 'da_lq2': da_lq2[i], 'da_lk2': da_lk2[i],
            'da_norm_g': da_norm_g[i],
            'rw_ln_g': rw_ln_g[i].reshape(1, RW_W), 'rw_ln_b': rw_ln_b[i].reshape(1, RW_W),
            'ssm_norm_g': ssm_norm_g[i].reshape(1, SSM_W),
            'p_attn': p_attn[i].astype(BF16), 'p_rwkv': p_rwkv[i].astype(BF16),
            'p_ssm': p_ssm[i].astype(BF16), 'w_out': w_out[i].astype(BF16),
            'ln1_g': ln1_g[i].reshape(1, D_MODEL), 'ln1_b': ln1_b[i].reshape(1, D_MODEL),
        }
        w1, w3, w2 = ffn_w1[i].astype(BF16), ffn_w3[i].astype(BF16), ffn_w2[i].astype(BF16)
        l2g, l2b = ln2_g[i].reshape(1, D_MODEL), ln2_b[i].reshape(1, D_MODEL)
        last = i == DEPTH - 1
        mod_x = (jax.nn.silu(c) @ ada_w[i] + ada_b[i])[:, None, :]
        mod_c = jnp.broadcast_to((jax.nn.silu(c_ctx) @ ada_w[i] + ada_b[i])[None, None, :], (b, 1, 6 * D_MODEL))
        sh1, sc1, g1, sh2, sc2, g2 = jnp.split(mod_x, 6, axis=-1)
        csh1, csc1, cg1, csh2, csc2, cg2 = jnp.split(mod_c, 6, axis=-1)
        xc1, side = _mixer(xc, (csh1, csc1, cg1), lp, i, None, not last, None)
        x1, _ = _mixer(x, (sh1, sc1, g1), lp, i, side, True, rope_tabs)
        x = _ffn(x1, sc2, sh2, g2, w1, w3, w2, l2g, l2b, l).reshape(b, l, D_MODEL)
        if not last:
            xc = _ffn(xc1, csc2, csh2, cg2, w1, w3, w2, l2g, l2b, lc).reshape(b, lc, D_MODEL)
    return x
```

```python
import functools
import math

import jax
import jax.numpy as jnp
from jax import lax
from jax.experimental import pallas as pl
from jax.experimental.pallas import tpu as pltpu

F32 = jnp.float32
BF16 = jnp.bfloat16

D_MODEL = 1024
DEPTH = 4
GRID_W = 64
DA_HEADS = 4
DA_HD = 64
DA_W = DA_HEADS * 2 * DA_HD
ROPE_BASE = 10000.0
RW_HEADS = 8
RW_HD = 64
RW_W = RW_HEADS * RW_HD
RW_DECAY_R = 64
RW_ICLR_R = 64
RW_GATE_R = 128
RW_GN_EPS = 64e-5
RW_COLS = 3 * RW_W + 2 * RW_DECAY_R + 2 * RW_ICLR_R + RW_GATE_R
SSM_HEADS = 8
SSM_HD = 64
SSM_W = SSM_HEADS * SSM_HD
SSM_GROUPS = 2
SSM_STATE = 128
SSM_XBC = SSM_W + 2 * SSM_GROUPS * SSM_STATE
FFN_HIDDEN = (8 * D_MODEL + 3 * 256 - 1) // (3 * 256) * 256
NORM_EPS = 1e-5
W_IN_SPLIT = (DA_W, DA_W, DA_W, RW_COLS, SSM_W, SSM_XBC, 2 * SSM_HEADS, D_MODEL, D_MODEL, D_MODEL)
ALPHA = (2.0 * DEPTH) ** 0.25

LANES = 128
SUBLANES = 8
HALO_ROWS = 16
VMEM_LIMIT_BYTES = 56 * 1024 * 1024

PROJ_COLS = 8192
RW_BLK = 2048
C_RW = 0
C_DT = RW_COLS
C_GA, C_GR, C_GS = RW_BLK, RW_BLK + D_MODEL, RW_BLK + 2 * D_MODEL
C_Q = RW_BLK + 3 * D_MODEL
C_K = C_Q + DA_W
C_V = C_K + DA_W
C_Z = C_V + DA_W
C_XBC = C_Z + SSM_W

Q_SCALE = math.log2(math.e) * DA_HD ** -0.5
ATTN_SUBTILE = 128
ATTN_BLOCK = 512
RW_CHUNK = 64
RW_BLOCK = 256
SSD_CHUNK = 128


def _cparams(sem):
    return pltpu.CompilerParams(dimension_semantics=sem, vmem_limit_bytes=VMEM_LIMIT_BYTES)


def _dot(a, b):
    return jnp.dot(a, b, preferred_element_type=F32)


def _dot_nt(a, b):
    return lax.dot_general(a, b, (((1,), (1,)), ((), ())), preferred_element_type=F32)


def _dot_tn(a, b):
    return lax.dot_general(a, b, (((0,), (0,)), ((), ())), preferred_element_type=F32)


def _bf(x):
    return x.astype(BF16)


def _split3(x):
    hi = _bf(x)
    r1 = x - hi.astype(F32)
    mid = _bf(r1)
    lo = _bf(r1 - mid.astype(F32))
    return hi, mid, lo


def _dot3(a, b):
    ah, am, _ = _split3(a)
    bh, bm, _ = _split3(b)
    return _dot(ah, bh) + (_dot(ah, bm) + _dot(am, bh))


def _cumsum_rows(tri_b, x):
    n = x.shape[1]
    hi, mid, lo = _split3(x)
    o = _dot(tri_b, jnp.concatenate([hi, mid, lo], axis=1))
    return o[:, :n] + o[:, n:2 * n] + o[:, 2 * n:]


def _seg_sum(x, seg_b):
    m = x.shape[0]
    hi = _bf(x)
    lo = _bf(x - hi.astype(F32))
    o = _dot(jnp.concatenate([hi, lo], axis=0), seg_b)
    return o[:m] + o[m:]


def _sigmoid(x):
    return 1.0 / (1.0 + jnp.exp(-x))


def _softplus(x):
    return jnp.maximum(x, 0.0) + jnp.log(1.0 + jnp.exp(-jnp.abs(x)))


def _layernorm_rows(t, g, b):
    mu = jnp.mean(t, -1, keepdims=True)
    d = t - mu
    var = jnp.mean(d * d, -1, keepdims=True)
    return d * lax.rsqrt(var + NORM_EPS) * g + b


def _neighbour_rows(x, prev_row, next_row):
    t = x.shape[0]
    rowi = lax.broadcasted_iota(jnp.int32, (t, 1), 0)
    x_m1 = jnp.where(rowi == 0, prev_row, pltpu.roll(x, 1, 0))
    x_p1 = jnp.where(rowi == t - 1, next_row, pltpu.roll(x, t - 1, 0))
    return x_m1, x_p1


def _halo_specs(width, col_blk, seq_len, tb, nc, n_rows):
    per8 = tb // HALO_ROWS
    last8 = n_rows // HALO_ROWS - 1

    def blk(d, bi, c):
        return bi * nc + c + d * (nc - 1 - 2 * c)

    prev = pl.BlockSpec((HALO_ROWS, width), lambda d, bi, c: (jnp.maximum(blk(d, bi, c) * per8 - 1, 0), col_blk))
    nxt = pl.BlockSpec((HALO_ROWS, width),
                       lambda d, bi, c: (jnp.minimum((blk(d, bi, c) + 1) * per8, last8), col_blk))
    cur = pl.BlockSpec((tb, width), lambda d, bi, c: (blk(d, bi, c), col_blk))
    return cur, prev, nxt


def _inproj_kernel(x_ref, sc_ref, sh_ref, w_ref, o_ref, h_scr):
    @pl.when(pl.program_id(1) == 0)
    def _():
        h = x_ref[...] * (1.0 + sc_ref[0]) + sh_ref[0]
        h_scr[...] = h.astype(BF16)

    o_ref[...] = _bf(_dot(h_scr[...], w_ref[...]))


def _inproj(x2d, sc, sh, w, seq_len):
    n = x2d.shape[0]
    tm = min(1024, seq_len)
    tn = 1024
    per_b = seq_len // tm
    return pl.pallas_call(
        _inproj_kernel,
        grid=(n // tm, PROJ_COLS // tn),
        in_specs=[
            pl.BlockSpec((tm, D_MODEL), lambda i, j: (i, 0)),
            pl.BlockSpec((1, 1, D_MODEL), lambda i, j: (i // per_b, 0, 0)),
            pl.BlockSpec((1, 1, D_MODEL), lambda i, j: (i // per_b, 0, 0)),
            pl.BlockSpec((D_MODEL, tn), lambda i, j: (0, j)),
        ],
        out_specs=pl.BlockSpec((tm, tn), lambda i, j: (i, j)),
        out_shape=jax.ShapeDtypeStruct((n, PROJ_COLS), BF16),
        scratch_shapes=[pltpu.VMEM((tm, D_MODEL), BF16)],
        compiler_params=_cparams(("arbitrary", "arbitrary")),
        name="inproj",
    )(x2d, sc, sh, w)


def _rope_lanes(x, cos, s_dn, s_up):
    nf = DA_HD // 4
    return x * cos + pltpu.roll(x, LANES - nf, 1) * s_dn + pltpu.roll(x, nf, 1) * s_up


def _attn_core(lam, q, k_scr, v_scr, g_ref, o_ref, out_scale):
    tq = ATTN_SUBTILE
    nsub = q.shape[0] // tq
    lk = k_scr.shape[0]
    lane = lax.broadcasted_iota(jnp.int32, (1, LANES), 1)
    mid = (lk // 2) // LANES * LANES
    cuts = (0, mid, lk) if mid else (0, lk)
    spans = list(zip(cuts[:-1], cuts[1:]))
    scores = []
    for t in range(nsub):
        qt = q[t * tq:(t + 1) * tq]
        zero = jnp.zeros_like(qt)
        qq = jnp.concatenate([jnp.where(lane < DA_HD, qt, zero), jnp.where(lane >= DA_HD, qt, zero)], axis=0)
        scores.append([_dot_nt(qq, k_scr[lo:hi, :]) for lo, hi in spans])
    for t, ss in enumerate(scores):
        m = functools.reduce(jnp.maximum, [jnp.max(s, -1, keepdims=True) for s in ss])
        ov = functools.reduce(jnp.add, [_dot(_bf(jnp.exp2(s - m)), v_scr[lo:hi, :]) for s, (lo, hi) in zip(ss, spans)])
        on = ov[:, :LANES] / ov[:, LANES:]
        o = on[:tq] - lam * on[tq:]
        ms = jnp.mean(o * o, -1, keepdims=True)
        o_ref[t * tq:(t + 1) * tq, :] = _bf(o * lax.rsqrt(ms + NORM_EPS) * (g_ref[...] * out_scale))


def _attn_latent_kernel(lam_ref, q_ref, kl_ref, vl_ref, kc_ref, vc_ref, cos_ref, sdn_ref, sup_ref, g_ref,
                        o_ref, k_scr, v_scr, *, out_scale):
    i = pl.program_id(2)
    tq = q_ref.shape[0]
    lc = kc_ref.shape[0]

    @pl.when(i == 0)
    def _():
        k_scr[:lc, :] = kc_ref[...]
        k_scr[lc:, :] = _bf(_rope_lanes(kl_ref[...].astype(F32), cos_ref[...], sdn_ref[...], sup_ref[...]))
        v_scr[:lc, :LANES] = vc_ref[...]
        v_scr[lc:, :LANES] = vl_ref[...]
        v_scr[:, LANES:] = jnp.ones((v_scr.shape[0], LANES), BF16)

    rows = pl.ds(pl.multiple_of(i * tq, tq), tq)
    q = _rope_lanes(q_ref[...].astype(F32), cos_ref[rows, :], sdn_ref[rows, :], sup_ref[rows, :])
    _attn_core(lam_ref[0], _bf(q * Q_SCALE), k_scr, v_scr, g_ref, o_ref, out_scale)


def _attn_ctx_kernel(lam_ref, q_ref, kl_ref, vl_ref, g_ref, o_ref, k_scr, v_scr, *, out_scale):
    @pl.when(pl.program_id(2) == 0)
    def _():
        k_scr[...] = kl_ref[...]
        v_scr[:, :LANES] = vl_ref[...]
        v_scr[:, LANES:] = jnp.ones((v_scr.shape[0], LANES), BF16)

    _attn_core(lam_ref[0], _bf(q_ref[...].astype(F32) * Q_SCALE), k_scr, v_scr, g_ref, o_ref, out_scale)


def _diff_attention(lam, proj, batch, seq_len, g, out_scale, proj_ctx=None, ctx_len=0, rope_tabs=None):
    tq = min(ATTN_BLOCK, seq_len)
    nq = seq_len // tq
    lk = seq_len + ctx_len
    cq, ck, cv = C_Q // LANES, C_K // LANES, C_V // LANES
    smem = pl.BlockSpec(memory_space=pltpu.SMEM)
    qspec = pl.BlockSpec((tq, LANES), lambda bi, h, i: (bi * nq + i, cq + h))
    kspec = pl.BlockSpec((seq_len, LANES), lambda bi, h, i: (bi, ck + h))
    vspec = pl.BlockSpec((seq_len, LANES), lambda bi, h, i: (bi, cv + h))
    gspec = pl.BlockSpec((1, LANES), lambda bi, h, i: (0, 0))
    common = dict(
        grid=(batch, DA_HEADS, nq),
        out_specs=pl.BlockSpec((tq, LANES), lambda bi, h, i: (bi * nq + i, h)),
        out_shape=jax.ShapeDtypeStruct((batch * seq_len, DA_W), BF16),
        scratch_shapes=[pltpu.VMEM((lk, LANES), BF16), pltpu.VMEM((lk, 2 * LANES), BF16)],
        compiler_params=_cparams(("arbitrary", "arbitrary", "arbitrary")),
    )
    if proj_ctx is None:
        return pl.pallas_call(
            functools.partial(_attn_ctx_kernel, out_scale=out_scale),
            in_specs=[smem, qspec, kspec, vspec, gspec], name="diff_attn_ctx", **common,
        )(lam, proj, proj, proj, g)
    tab = pl.BlockSpec((seq_len, LANES), lambda bi, h, i: (0, 0))
    return pl.pallas_call(
        functools.partial(_attn_latent_kernel, out_scale=out_scale),
        in_specs=[smem, qspec, kspec, vspec,
                  pl.BlockSpec((ctx_len, LANES), lambda bi, h, i: (bi, ck + h)),
                  pl.BlockSpec((ctx_len, LANES), lambda bi, h, i: (bi, cv + h)),
                  tab, tab, tab, gspec],
        name="diff_attn", **common,
    )(lam, proj, proj, proj, proj_ctx, proj_ctx, *rope_tabs, g)


def _rwkv_kernel(u_ref, up_ref, un_ref, mu_ref, w0_ref, w2_ref, a0_ref, a2_ref, g2_ref, vec_ref, seg_ref,
                 s0_ref, y_ref, aux_ref, sf_ref, s_scr, op_scr):
    d = pl.program_id(0)
    c = pl.program_id(2)
    nc = pl.num_programs(2)
    t = RW_CHUNK
    tb = u_ref.shape[0]
    nper = tb // t
    cc = c + d * (nc - 1 - 2 * c)

    @pl.when(c == 0)
    def _():
        s_scr[...] = s0_ref[0, 0]

    u = u_ref[...].astype(F32)
    prev = up_ref[HALO_ROWS - 1:HALO_ROWS, :].astype(F32) * (cc > 0).astype(F32)
    nxt = un_ref[0:1, :].astype(F32) * (cc < nc - 1).astype(F32)
    u_m1, u_p1 = _neighbour_rows(u, prev, nxt)
    u = u + mu_ref[...] * (0.5 * (u_m1 + u_p1) - u)
    r = u[:, :RW_W]
    k = u[:, RW_W:2 * RW_W]
    v = u[:, 2 * RW_W:3 * RW_W]
    o = 3 * RW_W
    wd = _bf(jnp.tanh(u[:, o:o + 2 * RW_DECAY_R]))
    o += 2 * RW_DECAY_R
    ad = _bf(u[:, o:o + 2 * RW_ICLR_R])
    o += 2 * RW_ICLR_R
    gd = u[:, o:o + RW_GATE_R]
    kk_gain, ka, rk = vec_ref[0:1, :], vec_ref[1:2, :], vec_ref[2:3, :]
    seg = seg_ref[...]
    lw_all = -math.exp(-0.5) * _sigmoid(w0_ref[0] + _dot(wd, w2_ref[0]))
    a_all = _sigmoid(a0_ref[pl.ds(d, 1), :] + _dot(ad, a2_ref[d]))
    kk_all = k * kk_gain
    kk_all = kk_all * lax.rsqrt(jnp.maximum(_seg_sum(kk_all * kk_all, seg), 1e-24))
    op_scr[0] = r
    op_scr[1] = v
    op_scr[2] = kk_all
    op_scr[3] = k * (1.0 + (a_all - 1.0) * ka)
    op_scr[4] = a_all
    op_scr[5] = lw_all

    @pl.when(d == 0)
    def _():
        a_other = _sigmoid(a0_ref[1:2, :] + _dot(ad, a2_ref[1]))
        k_bonus = k * (1.0 + (0.5 * (a_all + a_other) - 1.0) * ka)
        aux_ref[0, 0] = _bf(_seg_sum(r * k_bonus * rk, seg) * v)

    @pl.when(d == 1)
    def _():
        aux_ref[0, 0] = _bf(_dot(_bf(_sigmoid(gd)), g2_ref[...]))

    sign = 1 - 2 * d
    row = lax.broadcasted_iota(jnp.int32, (t, t), 0)
    col = lax.broadcasted_iota(jnp.int32, (t, t), 1)
    tri_b = ((row - col) * sign >= 0).astype(BF16)
    row2 = lax.broadcasted_iota(jnp.int32, (t, 2 * t), 0)
    col2 = lax.broadcasted_iota(jnp.int32, (t, 2 * t), 1)
    col2 = jnp.where(col2 >= t, col2 - t, col2)
    delta2 = (row2 - col2) * sign
    strict2 = delta2 > 0
    incl2 = delta2 >= 0
    colh = lax.broadcasted_iota(jnp.int32, (1, 2 * t), 1)
    cm0 = (colh < t).astype(F32)
    cm1 = 1.0 - cm0
    lane = lax.broadcasted_iota(jnp.int32, (1, LANES), 1)
    m0 = (lane < RW_HD).astype(F32)
    m1 = 1.0 - m0
    rb = lax.broadcasted_iota(jnp.int32, (LANES, LANES), 0) // RW_HD
    cb = lax.broadcasted_iota(jnp.int32, (LANES, LANES), 1) // RW_HD
    bd = (rb == cb).astype(F32)
    eye = (lax.broadcasted_iota(jnp.int32, (2 * t, 2 * t), 0)
           == lax.broadcasted_iota(jnp.int32, (2 * t, 2 * t), 1)).astype(F32)

    def stack2(x):
        return _bf(jnp.concatenate([x * m0, x * m1], axis=0))

    nhp = RW_W // LANES
    inst = [(j, i) for i in range(nper) for j in range(nhp)]

    def rows_of(i):
        return pl.ds(pl.multiple_of((i + d * (nper - 1 - 2 * i)) * t, t), t)

    def sl_of(j):
        return slice(j * LANES, (j + 1) * LANES)

    lws = [op_scr[5, rows_of(i), sl_of(j)] for j, i in inst]
    cums = [_cumsum_rows(tri_b, lw) for lw in lws]
    pre = []
    for (j, i), lw, cum in zip(inst, lws, cums):
        rows, sl = rows_of(i), sl_of(j)
        r = op_scr[0, rows, sl]
        v = op_scr[1, rows, sl]
        kk = op_scr[2, rows, sl]
        kd = op_scr[3, rows, sl]
        a = op_scr[4, rows, sl]
        tot = jnp.sum(lw, axis=0, keepdims=True)
        g_inv = jnp.exp(-cum)
        g_rem = jnp.exp(tot - cum)
        rt = r * jnp.exp(cum)
        kt = _bf(kk * jnp.exp(cum - lw))
        b = kk * a
        bh = b * g_inv
        kh = kd * g_inv
        ystack = _bf(jnp.concatenate([bh * m0, bh * m1, kh * m0, kh * m1], axis=0))
        pre.append(dict(v=v, rt=rt, kt=kt, ystack=ystack, g_tot=jnp.exp(tot), vst=stack2(v),
                        rhs_r=_bf(jnp.concatenate([kd * g_rem, b * g_rem], axis=0))))
    ggs = [_dot_nt(jnp.concatenate([p['kt'], _bf(p['rt'])], axis=0), p['ystack']) for p in pre]
    for p, gg in zip(pre, ggs):
        mcat = jnp.where(strict2, gg[:t, :2 * t], 0.0)
        p['ncat'] = _bf(jnp.where(strict2, gg[:t, 2 * t:], 0.0))
        p['qcat'] = _bf(jnp.where(incl2, gg[t:, :2 * t], 0.0))
        p['pcat'] = _bf(jnp.where(incl2, gg[t:, 2 * t:], 0.0))
        p['x0'] = -jnp.concatenate([mcat * cm0, mcat * cm1], axis=0)
    n_dbl = int(math.log2(t))
    ws = [eye + p['x0'] for p in pre]
    xs = [_dot(_bf(p['x0']), _bf(p['x0'])) for p in pre]
    nvs = [_dot(p['ncat'], p['vst']) for p in pre]
    for step in range(1, n_dbl):
        last = step == n_dbl - 1
        outs = [_dot(_bf(w if last else jnp.concatenate([w, x], axis=0)), _bf(x)) for w, x in zip(ws, xs)]
        ws = [w + o[:2 * t] for w, o in zip(ws, outs)]
        if not last:
            xs = [o[2 * t:] for o in outs]
    errs = [eye - _dot3(eye - p['x0'], w) for p, w in zip(pre, ws)]
    ws = [w + _dot(_bf(w), _bf(e)) for w, e in zip(ws, errs)]
    wcats = [_bf(w[:t] + w[t:]) for w in ws]
    uks = [_dot(wc, jnp.concatenate([stack2(nv), stack2(p['kt'].astype(F32))], axis=1))
           for wc, nv, p in zip(wcats, nvs, pre)]
    qus = [_dot(p['qcat'], jnp.concatenate([stack2(uk[:, :LANES]), stack2(uk[:, LANES:])], axis=1))
           for p, uk in zip(pre, uks)]
    pvs = [_dot(p['pcat'], p['vst']) for p in pre]
    zeros_t = jnp.zeros((t, LANES), F32)
    dcs = [_dot_tn(_bf(jnp.concatenate([jnp.concatenate([p['v'], -uk[:, :LANES]], axis=0),
                                        jnp.concatenate([zeros_t, uk[:, LANES:]], axis=0)], axis=1)), p['rhs_r'])
           for p, uk in zip(pre, uks)]
    for p, qu, pv, dc in zip(pre, qus, pvs, dcs):
        p['y0'] = pv - qu[:, :LANES]
        p['rp'] = _bf(p['rt'] - qu[:, LANES:])
        p['d0'] = bd * dc[:LANES]
        p['c0'] = _bf(bd * dc[LANES:])
    states = [s_scr[j] for j in range(nhp)]
    for i in range(nper):
        ps = [pre[i * nhp + j] for j in range(nhp)]
        sbs = [_bf(sbd) for sbd in states]
        ys = [_dot_nt(p['rp'], sb) for p, sb in zip(ps, sbs)]
        scs = [_dot(sb, p['c0']) for p, sb in zip(ps, sbs)]
        for j in range(nhp):
            y_ref[0, 0, rows_of(i), sl_of(j)] = _bf(ps[j]['y0'] + ys[j])
        states = [sbd * p['g_tot'] - sc + p['d0'] for sbd, p, sc in zip(states, ps, scs)]
    for j in range(nhp):
        s_scr[j] = states[j]

    @pl.when(c == nc - 1)
    def _():
        sf_ref[0, 0] = s_scr[...]


def _rwkv_scan(proj, batch, seq_len, rwp, s0):
    n = proj.shape[0]
    tb = min(RW_BLOCK, seq_len)
    nc = seq_len // tb
    nhp = RW_W // LANES
    cur, prev, nxt = _halo_specs(RW_BLK, C_RW // RW_BLK, seq_len, tb, nc, n)
    full2 = lambda d, bi, c: (0, 0)
    full3 = lambda d, bi, c: (0, 0, 0)
    dmap = lambda d, bi, c: (d, bi, c + d * (nc - 1 - 2 * c), 0)
    smap = lambda d, bi, c: (d, bi, 0, 0, 0)
    return pl.pallas_call(
        _rwkv_kernel,
        grid=(2, batch, nc),
        in_specs=[
            cur, prev, nxt,
            pl.BlockSpec((1, RW_BLK), full2),
            pl.BlockSpec((1, 1, RW_W), lambda d, bi, c: (d, 0, 0)),
            pl.BlockSpec((1, 2 * RW_DECAY_R, RW_W), lambda d, bi, c: (d, 0, 0)),
            pl.BlockSpec((2, RW_W), full2),
            pl.BlockSpec((2, 2 * RW_ICLR_R, RW_W), full3),
            pl.BlockSpec((RW_GATE_R, RW_W), full2),
            pl.BlockSpec((3, RW_W), full2),
            pl.BlockSpec((RW_W, RW_W), full2),
            pl.BlockSpec((1, 1, nhp, LANES, LANES), smap),
        ],
        out_specs=[
            pl.BlockSpec((1, 1, tb, RW_W), dmap),
            pl.BlockSpec((1, 1, tb, RW_W), dmap),
            pl.BlockSpec((1, 1, nhp, LANES, LANES), smap),
        ],
        out_shape=[
            jax.ShapeDtypeStruct((2, batch, seq_len, RW_W), BF16),
            jax.ShapeDtypeStruct((2, batch, seq_len, RW_W), BF16),
            jax.ShapeDtypeStruct((2, batch, nhp, LANES, LANES), F32),
        ],
        scratch_shapes=[pltpu.VMEM((nhp, LANES, LANES), F32), pltpu.VMEM((6, tb, RW_W), F32)],
        compiler_params=_cparams(("arbitrary", "arbitrary", "arbitrary")),
        name="rwkv_scan",
    )(proj, proj, proj, rwp['mu'], rwp['w0'], rwp['w2'], rwp['a0'], rwp['a2'], rwp['g2'], rwp['vec'],
      rwp['seg'], s0)


def _ssd_kernel(x_ref, xp_ref, xn_ref, dt_ref, cw_ref, cb_ref, dtb_ref, alog_ref, dskip_ref, s0_ref,
                y_ref, sf_ref, s_scr):
    d = pl.program_id(0)
    c = pl.program_id(2)
    nc = pl.num_programs(2)
    t = SSD_CHUNK
    gw = SSM_W // SSM_GROUPS
    hpg = SSM_HEADS // SSM_GROUPS
    nh = SSM_HEADS
    cc = c + d * (nc - 1 - 2 * c)

    @pl.when(c == 0)
    def _():
        s_scr[...] = s0_ref[0, 0]

    x = x_ref[...].astype(F32)
    prev = xp_ref[HALO_ROWS - 1:HALO_ROWS, :].astype(F32) * (cc > 0).astype(F32)
    nxt = xn_ref[0:1, :].astype(F32) * (cc < nc - 1).astype(F32)
    x_m1, x_p1 = _neighbour_rows(x, prev, nxt)
    xc = x_m1 * cw_ref[0:1, :] + x * cw_ref[1:2, :] + x_p1 * cw_ref[2:3, :] + cb_ref[...]
    xc = xc * _sigmoid(xc)
    xs = xc[:, :SSM_W]
    bm = xc[:, SSM_W:SSM_W + SSM_GROUPS * SSM_STATE]
    cm = xc[:, SSM_W + SSM_GROUPS * SSM_STATE:]
    dt2 = _softplus(dt_ref[...].astype(F32)[:, :2 * nh] + dtb_ref[...])
    a2 = dt2 * (-jnp.exp(alog_ref[...]))
    fwd = d == 0
    dt = jnp.where(fwd, dt2[:, :nh], dt2[:, nh:])
    a = jnp.where(fwd, a2[:, :nh], a2[:, nh:])

    sign = 1 - 2 * d
    row = lax.broadcasted_iota(jnp.int32, (t, t), 0)
    col = lax.broadcasted_iota(jnp.int32, (t, t), 1)
    delta = (row - col) * sign
    incl = delta >= 0
    tri_b = incl.astype(BF16)
    tri_t_b = (delta <= 0).astype(BF16)
    lane = lax.broadcasted_iota(jnp.int32, (1, LANES), 1)
    m0 = (lane < SSM_HD).astype(F32)
    m1 = 1.0 - m0

    a3 = jnp.concatenate(_split3(a), axis=1)
    cs3 = _dot(tri_b, a3)
    cs = cs3[:, :nh] + cs3[:, nh:2 * nh] + cs3[:, 2 * nh:]
    cs3_t = _dot_tn(a3, tri_t_b)
    cs_t = cs3_t[:nh] + cs3_t[nh:2 * nh] + cs3_t[2 * nh:]
    tot = jnp.sum(a, axis=0, keepdims=True)
    e_cs = jnp.exp(cs)
    e_rem = jnp.exp(tot - cs)
    e_tot = jnp.exp(tot)

    lane_g = lax.broadcasted_iota(jnp.int32, (1, gw), 1) // SSM_HD

    def expand(z, g):
        out = jnp.zeros((z.shape[0], gw), F32)
        for hh in range(hpg):
            h = g * hpg + hh
            out = jnp.where(lane_g == hh, z[:, h:h + 1], out)
        return out

    skip = fwd.astype(F32)
    for g in range(SSM_GROUPS):
        cg = cm[:, g * SSM_STATE:(g + 1) * SSM_STATE]
        bg = bm[:, g * SSM_STATE:(g + 1) * SSM_STATE]
        xsg = xs[:, g * gw:(g + 1) * gw]
        xg = xsg * expand(dt, g)
        cgb = _bf(cg)
        bgb = _bf(bg)
        gm = _dot_nt(cgb, bgb)
        st = s_scr[g]
        y_off = _dot(cgb, _bf(st)) * expand(e_cs, g)
        ys = []
        for hp in range(hpg // 2):
            scs = []
            for hh in range(2):
                h = g * hpg + 2 * hp + hh
                diff = jnp.where(incl, cs[:, h:h + 1] - cs_t[h:h + 1, :], 0.0)
                scs.append(gm * jnp.where(incl, jnp.exp(diff), 0.0))
            xp = xg[:, hp * LANES:(hp + 1) * LANES]
            xst = _bf(jnp.concatenate([xp * m0, xp * m1], axis=0))
            ys.append(_dot(_bf(jnp.concatenate(scs, axis=1)), xst))
        y_ref[0, 0, :, g * gw:(g + 1) * gw] = _bf(jnp.concatenate(ys, axis=1) + y_off
                                                  + skip * dskip_ref[:, g * gw:(g + 1) * gw] * xsg)
        s_scr[g] = st * expand(e_tot, g) + _dot_tn(bgb, _bf(xg * expand(e_rem, g)))

    @pl.when(c == nc - 1)
    def _():
        sf_ref[0, 0] = s_scr[...]


def _ssd_scan(proj, batch, seq_len, ssp, s0):
    n = proj.shape[0]
    t = min(SSD_CHUNK, seq_len)
    nc = seq_len // t
    gw = SSM_W // SSM_GROUPS
    cur, prev, nxt = _halo_specs(SSM_XBC, C_XBC // SSM_XBC, seq_len, t, nc, n)
    dtspec, _, _ = _halo_specs(LANES, C_DT // LANES, seq_len, t, nc, n)
    full2 = lambda d, bi, c: (0, 0)
    dmap = lambda d, bi, c: (d, bi, c + d * (nc - 1 - 2 * c), 0)
    smap = lambda d, bi, c: (d, bi, 0, 0, 0)
    return pl.pallas_call(
        _ssd_kernel,
        grid=(2, batch, nc),
        in_specs=[
            cur, prev, nxt, dtspec,
            pl.BlockSpec((3, SSM_XBC), full2),
            pl.BlockSpec((1, SSM_XBC), full2),
            pl.BlockSpec((1, 2 * SSM_HEADS), full2),
            pl.BlockSpec((1, 2 * SSM_HEADS), full2),
            pl.BlockSpec((1, SSM_W), full2),
            pl.BlockSpec((1, 1, SSM_GROUPS, SSM_STATE, gw), smap),
        ],
        out_specs=[
            pl.BlockSpec((1, 1, t, SSM_W), dmap),
            pl.BlockSpec((1, 1, SSM_GROUPS, SSM_STATE, gw), smap),
        ],
        out_shape=[
            jax.ShapeDtypeStruct((2, batch, seq_len, SSM_W), BF16),
            jax.ShapeDtypeStruct((2, batch, SSM_GROUPS, SSM_STATE, gw), F32),
        ],
        scratch_shapes=[pltpu.VMEM((SSM_GROUPS, SSM_STATE, gw), F32)],
        compiler_params=_cparams(("arbitrary", "arbitrary", "arbitrary")),
        name="ssd_scan",
    )(proj, proj, proj, proj, ssp['conv_w'], ssp['conv_b'], ssp['dt_bias'], ssp['a_log'], ssp['dskip'], s0)


def _merge_kernel(ya_ref, yr_ref, aux_ref, ys_ref, z_ref, ga_ref, gr_ref, gs_ref, x_ref, g1_ref,
                  seg_ref, rg_ref, rb_ref, sg_ref, pa_ref, pr_ref, ps_ref, wo_ref, lg_ref, lb_ref, o_ref):
    seg = seg_ref[...]
    yr = yr_ref[0].astype(F32) + yr_ref[1].astype(F32)
    mu = _seg_sum(yr, seg) * (1.0 / RW_HD)
    dv = yr - mu
    var = _seg_sum(dv * dv, seg) * (1.0 / RW_HD)
    yr = (dv * lax.rsqrt(var + RW_GN_EPS) * rg_ref[...] + rb_ref[...] + aux_ref[0].astype(F32)) * aux_ref[1].astype(F32)
    z = z_ref[...].astype(F32)
    ys = (ys_ref[0].astype(F32) + ys_ref[1].astype(F32)) * (z * _sigmoid(z))
    gw = SSM_W // SSM_GROUPS
    parts = []
    for g in range(SSM_GROUPS):
        yg = ys[:, g * gw:(g + 1) * gw]
        parts.append(yg * lax.rsqrt(jnp.mean(yg * yg, -1, keepdims=True) + NORM_EPS))
    ys = jnp.concatenate(parts, axis=1) * sg_ref[...]
    m = (_sigmoid(ga_ref[...].astype(F32)) * _dot(ya_ref[...], pa_ref[...])
         + _sigmoid(gr_ref[...].astype(F32)) * _dot(_bf(yr), pr_ref[...])
         + _sigmoid(gs_ref[...].astype(F32)) * _dot(_bf(ys), ps_ref[...]))
    out = _dot(_bf(m), wo_ref[...])
    o_ref[...] = _layernorm_rows(ALPHA * x_ref[...] + g1_ref[0] * out, lg_ref[...], lb_ref[...])


def _merge(ya, yr2, aux, ys2, proj, x2d, g1, lp, seq_len):
    n = x2d.shape[0]
    tm = min(256, seq_len)
    per_b = seq_len // tm
    row = lambda i: (i, 0)
    row3 = lambda i: (0, i, 0)
    full = lambda i: (0, 0)
    gate = lambda col: pl.BlockSpec((tm, D_MODEL), lambda i: (i, col // D_MODEL))
    vec512 = pl.BlockSpec((1, RW_W), full)
    vec = pl.BlockSpec((1, D_MODEL), full)
    return pl.pallas_call(
        _merge_kernel,
        grid=(n // tm,),
        in_specs=[
            pl.BlockSpec((tm, DA_W), row),
            pl.BlockSpec((2, tm, RW_W), row3), pl.BlockSpec((2, tm, RW_W), row3),
            pl.BlockSpec((2, tm, SSM_W), row3),
            pl.BlockSpec((tm, SSM_W), lambda i: (i, C_Z // SSM_W)),
            gate(C_GA), gate(C_GR), gate(C_GS),
            pl.BlockSpec((tm, D_MODEL), row),
            pl.BlockSpec((1, 1, D_MODEL), lambda i: (i // per_b, 0, 0)),
            pl.BlockSpec((RW_W, RW_W), full), vec512, vec512, vec512,
            pl.BlockSpec((DA_W, D_MODEL), full), pl.BlockSpec((RW_W, D_MODEL), full),
            pl.BlockSpec((SSM_W, D_MODEL), full), pl.BlockSpec((D_MODEL, D_MODEL), full),
            vec, vec,
        ],
        out_specs=pl.BlockSpec((tm, D_MODEL), row),
        out_shape=jax.ShapeDtypeStruct((n, D_MODEL), F32),
        compiler_params=_cparams(("arbitrary",)),
        name="merge_ln",
    )(ya, yr2, aux, ys2, proj, proj, proj, proj, x2d, g1, lp['rw']['seg'], lp['rw_ln_g'], lp['rw_ln_b'],
      lp['ssm_norm_g'], lp['p_attn'], lp['p_rwkv'], lp['p_ssm'], lp['w_out'], lp['ln1_g'], lp['ln1_b'])


def _ffn_kernel(x_ref, sc_ref, sh_ref, g2_ref, w1_ref, w3_ref, w2_ref, lg_ref, lb_ref, o_ref, h_scr, acc_scr):
    j = pl.program_id(1)

    @pl.when(j == 0)
    def _():
        h_scr[...] = (x_ref[...] * (1.0 + sc_ref[0]) + sh_ref[0]).astype(BF16)
        acc_scr[...] = jnp.zeros_like(acc_scr)

    h = h_scr[...]
    u = jax.nn.silu(_dot(h, w1_ref[...])) * _dot(h, w3_ref[...])
    acc_scr[...] += _dot(u.astype(BF16), w2_ref[...])

    @pl.when(j == pl.num_programs(1) - 1)
    def _():
        o_ref[...] = _layernorm_rows(ALPHA * x_ref[...] + g2_ref[0] * acc_scr[...], lg_ref[...], lb_ref[...])


def _ffn(x2d, sc, sh, g2, w1, w3, w2, lg, lb, seq_len):
    n = x2d.shape[0]
    tm = min(512, seq_len)
    th = FFN_HIDDEN // 2
    per_b = seq_len // tm
    mod = pl.BlockSpec((1, 1, D_MODEL), lambda i, j: (i // per_b, 0, 0))
    return pl.pallas_call(
        _ffn_kernel,
        grid=(n // tm, FFN_HIDDEN // th),
        in_specs=[
            pl.BlockSpec((tm, D_MODEL), lambda i, j: (i, 0)),
            mod, mod, mod,
            pl.BlockSpec((D_MODEL, th), lambda i, j: (0, j)),
            pl.BlockSpec((D_MODEL, th), lambda i, j: (0, j)),
            pl.BlockSpec((th, D_MODEL), lambda i, j: (j, 0)),
            pl.BlockSpec((1, D_MODEL), lambda i, j: (0, 0)),
            pl.BlockSpec((1, D_MODEL), lambda i, j: (0, 0)),
        ],
        out_specs=pl.BlockSpec((tm, D_MODEL), lambda i, j: (i, 0)),
        out_shape=jax.ShapeDtypeStruct((n, D_MODEL), F32),
        scratch_shapes=[pltpu.VMEM((tm, D_MODEL), BF16), pltpu.VMEM((tm, D_MODEL), F32)],
        compiler_params=_cparams(("arbitrary", "arbitrary")),
        name="ffn_ln",
    )(x2d, sc, sh, g2, w1, w3, w2, lg, lb)


def _rope_tables(n_tok):
    rows = n_tok // GRID_W
    row = jnp.repeat(jnp.arange(rows), GRID_W).astype(F32)
    col = jnp.tile(jnp.arange(GRID_W), rows).astype(F32)
    nf = DA_HD // 4
    inv = ROPE_BASE ** (-jnp.arange(nf, dtype=F32) / nf)
    ang = jnp.concatenate([row[:, None] * inv] * 2 + [col[:, None] * inv] * 2, axis=-1)
    cos = jnp.cos(ang)
    sin = jnp.sin(ang)
    first = (jnp.arange(DA_HD) % (2 * nf)) < nf
    s_dn = jnp.where(first, -sin, 0.0)
    s_up = jnp.where(first, 0.0, sin)
    reps = LANES // DA_HD
    return jnp.tile(cos, (1, reps)), jnp.tile(s_dn, (1, reps)), jnp.tile(s_up, (1, reps))


def _permute_w_in(w_in):
    parts, acc = [], 0
    for s in W_IN_SPLIT:
        parts.append(w_in[:, acc:acc + s])
        acc += s
    q, k, v, rw, z, xbc, dt, ga, gr, gs = parts
    pad = jnp.zeros((w_in.shape[0], RW_BLK - RW_COLS - dt.shape[1]), w_in.dtype)
    return jnp.concatenate([rw, dt, pad, ga, gr, gs, q, k, v, z, xbc], axis=1).astype(BF16)


def _pad_lora(w, rank):
    z = jnp.zeros_like(w[0])
    return jnp.stack([jnp.concatenate([w[0], z], axis=0), jnp.concatenate([z, w[1]], axis=0)]).astype(BF16)


def _mixer(x, mod, lp, layer, side, want_out, rope_tabs):
    b, l, _ = x.shape
    sh1, sc1, g1 = mod
    n = b * l
    x2d = x.reshape(n, D_MODEL)
    proj = _inproj(x2d, sc1, sh1, lp['w_in_p'], l)
    if side is None:
        proj_ctx, ctx_len = None, 0
        s0_r = jnp.zeros((2, b, RW_W // LANES, LANES, LANES), F32)
        s0_s = jnp.zeros((2, b, SSM_GROUPS, SSM_STATE, SSM_W // SSM_GROUPS), F32)
    else:
        (proj_ctx, ctx_len), s0_r, s0_s = side
    y_r2, aux, sf_r = _rwkv_scan(proj, b, l, lp['rw'], s0_r)
    y_s2, sf_s = _ssd_scan(proj, b, l, lp['ssm'], s0_s)
    new_side = ((proj, l), sf_r, sf_s)
    if not want_out:
        return None, new_side
    lam_init = 0.8 - 0.6 * math.exp(-0.3 * layer)
    lam = (jnp.exp(jnp.sum(lp['da_lq1'] * lp['da_lk1'])) - jnp.exp(jnp.sum(lp['da_lq2'] * lp['da_lk2']))
           + lam_init).reshape(1)
    y_a = _diff_attention(lam, proj, b, l, lp['da_norm_g'].reshape(1, 2 * DA_HD), 1.0 - lam_init,
                          proj_ctx, ctx_len, rope_tabs)
    x1 = _merge(y_a, y_r2.reshape(2, n, RW_W), aux.reshape(2, n, RW_W),
                y_s2.reshape(2, n, SSM_W), proj, x2d, g1, lp, l)
    return x1, new_side


def kernel(x, c, ctx, c_ctx, ada_w, ada_b, w_in, da_lq1, da_lk1, da_lq2, da_lk2, da_norm_g,
           rw_mu, rw_w0, rw_w2, rw_a0, rw_a2, rw_g2, rw_kk, rw_ka, rw_rk, rw_ln_g, rw_ln_b,
           ssm_conv_w, ssm_conv_b, ssm_dt_bias, ssm_a_log, ssm_d, ssm_norm_g,
           p_attn, p_rwkv, p_ssm, w_out, ln1_g, ln1_b, ffn_w1, ffn_w3, ffn_w2, ln2_g, ln2_b):
    b, l, _ = x.shape
    lc = ctx.shape[1]
    rope_tabs = _rope_tables(l)
    seg = (jnp.arange(RW_W)[:, None] // RW_HD == jnp.arange(RW_W)[None, :] // RW_HD).astype(BF16)
    xc = ctx
    for i in range(DEPTH):
        rw = {
            'mu': jnp.pad(rw_mu[i], (0, RW_BLK - RW_COLS)).reshape(1, RW_BLK),
            'w0': rw_w0[i].reshape(2, 1, RW_W),
            'w2': _pad_lora(rw_w2[i], RW_DECAY_R),
            'a0': rw_a0[i],
            'a2': _pad_lora(rw_a2[i], RW_ICLR_R),
            'g2': rw_g2[i].astype(BF16),
            'vec': jnp.stack([rw_kk[i], rw_ka[i], rw_rk[i].reshape(RW_W)]),
            'seg': seg,
        }
        ssm = {
            'conv_w': ssm_conv_w[i], 'conv_b': ssm_conv_b[i].reshape(1, SSM_XBC),
            'dt_bias': ssm_dt_bias[i].reshape(1, 2 * SSM_HEADS), 'a_log': ssm_a_log[i].reshape(1, 2 * SSM_HEADS),
            'dskip': jnp.repeat(ssm_d[i], SSM_HD).reshape(1, SSM_W),
        }
        lp = {
            'w_in_p': _permute_w_in(w_in[i]), 'rw': rw, 'ssm': ssm,
            'da_lq1': da_lq1[i], 'da_lk1': da_lk1[i], 'da_lq2': da_lq2[i], 'da_lk2': da_lk2[i],
            'da_norm_g': da_norm_g[i],
            'rw_ln_g': rw_ln_g[i].reshape(1, RW_W), 'rw_ln_b': rw_ln_b[i].reshape(1, RW_W),
            'ssm_norm_g': ssm_norm_g[i].reshape(1, SSM_W),
            'p_attn': p_attn[i].astype(BF16), 'p_rwkv': p_rwkv[i].astype(BF16),
            'p_ssm': p_ssm[i].astype(BF16), 'w_out': w_out[i].astype(BF16),
            'ln1_g': ln1_g[i].reshape(1, D_MODEL), 'ln1_b': ln1_b[i].reshape(1, D_MODEL),
        }
        w1, w3, w2 = ffn_w1[i].astype(BF16), ffn_w3[i].astype(BF16), ffn_w2[i].astype(BF16)
        l2g, l2b = ln2_g[i].reshape(1, D_MODEL), ln2_b[i].reshape(1, D_MODEL)
        last = i == DEPTH - 1
        mod_x = (jax.nn.silu(c) @ ada_w[i] + ada_b[i])[:, None, :]
        mod_c = jnp.broadcast_to((jax.nn.silu(c_ctx) @ ada_w[i] + ada_b[i])[None, None, :], (b, 1, 6 * D_MODEL))
        sh1, sc1, g1, sh2, sc2, g2 = jnp.split(mod_x, 6, axis=-1)
        csh1, csc1, cg1, csh2, csc2, cg2 = jnp.split(mod_c, 6, axis=-1)
        xc1, side = _mixer(xc, (csh1, csc1, cg1), lp, i, None, not last, None)
        x1, _ = _mixer(x, (sh1, sc1, g1), lp, i, side, True, rope_tabs)
        x = _ffn(x1, sc2, sh2, g2, w1, w3, w2, l2g, l2b, l).reshape(b, l, D_MODEL)
        if not last:
            xc = _ffn(xc1, csc2, csh2, cg2, w1, w3, w2, l2g, l2b, lc).reshape(b, lc, D_MODEL)
    return x
```

```python
import functools
import math

import jax
import jax.numpy as jnp
from jax import lax
from jax.experimental import pallas as pl
from jax.experimental.pallas import tpu as pltpu

F32 = jnp.float32
BF16 = jnp.bfloat16

D_MODEL = 1024
DEPTH = 4
GRID_W = 64
DA_HEADS = 4
DA_HD = 64
DA_W = DA_HEADS * 2 * DA_HD
ROPE_BASE = 10000.0
RW_HEADS = 8
RW_HD = 64
RW_W = RW_HEADS * RW_HD
RW_DECAY_R = 64
RW_ICLR_R = 64
RW_GATE_R = 128
RW_GN_EPS = 64e-5
RW_COLS = 3 * RW_W + 2 * RW_DECAY_R + 2 * RW_ICLR_R + RW_GATE_R
SSM_HEADS = 8
SSM_HD = 64
SSM_W = SSM_HEADS * SSM_HD
SSM_GROUPS = 2
SSM_STATE = 128
SSM_XBC = SSM_W + 2 * SSM_GROUPS * SSM_STATE
FFN_HIDDEN = (8 * D_MODEL + 3 * 256 - 1) // (3 * 256) * 256
NORM_EPS = 1e-5
W_IN_SPLIT = (DA_W, DA_W, DA_W, RW_COLS, SSM_W, SSM_XBC, 2 * SSM_HEADS, D_MODEL, D_MODEL, D_MODEL)
ALPHA = (2.0 * DEPTH) ** 0.25

LANES = 128
SUBLANES = 8
HALO_ROWS = 16
VMEM_LIMIT_BYTES = 56 * 1024 * 1024

PROJ_COLS = 8192
RW_BLK = 2048
C_RW = 0
C_DT = RW_COLS
C_GA, C_GR, C_GS = RW_BLK, RW_BLK + D_MODEL, RW_BLK + 2 * D_MODEL
C_Q = RW_BLK + 3 * D_MODEL
C_K = C_Q + DA_W
C_V = C_K + DA_W
C_Z = C_V + DA_W
C_XBC = C_Z + SSM_W

Q_SCALE = math.log2(math.e) * DA_HD ** -0.5
ATTN_SUBTILE = 128
ATTN_BLOCK = 512
RW_CHUNK = 64
RW_BLOCK = 256
SSD_CHUNK = 128


def _cparams(sem):
    return pltpu.CompilerParams(dimension_semantics=sem, vmem_limit_bytes=VMEM_LIMIT_BYTES)


def _dot(a, b):
    return jnp.dot(a, b, preferred_element_type=F32)


def _dot_nt(a, b):
    return lax.dot_general(a, b, (((1,), (1,)), ((), ())), preferred_element_type=F32)


def _dot_tn(a, b):
    return lax.dot_general(a, b, (((0,), (0,)), ((), ())), preferred_element_type=F32)


def _bf(x):
    return x.astype(BF16)


def _split3(x):
    hi = _bf(x)
    r1 = x - hi.astype(F32)
    mid = _bf(r1)
    lo = _bf(r1 - mid.astype(F32))
    return hi, mid, lo


def _dot3(a, b):
    ah, am, _ = _split3(a)
    bh, bm, _ = _split3(b)
    return _dot(ah, bh) + (_dot(ah, bm) + _dot(am, bh))


def _cumsum_rows(tri_b, x):
    n = x.shape[1]
    hi, mid, lo = _split3(x)
    o = _dot(tri_b, jnp.concatenate([hi, mid, lo], axis=1))
    return o[:, :n] + o[:, n:2 * n] + o[:, 2 * n:]


def _seg_sum(x, seg_b):
    m = x.shape[0]
    hi = _bf(x)
    lo = _bf(x - hi.astype(F32))
    o = _dot(jnp.concatenate([hi, lo], axis=0), seg_b)
    return o[:m] + o[m:]


def _sigmoid(x):
    return 1.0 / (1.0 + jnp.exp(-x))


def _softplus(x):
    return jnp.maximum(x, 0.0) + jnp.log(1.0 + jnp.exp(-jnp.abs(x)))


def _layernorm_rows(t, g, b):
    mu = jnp.mean(t, -1, keepdims=True)
    d = t - mu
    var = jnp.mean(d * d, -1, keepdims=True)
    return d * lax.rsqrt(var + NORM_EPS) * g + b


def _neighbour_rows(x, prev_row, next_row):
    t = x.shape[0]
    rowi = lax.broadcasted_iota(jnp.int32, (t, 1), 0)
    x_m1 = jnp.where(rowi == 0, prev_row, pltpu.roll(x, 1, 0))
    x_p1 = jnp.where(rowi == t - 1, next_row, pltpu.roll(x, t - 1, 0))
    return x_m1, x_p1


def _halo_specs(width, col_blk, seq_len, tb, nc, n_rows):
    per8 = tb // HALO_ROWS
    last8 = n_rows // HALO_ROWS - 1

    def blk(d, bi, c):
        return bi * nc + c + d * (nc - 1 - 2 * c)

    prev = pl.BlockSpec((HALO_ROWS, width), lambda d, bi, c: (jnp.maximum(blk(d, bi, c) * per8 - 1, 0), col_blk))
    nxt = pl.BlockSpec((HALO_ROWS, width),
                       lambda d, bi, c: (jnp.minimum((blk(d, bi, c) + 1) * per8, last8), col_blk))
    cur = pl.BlockSpec((tb, width), lambda d, bi, c: (blk(d, bi, c), col_blk))
    return cur, prev, nxt


def _inproj_kernel(x_ref, sc_ref, sh_ref, w_ref, o_ref, h_scr):
    @pl.when(pl.program_id(1) == 0)
    def _():
        h = x_ref[...] * (1.0 + sc_ref[0]) + sh_ref[0]
        h_scr[...] = h.astype(BF16)

    o_ref[...] = _bf(_dot(h_scr[...], w_ref[...]))


def _inproj(x2d, sc, sh, w, seq_len):
    n = x2d.shape[0]
    tm = min(1024, seq_len)
    tn = 1024
    per_b = seq_len // tm
    return pl.pallas_call(
        _inproj_kernel,
        grid=(n // tm, PROJ_COLS // tn),
        in_specs=[
            pl.BlockSpec((tm, D_MODEL), lambda i, j: (i, 0)),
            pl.BlockSpec((1, 1, D_MODEL), lambda i, j: (i // per_b, 0, 0)),
            pl.BlockSpec((1, 1, D_MODEL), lambda i, j: (i // per_b, 0, 0)),
            pl.BlockSpec((D_MODEL, tn), lambda i, j: (0, j)),
        ],
        out_specs=pl.BlockSpec((tm, tn), lambda i, j: (i, j)),
        out_shape=jax.ShapeDtypeStruct((n, PROJ_COLS), BF16),
        scratch_shapes=[pltpu.VMEM((tm, D_MODEL), BF16)],
        compiler_params=_cparams(("arbitrary", "arbitrary")),
        name="inproj",
    )(x2d, sc, sh, w)


def _rope_lanes(x, cos, s_dn, s_up):
    nf = DA_HD // 4
    return x * cos + pltpu.roll(x, LANES - nf, 1) * s_dn + pltpu.roll(x, nf, 1) * s_up


def _attn_core(lam, q, k_scr, v_scr, g_ref, o_ref, out_scale):
    tq = ATTN_SUBTILE
    nsub = q.shape[0] // tq
    lk = k_scr.shape[0]
    lane = lax.broadcasted_iota(jnp.int32, (1, LANES), 1)
    mid = (lk // 2) // LANES * LANES
    cuts = (0, mid, lk) if mid else (0, lk)
    spans = list(zip(cuts[:-1], cuts[1:]))
    scores = []
    for t in range(nsub):
        qt = q[t * tq:(t + 1) * tq]
        zero = jnp.zeros_like(qt)
        qq = jnp.concatenate([jnp.where(lane < DA_HD, qt, zero), jnp.where(lane >= DA_HD, qt, zero)], axis=0)
        scores.append([_dot_nt(qq, k_scr[lo:hi, :]) for lo, hi in spans])
    for t, ss in enumerate(scores):
        m = functools.reduce(jnp.maximum, [jnp.max(s, -1, keepdims=True) for s in ss])
        ov = functools.reduce(jnp.add, [_dot(_bf(jnp.exp2(s - m)), v_scr[lo:hi, :]) for s, (lo, hi) in zip(ss, spans)])
        on = ov[:, :LANES] / ov[:, LANES:]
        o = on[:tq] - lam * on[tq:]
        ms = jnp.mean(o * o, -1, keepdims=True)
        o_ref[t * tq:(t + 1) * tq, :] = _bf(o * lax.rsqrt(ms + NORM_EPS) * (g_ref[...] * out_scale))


def _attn_latent_kernel(lam_ref, q_ref, kl_ref, vl_ref, kc_ref, vc_ref, cos_ref, sdn_ref, sup_ref, g_ref,
                        o_ref, k_scr, v_scr, *, out_scale):
    i = pl.program_id(2)
    tq = q_ref.shape[0]
    lc = kc_ref.shape[0]

    @pl.when(i == 0)
    def _():
        k_scr[:lc, :] = kc_ref[...]
        k_scr[lc:, :] = _bf(_rope_lanes(kl_ref[...].astype(F32), cos_ref[...], sdn_ref[...], sup_ref[...]))
        v_scr[:lc, :LANES] = vc_ref[...]
        v_scr[lc:, :LANES] = vl_ref[...]
        v_scr[:, LANES:] = jnp.ones((v_scr.shape[0], LANES), BF16)

    rows = pl.ds(pl.multiple_of(i * tq, tq), tq)
    q = _rope_lanes(q_ref[...].astype(F32), cos_ref[rows, :], sdn_ref[rows, :], sup_ref[rows, :])
    _attn_core(lam_ref[0], _bf(q * Q_SCALE), k_scr, v_scr, g_ref, o_ref, out_scale)


def _attn_ctx_kernel(lam_ref, q_ref, kl_ref, vl_ref, g_ref, o_ref, k_scr, v_scr, *, out_scale):
    @pl.when(pl.program_id(2) == 0)
    def _():
        k_scr[...] = kl_ref[...]
        v_scr[:, :LANES] = vl_ref[...]
        v_scr[:, LANES:] = jnp.ones((v_scr.shape[0], LANES), BF16)

    _attn_core(lam_ref[0], _bf(q_ref[...].astype(F32) * Q_SCALE), k_scr, v_scr, g_ref, o_ref, out_scale)


def _diff_attention(lam, proj, batch, seq_len, g, out_scale, proj_ctx=None, ctx_len=0, rope_tabs=None):
    tq = min(ATTN_BLOCK, seq_len)
    nq = seq_len // tq
    lk = seq_len + ctx_len
    cq, ck, cv = C_Q // LANES, C_K // LANES, C_V // LANES
    smem = pl.BlockSpec(memory_space=pltpu.SMEM)
    qspec = pl.BlockSpec((tq, LANES), lambda bi, h, i: (bi * nq + i, cq + h))
    kspec = pl.BlockSpec((seq_len, LANES), lambda bi, h, i: (bi, ck + h))
    vspec = pl.BlockSpec((seq_len, LANES), lambda bi, h, i: (bi, cv + h))
    gspec = pl.BlockSpec((1, LANES), lambda bi, h, i: (0, 0))
    common = dict(
        grid=(batch, DA_HEADS, nq),
        out_specs=pl.BlockSpec((tq, LANES), lambda bi, h, i: (bi * nq + i, h)),
        out_shape=jax.ShapeDtypeStruct((batch * seq_len, DA_W), BF16),
        scratch_shapes=[pltpu.VMEM((lk, LANES), BF16), pltpu.VMEM((lk, 2 * LANES), BF16)],
        compiler_params=_cparams(("arbitrary", "arbitrary", "arbitrary")),
    )
    if proj_ctx is None:
        return pl.pallas_call(
            functools.partial(_attn_ctx_kernel, out_scale=out_scale),
            in_specs=[smem, qspec, kspec, vspec, gspec], name="diff_attn_ctx", **common,
        )(lam, proj, proj, proj, g)
    tab = pl.BlockSpec((seq_len, LANES), lambda bi, h, i: (0, 0))
    return pl.pallas_call(
        functools.partial(_attn_latent_kernel, out_scale=out_scale),
        in_specs=[smem, qspec, kspec, vspec,
                  pl.BlockSpec((ctx_len, LANES), lambda bi, h, i: (bi, ck + h)),
                  pl.BlockSpec((ctx_len, LANES), lambda bi, h, i: (bi, cv + h)),
                  tab, tab, tab, gspec],
        name="diff_attn", **common,
    )(lam, proj, proj, proj, proj_ctx, proj_ctx, *rope_tabs, g)


def _rwkv_kernel(u_ref, up_ref, un_ref, mu_ref, w0_ref, w2_ref, a0_ref, a2_ref, g2_ref, vec_ref, seg_ref,
                 s0_ref, y_ref, aux_ref, sf_ref, s_scr):
    d = pl.program_id(0)
    c = pl.program_id(2)
    nc = pl.num_programs(2)

    @pl.when(c == 0)
    def _():
        s_scr[...] = s0_ref[0, 0]

    refs = (u_ref, up_ref, un_ref, mu_ref, w0_ref, w2_ref, a0_ref, a2_ref, g2_ref, vec_ref, seg_ref,
            y_ref, aux_ref, s_scr)

    @pl.when(d == 0)
    def _():
        _rwkv_step(True, c, nc, *refs)

    @pl.when(d == 1)
    def _():
        _rwkv_step(False, nc - 1 - c, nc, *refs)

    @pl.when(c == nc - 1)
    def _():
        sf_ref[0, 0] = s_scr[...]


def _rwkv_step(fwd, cc, nc, u_ref, up_ref, un_ref, mu_ref, w0_ref, w2_ref, a0_ref, a2_ref, g2_ref, vec_ref,
               seg_ref, y_ref, aux_ref, s_scr):
    t = RW_CHUNK
    tb = u_ref.shape[0]
    nper = tb // t
    di = 0 if fwd else 1

    u = u_ref[:, :RW_COLS].astype(F32)
    prev = up_ref[HALO_ROWS - 1:HALO_ROWS, :RW_COLS].astype(F32) * (cc > 0).astype(F32)
    nxt = un_ref[0:1, :RW_COLS].astype(F32) * (cc < nc - 1).astype(F32)
    u_m1, u_p1 = _neighbour_rows(u, prev, nxt)
    u = u + mu_ref[:, :RW_COLS] * (0.5 * (u_m1 + u_p1) - u)
    r = r_all = u[:, :RW_W]
    k = u[:, RW_W:2 * RW_W]
    v = v_all = u[:, 2 * RW_W:3 * RW_W]
    o = 3 * RW_W
    wd = _bf(jnp.tanh(u[:, o:o + 2 * RW_DECAY_R]))
    o += 2 * RW_DECAY_R
    ad = _bf(u[:, o:o + 2 * RW_ICLR_R])
    o += 2 * RW_ICLR_R
    gd = u[:, o:o + RW_GATE_R]
    kk_gain, ka, rk = vec_ref[0:1, :], vec_ref[1:2, :], vec_ref[2:3, :]
    seg = seg_ref[...]
    lw_all = -math.exp(-0.5) * _sigmoid(w0_ref[0] + _dot(wd, w2_ref[0]))
    a_all = _sigmoid(a0_ref[di:di + 1, :] + _dot(ad, a2_ref[di]))
    kk_all = k * kk_gain
    kk_all = kk_all * lax.rsqrt(jnp.maximum(_seg_sum(kk_all * kk_all, seg), 1e-24))
    kd_all = k * (1.0 + (a_all - 1.0) * ka)
    if fwd:
        a_other = _sigmoid(a0_ref[1:2, :] + _dot(ad, a2_ref[1]))
        k_bonus = k * (1.0 + (0.5 * (a_all + a_other) - 1.0) * ka)
        aux_ref[0, 0] = _bf(_seg_sum(r * k_bonus * rk, seg) * v)
    else:
        aux_ref[0, 0] = _bf(_dot(_bf(_sigmoid(gd)), g2_ref[...]))

    sign = 1 if fwd else -1
    row = lax.broadcasted_iota(jnp.int32, (t, t), 0)
    col = lax.broadcasted_iota(jnp.int32, (t, t), 1)
    tri_b = ((row - col) * sign >= 0).astype(BF16)
    row2 = lax.broadcasted_iota(jnp.int32, (t, 2 * t), 0)
    col2 = lax.broadcasted_iota(jnp.int32, (t, 2 * t), 1)
    col2 = jnp.where(col2 >= t, col2 - t, col2)
    delta2 = (row2 - col2) * sign
    strict2 = delta2 > 0
    incl2 = delta2 >= 0
    colh = lax.broadcasted_iota(jnp.int32, (1, 2 * t), 1)
    cm0 = (colh < t).astype(F32)
    cm1 = 1.0 - cm0
    lane = lax.broadcasted_iota(jnp.int32, (1, LANES), 1)
    m0 = (lane < RW_HD).astype(F32)
    m1 = 1.0 - m0
    rb = lax.broadcasted_iota(jnp.int32, (LANES, LANES), 0) // RW_HD
    cb = lax.broadcasted_iota(jnp.int32, (LANES, LANES), 1) // RW_HD
    bd = (rb == cb).astype(F32)
    eye = (lax.broadcasted_iota(jnp.int32, (2 * t, 2 * t), 0)
           == lax.broadcasted_iota(jnp.int32, (2 * t, 2 * t), 1)).astype(F32)

    def stack2(x):
        return _bf(jnp.concatenate([x * m0, x * m1], axis=0))

    nhp = RW_W // LANES
    inst = [(j, i) for i in range(nper) for j in range(nhp)]

    def rows_of(i):
        ip = i if fwd else nper - 1 - i
        return slice(ip * t, (ip + 1) * t)

    def sl_of(j):
        return slice(j * LANES, (j + 1) * LANES)

    lws = [lw_all[rows_of(i), sl_of(j)] for j, i in inst]
    cums = [_cumsum_rows(tri_b, lw) for lw in lws]
    pre = []
    for (j, i), lw, cum in zip(inst, lws, cums):
        rows, sl = rows_of(i), sl_of(j)
        r = r_all[rows, sl]
        v = v_all[rows, sl]
        kk = kk_all[rows, sl]
        kd = kd_all[rows, sl]
        a = a_all[rows, sl]
        tot = jnp.sum(lw, axis=0, keepdims=True)
        g_inv = jnp.exp(-cum)
        g_rem = jnp.exp(tot - cum)
        rt = r * jnp.exp(cum)
        kt = _bf(kk * jnp.exp(cum - lw))
        b = kk * a
        bh = b * g_inv
        kh = kd * g_inv
        ystack = _bf(jnp.concatenate([bh * m0, bh * m1, kh * m0, kh * m1], axis=0))
        pre.append(dict(v=v, rt=rt, kt=kt, ystack=ystack, g_tot=jnp.exp(tot), vst=stack2(v),
                        rhs_r=_bf(jnp.concatenate([kd * g_rem, b * g_rem], axis=0))))
    ggs = [_dot_nt(jnp.concatenate([p['kt'], _bf(p['rt'])], axis=0), p['ystack']) for p in pre]
    for p, gg in zip(pre, ggs):
        mcat = jnp.where(strict2, gg[:t, :2 * t], 0.0)
        p['ncat'] = _bf(jnp.where(strict2, gg[:t, 2 * t:], 0.0))
        p['qcat'] = _bf(jnp.where(incl2, gg[t:, :2 * t], 0.0))
        p['pcat'] = _bf(jnp.where(incl2, gg[t:, 2 * t:], 0.0))
        p['x0'] = -jnp.concatenate([mcat * cm0, mcat * cm1], axis=0)
    n_dbl = int(math.log2(t))
    ws = [eye + p['x0'] for p in pre]
    xs = [_dot(_bf(p['x0']), _bf(p['x0'])) for p in pre]
    nvs = [_dot(p['ncat'], p['vst']) for p in pre]
    for step in range(1, n_dbl):
        last = step == n_dbl - 1
        outs = [_dot(_bf(w if last else jnp.concatenate([w, x], axis=0)), _bf(x)) for w, x in zip(ws, xs)]
        ws = [w + o[:2 * t] for w, o in zip(ws, outs)]
        if not last:
            xs = [o[2 * t:] for o in outs]
    errs = [eye - _dot3(eye - p['x0'], w) for p, w in zip(pre, ws)]
    ws = [w + _dot(_bf(w), _bf(e)) for w, e in zip(ws, errs)]
    wcats = [_bf(w[:t] + w[t:]) for w in ws]
    uks = [_dot(wc, jnp.concatenate([stack2(nv), stack2(p['kt'].astype(F32))], axis=1))
           for wc, nv, p in zip(wcats, nvs, pre)]
    qus = [_dot(p['qcat'], jnp.concatenate([stack2(uk[:, :LANES]), stack2(uk[:, LANES:])], axis=1))
           for p, uk in zip(pre, uks)]
    pvs = [_dot(p['pcat'], p['vst']) for p in pre]
    zeros_t = jnp.zeros((t, LANES), F32)
    dcs = [_dot_tn(_bf(jnp.concatenate([jnp.concatenate([p['v'], -uk[:, :LANES]], axis=0),
                                        jnp.concatenate([zeros_t, uk[:, LANES:]], axis=0)], axis=1)), p['rhs_r'])
           for p, uk in zip(pre, uks)]
    for p, qu, pv, dc in zip(pre, qus, pvs, dcs):
        p['y0'] = pv - qu[:, :LANES]
        p['rp'] = _bf(p['rt'] - qu[:, LANES:])
        p['d0'] = bd * dc[:LANES]
        p['c0'] = _bf(bd * dc[LANES:])
    states = [s_scr[j] for j in range(nhp)]
    for i in range(nper):
        ps = [pre[i * nhp + j] for j in range(nhp)]
        sbs = [_bf(sbd) for sbd in states]
        ys = [_dot_nt(p['rp'], sb) for p, sb in zip(ps, sbs)]
        scs = [_dot(sb, p['c0']) for p, sb in zip(ps, sbs)]
        for j in range(nhp):
            y_ref[0, 0, rows_of(i), sl_of(j)] = _bf(ps[j]['y0'] + ys[j])
        states = [sbd * p['g_tot'] - sc + p['d0'] for sbd, p, sc in zip(states, ps, scs)]
    for j in range(nhp):
        s_scr[j] = states[j]


def _rwkv_scan(proj, batch, seq_len, rwp, s0):
    n = proj.shape[0]
    tb = min(RW_BLOCK, seq_len)
    nc = seq_len // tb
    nhp = RW_W // LANES
    cur, prev, nxt = _halo_specs(RW_BLK, C_RW // RW_BLK, seq_len, tb, nc, n)
    full2 = lambda d, bi, c: (0, 0)
    full3 = lambda d, bi, c: (0, 0, 0)
    dmap = lambda d, bi, c: (d, bi, c + d * (nc - 1 - 2 * c), 0)
    smap = lambda d, bi, c: (d, bi, 0, 0, 0)
    return pl.pallas_call(
        _rwkv_kernel,
        grid=(2, batch, nc),
        in_specs=[
            cur, prev, nxt,
            pl.BlockSpec((1, RW_BLK), full2),
            pl.BlockSpec((1, 1, RW_W), lambda d, bi, c: (d, 0, 0)),
            pl.BlockSpec((1, 2 * RW_DECAY_R, RW_W), lambda d, bi, c: (d, 0, 0)),
            pl.BlockSpec((2, RW_W), full2),
            pl.BlockSpec((2, 2 * RW_ICLR_R, RW_W), full3),
            pl.BlockSpec((RW_GATE_R, RW_W), full2),
            pl.BlockSpec((3, RW_W), full2),
            pl.BlockSpec((RW_W, RW_W), full2),
            pl.BlockSpec((1, 1, nhp, LANES, LANES), smap),
        ],
        out_specs=[
            pl.BlockSpec((1, 1, tb, RW_W), dmap),
            pl.BlockSpec((1, 1, tb, RW_W), dmap),
            pl.BlockSpec((1, 1, nhp, LANES, LANES), smap),
        ],
        out_shape=[
            jax.ShapeDtypeStruct((2, batch, seq_len, RW_W), BF16),
            jax.ShapeDtypeStruct((2, batch, seq_len, RW_W), BF16),
            jax.ShapeDtypeStruct((2, batch, nhp, LANES, LANES), F32),
        ],
        scratch_shapes=[pltpu.VMEM((nhp, LANES, LANES), F32)],
        compiler_params=_cparams(("arbitrary", "arbitrary", "arbitrary")),
        name="rwkv_scan",
    )(proj, proj, proj, rwp['mu'], rwp['w0'], rwp['w2'], rwp['a0'], rwp['a2'], rwp['g2'], rwp['vec'],
      rwp['seg'], s0)


def _ssd_kernel(x_ref, xp_ref, xn_ref, dt_ref, cw_ref, cb_ref, dtb_ref, alog_ref, dskip_ref, s0_ref,
                y_ref, sf_ref, s_scr):
    d = pl.program_id(0)
    c = pl.program_id(2)
    nc = pl.num_programs(2)
    t = SSD_CHUNK
    gw = SSM_W // SSM_GROUPS
    hpg = SSM_HEADS // SSM_GROUPS
    nh = SSM_HEADS
    cc = c + d * (nc - 1 - 2 * c)

    @pl.when(c == 0)
    def _():
        s_scr[...] = s0_ref[0, 0]

    x = x_ref[...].astype(F32)
    prev = xp_ref[HALO_ROWS - 1:HALO_ROWS, :].astype(F32) * (cc > 0).astype(F32)
    nxt = xn_ref[0:1, :].astype(F32) * (cc < nc - 1).astype(F32)
    x_m1, x_p1 = _neighbour_rows(x, prev, nxt)
    xc = x_m1 * cw_ref[0:1, :] + x * cw_ref[1:2, :] + x_p1 * cw_ref[2:3, :] + cb_ref[...]
    xc = xc * _sigmoid(xc)
    xs = xc[:, :SSM_W]
    bm = xc[:, SSM_W:SSM_W + SSM_GROUPS * SSM_STATE]
    cm = xc[:, SSM_W + SSM_GROUPS * SSM_STATE:]
    dt2 = _softplus(dt_ref[...].astype(F32)[:, :2 * nh] + dtb_ref[...])
    a2 = dt2 * (-jnp.exp(alog_ref[...]))
    fwd = d == 0
    dt = jnp.where(fwd, dt2[:, :nh], dt2[:, nh:])
    a = jnp.where(fwd, a2[:, :nh], a2[:, nh:])

    sign = 1 - 2 * d
    row = lax.broadcasted_iota(jnp.int32, (t, t), 0)
    col = lax.broadcasted_iota(jnp.int32, (t, t), 1)
    delta = (row - col) * sign
    incl = delta >= 0
    tri_b = incl.astype(BF16)
    tri_t_b = (delta <= 0).astype(BF16)
    lane = lax.broadcasted_iota(jnp.int32, (1, LANES), 1)
    m0 = (lane < SSM_HD).astype(F32)
    m1 = 1.0 - m0

    a3 = jnp.concatenate(_split3(a), axis=1)
    cs3 = _dot(tri_b, a3)
    cs = cs3[:, :nh] + cs3[:, nh:2 * nh] + cs3[:, 2 * nh:]
    cs3_t = _dot_tn(a3, tri_t_b)
    cs_t = cs3_t[:nh] + cs3_t[nh:2 * nh] + cs3_t[2 * nh:]
    tot = jnp.sum(a, axis=0, keepdims=True)
    e_cs = jnp.exp(cs)
    e_rem = jnp.exp(tot - cs)
    e_tot = jnp.exp(tot)

    lane_g = lax.broadcasted_iota(jnp.int32, (1, gw), 1) // SSM_HD

    def expand(z, g):
        out = jnp.zeros((z.shape[0], gw), F32)
        for hh in range(hpg):
            h = g * hpg + hh
            out = jnp.where(lane_g == hh, z[:, h:h + 1], out)
        return out

    skip = fwd.astype(F32)
    for g in range(SSM_GROUPS):
        cg = cm[:, g * SSM_STATE:(g + 1) * SSM_STATE]
        bg = bm[:, g * SSM_STATE:(g + 1) * SSM_STATE]
        xsg = xs[:, g * gw:(g + 1) * gw]
        xg = xsg * expand(dt, g)
        cgb = _bf(cg)
        bgb = _bf(bg)
        gm = _dot_nt(cgb, bgb)
        st = s_scr[g]
        y_off = _dot(cgb, _bf(st)) * expand(e_cs, g)
        ys = []
        for hp in range(hpg // 2):
            scs = []
            for hh in range(2):
                h = g * hpg + 2 * hp + hh
                diff = jnp.where(incl, cs[:, h:h + 1] - cs_t[h:h + 1, :], 0.0)
                scs.append(gm * jnp.where(incl, jnp.exp(diff), 0.0))
            xp = xg[:, hp * LANES:(hp + 1) * LANES]
            xst = _bf(jnp.concatenate([xp * m0, xp * m1], axis=0))
            ys.append(_dot(_bf(jnp.concatenate(scs, axis=1)), xst))
        y_ref[0, 0, :, g * gw:(g + 1) * gw] = _bf(jnp.concatenate(ys, axis=1) + y_off
                                                  + skip * dskip_ref[:, g * gw:(g + 1) * gw] * xsg)
        s_scr[g] = st * expand(e_tot, g) + _dot_tn(bgb, _bf(xg * expand(e_rem, g)))

    @pl.when(c == nc - 1)
    def _():
        sf_ref[0, 0] = s_scr[...]


def _ssd_scan(proj, batch, seq_len, ssp, s0):
    n = proj.shape[0]
    t = min(SSD_CHUNK, seq_len)
    nc = seq_len // t
    gw = SSM_W // SSM_GROUPS
    cur, prev, nxt = _halo_specs(SSM_XBC, C_XBC // SSM_XBC, seq_len, t, nc, n)
    dtspec, _, _ = _halo_specs(LANES, C_DT // LANES, seq_len, t, nc, n)
    full2 = lambda d, bi, c: (0, 0)
    dmap = lambda d, bi, c: (d, bi, c + d * (nc - 1 - 2 * c), 0)
    smap = lambda d, bi, c: (d, bi, 0, 0, 0)
    return pl.pallas_call(
        _ssd_kernel,
        grid=(2, batch, nc),
        in_specs=[
            cur, prev, nxt, dtspec,
            pl.BlockSpec((3, SSM_XBC), full2),
            pl.BlockSpec((1, SSM_XBC), full2),
            pl.BlockSpec((1, 2 * SSM_HEADS), full2),
            pl.BlockSpec((1, 2 * SSM_HEADS), full2),
            pl.BlockSpec((1, SSM_W), full2),
            pl.BlockSpec((1, 1, SSM_GROUPS, SSM_STATE, gw), smap),
        ],
        out_specs=[
            pl.BlockSpec((1, 1, t, SSM_W), dmap),
            pl.BlockSpec((1, 1, SSM_GROUPS, SSM_STATE, gw), smap),
        ],
        out_shape=[
            jax.ShapeDtypeStruct((2, batch, seq_len, SSM_W), BF16),
            jax.ShapeDtypeStruct((2, batch, SSM_GROUPS, SSM_STATE, gw), F32),
        ],
        scratch_shapes=[pltpu.VMEM((SSM_GROUPS, SSM_STATE, gw), F32)],
        compiler_params=_cparams(("arbitrary", "arbitrary", "arbitrary")),
        name="ssd_scan",
    )(proj, proj, proj, proj, ssp['conv_w'], ssp['conv_b'], ssp['dt_bias'], ssp['a_log'], ssp['dskip'], s0)


def _merge_kernel(ya_ref, yr_ref, aux_ref, ys_ref, z_ref, ga_ref, gr_ref, gs_ref, x_ref, g1_ref,
                  seg_ref, rg_ref, rb_ref, sg_ref, pa_ref, pr_ref, ps_ref, wo_ref, lg_ref, lb_ref, o_ref):
    seg = seg_ref[...]
    yr = yr_ref[0].astype(F32) + yr_ref[1].astype(F32)
    mu = _seg_sum(yr, seg) * (1.0 / RW_HD)
    dv = yr - mu
    var = _seg_sum(dv * dv, seg) * (1.0 / RW_HD)
    yr = (dv * lax.rsqrt(var + RW_GN_EPS) * rg_ref[...] + rb_ref[...] + aux_ref[0].astype(F32)) * aux_ref[1].astype(F32)
    z = z_ref[...].astype(F32)
    ys = (ys_ref[0].astype(F32) + ys_ref[1].astype(F32)) * (z * _sigmoid(z))
    gw = SSM_W // SSM_GROUPS
    parts = []
    for g in range(SSM_GROUPS):
        yg = ys[:, g * gw:(g + 1) * gw]
        parts.append(yg * lax.rsqrt(jnp.mean(yg * yg, -1, keepdims=True) + NORM_EPS))
    ys = jnp.concatenate(parts, axis=1) * sg_ref[...]
    m = (_sigmoid(ga_ref[...].astype(F32)) * _dot(ya_ref[...], pa_ref[...])
         + _sigmoid(gr_ref[...].astype(F32)) * _dot(_bf(yr), pr_ref[...])
         + _sigmoid(gs_ref[...].astype(F32)) * _dot(_bf(ys), ps_ref[...]))
    out = _dot(_bf(m), wo_ref[...])
    o_ref[...] = _layernorm_rows(ALPHA * x_ref[...] + g1_ref[0] * out, lg_ref[...], lb_ref[...])


def _merge(ya, yr2, aux, ys2, proj, x2d, g1, lp, seq_len):
    n = x2d.shape[0]
    tm = min(256, seq_len)
    per_b = seq_len // tm
    row = lambda i: (i, 0)
    row3 = lambda i: (0, i, 0)
    full = lambda i: (0, 0)
    gate = lambda col: pl.BlockSpec((tm, D_MODEL), lambda i: (i, col // D_MODEL))
    vec512 = pl.BlockSpec((1, RW_W), full)
    vec = pl.BlockSpec((1, D_MODEL), full)
    return pl.pallas_call(
        _merge_kernel,
        grid=(n // tm,),
        in_specs=[
            pl.BlockSpec((tm, DA_W), row),
            pl.BlockSpec((2, tm, RW_W), row3), pl.BlockSpec((2, tm, RW_W), row3),
            pl.BlockSpec((2, tm, SSM_W), row3),
            pl.BlockSpec((tm, SSM_W), lambda i: (i, C_Z // SSM_W)),
            gate(C_GA), gate(C_GR), gate(C_GS),
            pl.BlockSpec((tm, D_MODEL), row),
            pl.BlockSpec((1, 1, D_MODEL), lambda i: (i // per_b, 0, 0)),
            pl.BlockSpec((RW_W, RW_W), full), vec512, vec512, vec512,
            pl.BlockSpec((DA_W, D_MODEL), full), pl.BlockSpec((RW_W, D_MODEL), full),
            pl.BlockSpec((SSM_W, D_MODEL), full), pl.BlockSpec((D_MODEL, D_MODEL), full),
            vec, vec,
        ],
        out_specs=pl.BlockSpec((tm, D_MODEL), row),
        out_shape=jax.ShapeDtypeStruct((n, D_MODEL), F32),
        compiler_params=_cparams(("arbitrary",)),
        name="merge_ln",
    )(ya, yr2, aux, ys2, proj, proj, proj, proj, x2d, g1, lp['rw']['seg'], lp['rw_ln_g'], lp['rw_ln_b'],
      lp['ssm_norm_g'], lp['p_attn'], lp['p_rwkv'], lp['p_ssm'], lp['w_out'], lp['ln1_g'], lp['ln1_b'])


def _ffn_kernel(x_ref, sc_ref, sh_ref, g2_ref, w1_ref, w3_ref, w2_ref, lg_ref, lb_ref, o_ref, h_scr, acc_scr):
    j = pl.program_id(1)

    @pl.when(j == 0)
    def _():
        h_scr[...] = (x_ref[...] * (1.0 + sc_ref[0]) + sh_ref[0]).astype(BF16)
        acc_scr[...] = jnp.zeros_like(acc_scr)

    h = h_scr[...]
    u = jax.nn.silu(_dot(h, w1_ref[...])) * _dot(h, w3_ref[...])
    acc_scr[...] += _dot(u.astype(BF16), w2_ref[...])

    @pl.when(j == pl.num_programs(1) - 1)
    def _():
        o_ref[...] = _layernorm_rows(ALPHA * x_ref[...] + g2_ref[0] * acc_scr[...], lg_ref[...], lb_ref[...])


def _ffn(x2d, sc, sh, g2, w1, w3, w2, lg, lb, seq_len):
    n = x2d.shape[0]
    tm = min(512, seq_len)
    th = FFN_HIDDEN // 2
    per_b = seq_len // tm
    mod = pl.BlockSpec((1, 1, D_MODEL), lambda i, j: (i // per_b, 0, 0))
    return pl.pallas_call(
        _ffn_kernel,
        grid=(n // tm, FFN_HIDDEN // th),
        in_specs=[
            pl.BlockSpec((tm, D_MODEL), lambda i, j: (i, 0)),
            mod, mod, mod,
            pl.BlockSpec((D_MODEL, th), lambda i, j: (0, j)),
            pl.BlockSpec((D_MODEL, th), lambda i, j: (0, j)),
            pl.BlockSpec((th, D_MODEL), lambda i, j: (j, 0)),
            pl.BlockSpec((1, D_MODEL), lambda i, j: (0, 0)),
            pl.BlockSpec((1, D_MODEL), lambda i, j: (0, 0)),
        ],
        out_specs=pl.BlockSpec((tm, D_MODEL), lambda i, j: (i, 0)),
        out_shape=jax.ShapeDtypeStruct((n, D_MODEL), F32),
        scratch_shapes=[pltpu.VMEM((tm, D_MODEL), BF16), pltpu.VMEM((tm, D_MODEL), F32)],
        compiler_params=_cparams(("arbitrary", "arbitrary")),
        name="ffn_ln",
    )(x2d, sc, sh, g2, w1, w3, w2, lg, lb)


def _rope_tables(n_tok):
    rows = n_tok // GRID_W
    row = jnp.repeat(jnp.arange(rows), GRID_W).astype(F32)
    col = jnp.tile(jnp.arange(GRID_W), rows).astype(F32)
    nf = DA_HD // 4
    inv = ROPE_BASE ** (-jnp.arange(nf, dtype=F32) / nf)
    ang = jnp.concatenate([row[:, None] * inv] * 2 + [col[:, None] * inv] * 2, axis=-1)
    cos = jnp.cos(ang)
    sin = jnp.sin(ang)
    first = (jnp.arange(DA_HD) % (2 * nf)) < nf
    s_dn = jnp.where(first, -sin, 0.0)
    s_up = jnp.where(first, 0.0, sin)
    reps = LANES // DA_HD
    return jnp.tile(cos, (1, reps)), jnp.tile(s_dn, (1, reps)), jnp.tile(s_up, (1, reps))


def _permute_w_in(w_in):
    parts, acc = [], 0
    for s in W_IN_SPLIT:
        parts.append(w_in[:, acc:acc + s])
        acc += s
    q, k, v, rw, z, xbc, dt, ga, gr, gs = parts
    pad = jnp.zeros((w_in.shape[0], RW_BLK - RW_COLS - dt.shape[1]), w_in.dtype)
    return jnp.concatenate([rw, dt, pad, ga, gr, gs, q, k, v, z, xbc], axis=1).astype(BF16)


def _pad_lora(w, rank):
    z = jnp.zeros_like(w[0])
    return jnp.stack([jnp.concatenate([w[0], z], axis=0), jnp.concatenate([z, w[1]], axis=0)]).astype(BF16)


def _mixer(x, mod, lp, layer, side, want_out, rope_tabs):
    b, l, _ = x.shape
    sh1, sc1, g1 = mod
    n = b * l
    x2d = x.reshape(n, D_MODEL)
    proj = _inproj(x2d, sc1, sh1, lp['w_in_p'], l)
    if side is None:
        proj_ctx, ctx_len = None, 0
        s0_r = jnp.zeros((2, b, RW_W // LANES, LANES, LANES), F32)
        s0_s = jnp.zeros((2, b, SSM_GROUPS, SSM_STATE, SSM_W // SSM_GROUPS), F32)
    else:
        (proj_ctx, ctx_len), s0_r, s0_s = side
    y_r2, aux, sf_r = _rwkv_scan(proj, b, l, lp['rw'], s0_r)
    y_s2, sf_s = _ssd_scan(proj, b, l, lp['ssm'], s0_s)
    new_side = ((proj, l), sf_r, sf_s)
    if not want_out:
        return None, new_side
    lam_init = 0.8 - 0.6 * math.exp(-0.3 * layer)
    lam = (jnp.exp(jnp.sum(lp['da_lq1'] * lp['da_lk1'])) - jnp.exp(jnp.sum(lp['da_lq2'] * lp['da_lk2']))
           + lam_init).reshape(1)
    y_a = _diff_attention(lam, proj, b, l, lp['da_norm_g'].reshape(1, 2 * DA_HD), 1.0 - lam_init,
                          proj_ctx, ctx_len, rope_tabs)
    x1 = _merge(y_a, y_r2.reshape(2, n, RW_W), aux.reshape(2, n, RW_W),
                y_s2.reshape(2, n, SSM_W), proj, x2d, g1, lp, l)
    return x1, new_side


def kernel(x, c, ctx, c_ctx, ada_w, ada_b, w_in, da_lq1, da_lk1, da_lq2, da_lk2, da_norm_g,
           rw_mu, rw_w0, rw_w2, rw_a0, rw_a2, rw_g2, rw_kk, rw_ka, rw_rk, rw_ln_g, rw_ln_b,
           ssm_conv_w, ssm_conv_b, ssm_dt_bias, ssm_a_log, ssm_d, ssm_norm_g,
           p_attn, p_rwkv, p_ssm, w_out, ln1_g, ln1_b, ffn_w1, ffn_w3, ffn_w2, ln2_g, ln2_b):
    b, l, _ = x.shape
    lc = ctx.shape[1]
    rope_tabs = _rope_tables(l)
    seg = (jnp.arange(RW_W)[:, None] // RW_HD == jnp.arange(RW_W)[None, :] // RW_HD).astype(BF16)
    xc = ctx
    for i in range(DEPTH):
        rw = {
            'mu': jnp.pad(rw_mu[i], (0, RW_BLK - RW_COLS)).reshape(1, RW_BLK),
            'w0': rw_w0[i].reshape(2, 1, RW_W),
            'w2': _pad_lora(rw_w2[i], RW_DECAY_R),
            'a0': rw_a0[i],
            'a2': _pad_lora(rw_a2[i], RW_ICLR_R),
            'g2': rw_g2[i].astype(BF16),
            'vec': jnp.stack([rw_kk[i], rw_ka[i], rw_rk[i].reshape(RW_W)]),
            'seg': seg,
        }
        ssm = {
            'conv_w': ssm_conv_w[i], 'conv_b': ssm_conv_b[i].reshape(1, SSM_XBC),
            'dt_bias': ssm_dt_bias[i].reshape(1, 2 * SSM_HEADS), 'a_log': ssm_a_log[i].reshape(1, 2 * SSM_HEADS),
            'dskip': jnp.repeat(ssm_d[i], SSM_HD).reshape(1, SSM_W),
        }
        lp = {
            'w_in_p': _permute_w_in(w_in[i]), 'rw': rw, 'ssm': ssm,
            'da_lq1': da_lq1[i], 'da_lk1': da_lk1[i], 'da_lq2': da_lq2[i], 'da_lk2': da_lk2[i],
            'da_norm_g': da_norm_g[i],
            'rw_ln_g': rw_ln_g[i].reshape(1, RW_W), 'rw_ln_b': rw_ln_b[i].reshape(1, RW_W),
            'ssm_norm_g': ssm_norm_g[i].reshape(1, SSM_W),
            'p_attn': p_attn[i].astype(BF16), 'p_rwkv': p_rwkv[i].astype(BF16),
            'p_ssm': p_ssm[i].astype(BF16), 'w_out': w_out[i].astype(BF16),
            'ln1_g': ln1_g[i].reshape(1, D_MODEL), 'ln1_b': ln1_b[i].reshape(1, D_MODEL),
        }
        w1, w3, w2 = ffn_w1[i].astype(BF16), ffn_w3[i].astype(BF16), ffn_w2[i].astype(BF16)
        l2g, l2b = ln2_g[i].reshape(1, D_MODEL), ln2_b[i].reshape(1, D_MODEL)
        last = i == DEPTH - 1
        mod_x = (jax.nn.silu(c) @ ada_w[i] + ada_b[i])[:, None, :]
        mod_c = jnp.broadcast_to((jax.nn.silu(c_ctx) @ ada_w[i] + ada_b[i])[None, None, :], (b, 1, 6 * D_MODEL))
        sh1, sc1, g1, sh2, sc2, g2 = jnp.split(mod_x, 6, axis=-1)
        csh1, csc1, cg1, csh2, csc2, cg2 = jnp.split(mod_c, 6, axis=-1)
        xc1, side = _mixer(xc, (csh1, csc1, cg1), lp, i, None, not last, None)
        x1, _ = _mixer(x, (sh1, sc1, g1), lp, i, side, True, rope_tabs)
        x = _ffn(x1, sc2, sh2, g2, w1, w3, w2, l2g, l2b, l).reshape(b, l, D_MODEL)
        if not last:
            xc = _ffn(xc1, csc2, csh2, cg2, w1, w3, w2, l2g, l2b, lc).reshape(b, lc, D_MODEL)
    return x
```

```python
import functools
import math

import jax
import jax.numpy as jnp
from jax import lax
from jax.experimental import pallas as pl
from jax.experimental.pallas import tpu as pltpu

F32 = jnp.float32
BF16 = jnp.bfloat16

D_MODEL = 1024
DEPTH = 4
GRID_W = 64
DA_HEADS = 4
DA_HD = 64
DA_W = DA_HEADS * 2 * DA_HD
ROPE_BASE = 10000.0
RW_HEADS = 8
RW_HD = 64
RW_W = RW_HEADS * RW_HD
RW_DECAY_R = 64
RW_ICLR_R = 64
RW_GATE_R = 128
RW_GN_EPS = 64e-5
RW_COLS = 3 * RW_W + 2 * RW_DECAY_R + 2 * RW_ICLR_R + RW_GATE_R
SSM_HEADS = 8
SSM_HD = 64
SSM_W = SSM_HEADS * SSM_HD
SSM_GROUPS = 2
SSM_STATE = 128
SSM_XBC = SSM_W + 2 * SSM_GROUPS * SSM_STATE
FFN_HIDDEN = (8 * D_MODEL + 3 * 256 - 1) // (3 * 256) * 256
NORM_EPS = 1e-5
W_IN_SPLIT = (DA_W, DA_W, DA_W, RW_COLS, SSM_W, SSM_XBC, 2 * SSM_HEADS, D_MODEL, D_MODEL, D_MODEL)
ALPHA = (2.0 * DEPTH) ** 0.25

LANES = 128
SUBLANES = 8
HALO_ROWS = 16
VMEM_LIMIT_BYTES = 56 * 1024 * 1024

PROJ_COLS = 8192
RW_BLK = 2048
C_RW = 0
C_DT = RW_COLS
C_GA, C_GR, C_GS = RW_BLK, RW_BLK + D_MODEL, RW_BLK + 2 * D_MODEL
C_Q = RW_BLK + 3 * D_MODEL
C_K = C_Q + DA_W
C_V = C_K + DA_W
C_Z = C_V + DA_W
C_XBC = C_Z + SSM_W

Q_SCALE = math.log2(math.e) * DA_HD ** -0.5
ATTN_SUBTILE = 128
ATTN_BLOCK = 512
RW_CHUNK = 64
RW_BLOCK = 256
SSD_CHUNK = 128
SSD_BLOCK = 256


def _cparams(sem):
    return pltpu.CompilerParams(dimension_semantics=sem, vmem_limit_bytes=VMEM_LIMIT_BYTES)


def _dot(a, b):
    return jnp.dot(a, b, preferred_element_type=F32)


def _dot_nt(a, b):
    return lax.dot_general(a, b, (((1,), (1,)), ((), ())), preferred_element_type=F32)


def _dot_tn(a, b):
    return lax.dot_general(a, b, (((0,), (0,)), ((), ())), preferred_element_type=F32)


def _bf(x):
    return x.astype(BF16)


def _split3(x):
    hi = _bf(x)
    r1 = x - hi.astype(F32)
    mid = _bf(r1)
    lo = _bf(r1 - mid.astype(F32))
    return hi, mid, lo


def _dot3(a, b):
    ah, am, _ = _split3(a)
    bh, bm, _ = _split3(b)
    return _dot(ah, bh) + (_dot(ah, bm) + _dot(am, bh))


def _cumsum_rows(tri_b, x):
    n = x.shape[1]
    hi, mid, lo = _split3(x)
    o = _dot(tri_b, jnp.concatenate([hi, mid, lo], axis=1))
    return o[:, :n] + o[:, n:2 * n] + o[:, 2 * n:]


def _seg_sum(x, seg_b):
    m = x.shape[0]
    hi = _bf(x)
    lo = _bf(x - hi.astype(F32))
    o = _dot(jnp.concatenate([hi, lo], axis=0), seg_b)
    return o[:m] + o[m:]


def _sigmoid(x):
    return 1.0 / (1.0 + jnp.exp(-x))


def _softplus(x):
    return jnp.maximum(x, 0.0) + jnp.log(1.0 + jnp.exp(-jnp.abs(x)))


def _layernorm_rows(t, g, b):
    mu = jnp.mean(t, -1, keepdims=True)
    d = t - mu
    var = jnp.mean(d * d, -1, keepdims=True)
    return d * lax.rsqrt(var + NORM_EPS) * g + b


def _neighbour_rows(x, prev_row, next_row):
    t = x.shape[0]
    rowi = lax.broadcasted_iota(jnp.int32, (t, 1), 0)
    x_m1 = jnp.where(rowi == 0, prev_row, pltpu.roll(x, 1, 0))
    x_p1 = jnp.where(rowi == t - 1, next_row, pltpu.roll(x, t - 1, 0))
    return x_m1, x_p1


def _halo_specs(width, col_blk, seq_len, tb, nc, n_rows):
    per8 = tb // HALO_ROWS
    last8 = n_rows // HALO_ROWS - 1

    def blk(d, bi, c):
        return bi * nc + c + d * (nc - 1 - 2 * c)

    prev = pl.BlockSpec((HALO_ROWS, width), lambda d, bi, c: (jnp.maximum(blk(d, bi, c) * per8 - 1, 0), col_blk))
    nxt = pl.BlockSpec((HALO_ROWS, width),
                       lambda d, bi, c: (jnp.minimum((blk(d, bi, c) + 1) * per8, last8), col_blk))
    cur = pl.BlockSpec((tb, width), lambda d, bi, c: (blk(d, bi, c), col_blk))
    return cur, prev, nxt


def _inproj_kernel(x_ref, sc_ref, sh_ref, w_ref, o_ref, h_scr):
    @pl.when(pl.program_id(1) == 0)
    def _():
        h = x_ref[...] * (1.0 + sc_ref[0]) + sh_ref[0]
        h_scr[...] = h.astype(BF16)

    o_ref[...] = _bf(_dot(h_scr[...], w_ref[...]))


def _inproj(x2d, sc, sh, w, seq_len):
    n = x2d.shape[0]
    tm = min(1024, seq_len)
    tn = 2048
    per_b = seq_len // tm
    return pl.pallas_call(
        _inproj_kernel,
        grid=(n // tm, PROJ_COLS // tn),
        in_specs=[
            pl.BlockSpec((tm, D_MODEL), lambda i, j: (i, 0)),
            pl.BlockSpec((1, 1, D_MODEL), lambda i, j: (i // per_b, 0, 0)),
            pl.BlockSpec((1, 1, D_MODEL), lambda i, j: (i // per_b, 0, 0)),
            pl.BlockSpec((D_MODEL, tn), lambda i, j: (0, j)),
        ],
        out_specs=pl.BlockSpec((tm, tn), lambda i, j: (i, j)),
        out_shape=jax.ShapeDtypeStruct((n, PROJ_COLS), BF16),
        scratch_shapes=[pltpu.VMEM((tm, D_MODEL), BF16)],
        compiler_params=_cparams(("arbitrary", "arbitrary")),
        name="inproj",
    )(x2d, sc, sh, w)


def _rope_lanes(x, cos, s_dn, s_up):
    nf = DA_HD // 4
    return x * cos + pltpu.roll(x, LANES - nf, 1) * s_dn + pltpu.roll(x, nf, 1) * s_up


def _attn_core(lam, q, k_scr, v_scr, g_ref, o_ref, out_scale):
    tq = ATTN_SUBTILE
    nsub = q.shape[0] // tq
    lk = k_scr.shape[0]
    lane = lax.broadcasted_iota(jnp.int32, (1, LANES), 1)
    mid = (lk // 2) // LANES * LANES
    cuts = (0, mid, lk) if mid else (0, lk)
    spans = list(zip(cuts[:-1], cuts[1:]))
    scores = []
    for t in range(nsub):
        qt = q[t * tq:(t + 1) * tq]
        zero = jnp.zeros_like(qt)
        qq = jnp.concatenate([jnp.where(lane < DA_HD, qt, zero), jnp.where(lane >= DA_HD, qt, zero)], axis=0)
        scores.append([_dot_nt(qq, k_scr[lo:hi, :]) for lo, hi in spans])
    for t, ss in enumerate(scores):
        m = functools.reduce(jnp.maximum, [jnp.max(s, -1, keepdims=True) for s in ss])
        ov = functools.reduce(jnp.add, [_dot(_bf(jnp.exp2(s - m)), v_scr[lo:hi, :]) for s, (lo, hi) in zip(ss, spans)])
        on = ov[:, :LANES] / ov[:, LANES:]
        o = on[:tq] - lam * on[tq:]
        ms = jnp.mean(o * o, -1, keepdims=True)
        o_ref[t * tq:(t + 1) * tq, :] = _bf(o * lax.rsqrt(ms + NORM_EPS) * (g_ref[...] * out_scale))


def _attn_latent_kernel(lam_ref, q_ref, kl_ref, vl_ref, kc_ref, vc_ref, cos_ref, sdn_ref, sup_ref, g_ref,
                        o_ref, k_scr, v_scr, *, out_scale):
    i = pl.program_id(2)
    tq = q_ref.shape[0]
    lc = kc_ref.shape[0]

    @pl.when(i == 0)
    def _():
        k_scr[:lc, :] = kc_ref[...]
        k_scr[lc:, :] = _bf(_rope_lanes(kl_ref[...].astype(F32), cos_ref[...], sdn_ref[...], sup_ref[...]))
        v_scr[:lc, :LANES] = vc_ref[...]
        v_scr[lc:, :LANES] = vl_ref[...]
        v_scr[:, LANES:] = jnp.ones((v_scr.shape[0], LANES), BF16)

    rows = pl.ds(pl.multiple_of(i * tq, tq), tq)
    q = _rope_lanes(q_ref[...].astype(F32), cos_ref[rows, :], sdn_ref[rows, :], sup_ref[rows, :])
    _attn_core(lam_ref[0], _bf(q * Q_SCALE), k_scr, v_scr, g_ref, o_ref, out_scale)


def _attn_ctx_kernel(lam_ref, q_ref, kl_ref, vl_ref, g_ref, o_ref, k_scr, v_scr, *, out_scale):
    @pl.when(pl.program_id(2) == 0)
    def _():
        k_scr[...] = kl_ref[...]
        v_scr[:, :LANES] = vl_ref[...]
        v_scr[:, LANES:] = jnp.ones((v_scr.shape[0], LANES), BF16)

    _attn_core(lam_ref[0], _bf(q_ref[...].astype(F32) * Q_SCALE), k_scr, v_scr, g_ref, o_ref, out_scale)


def _diff_attention(lam, proj, batch, seq_len, g, out_scale, proj_ctx=None, ctx_len=0, rope_tabs=None):
    tq = min(ATTN_BLOCK, seq_len)
    nq = seq_len // tq
    lk = seq_len + ctx_len
    cq, ck, cv = C_Q // LANES, C_K // LANES, C_V // LANES
    smem = pl.BlockSpec(memory_space=pltpu.SMEM)
    qspec = pl.BlockSpec((tq, LANES), lambda bi, h, i: (bi * nq + i, cq + h))
    kspec = pl.BlockSpec((seq_len, LANES), lambda bi, h, i: (bi, ck + h))
    vspec = pl.BlockSpec((seq_len, LANES), lambda bi, h, i: (bi, cv + h))
    gspec = pl.BlockSpec((1, LANES), lambda bi, h, i: (0, 0))
    common = dict(
        grid=(batch, DA_HEADS, nq),
        out_specs=pl.BlockSpec((tq, LANES), lambda bi, h, i: (bi * nq + i, h)),
        out_shape=jax.ShapeDtypeStruct((batch * seq_len, DA_W), BF16),
        scratch_shapes=[pltpu.VMEM((lk, LANES), BF16), pltpu.VMEM((lk, 2 * LANES), BF16)],
        compiler_params=_cparams(("arbitrary", "arbitrary", "arbitrary")),
    )
    if proj_ctx is None:
        return pl.pallas_call(
            functools.partial(_attn_ctx_kernel, out_scale=out_scale),
            in_specs=[smem, qspec, kspec, vspec, gspec], name="diff_attn_ctx", **common,
        )(lam, proj, proj, proj, g)
    tab = pl.BlockSpec((seq_len, LANES), lambda bi, h, i: (0, 0))
    return pl.pallas_call(
        functools.partial(_attn_latent_kernel, out_scale=out_scale),
        in_specs=[smem, qspec, kspec, vspec,
                  pl.BlockSpec((ctx_len, LANES), lambda bi, h, i: (bi, ck + h)),
                  pl.BlockSpec((ctx_len, LANES), lambda bi, h, i: (bi, cv + h)),
                  tab, tab, tab, gspec],
        name="diff_attn", **common,
    )(lam, proj, proj, proj, proj_ctx, proj_ctx, *rope_tabs, g)


def _rwkv_kernel(u_ref, up_ref, un_ref, mu_ref, w0_ref, w2_ref, a0_ref, a2_ref, g2_ref, vec_ref, seg_ref,
                 s0_ref, y_ref, aux_ref, sf_ref, s_scr):
    d = pl.program_id(0)
    c = pl.program_id(2)
    nc = pl.num_programs(2)

    @pl.when(c == 0)
    def _():
        s_scr[...] = s0_ref[0, 0]

    refs = (u_ref, up_ref, un_ref, mu_ref, w0_ref, w2_ref, a0_ref, a2_ref, g2_ref, vec_ref, seg_ref,
            y_ref, aux_ref, s_scr)

    @pl.when(d == 0)
    def _():
        _rwkv_step(True, c, nc, *refs)

    @pl.when(d == 1)
    def _():
        _rwkv_step(False, nc - 1 - c, nc, *refs)

    @pl.when(c == nc - 1)
    def _():
        sf_ref[0, 0] = s_scr[...]


def _rwkv_step(fwd, cc, nc, u_ref, up_ref, un_ref, mu_ref, w0_ref, w2_ref, a0_ref, a2_ref, g2_ref, vec_ref,
               seg_ref, y_ref, aux_ref, s_scr):
    t = RW_CHUNK
    tb = u_ref.shape[0]
    nper = tb // t
    di = 0 if fwd else 1

    u = u_ref[:, :RW_COLS].astype(F32)
    prev = up_ref[HALO_ROWS - 1:HALO_ROWS, :RW_COLS].astype(F32) * (cc > 0).astype(F32)
    nxt = un_ref[0:1, :RW_COLS].astype(F32) * (cc < nc - 1).astype(F32)
    u_m1, u_p1 = _neighbour_rows(u, prev, nxt)
    u = u + mu_ref[:, :RW_COLS] * (0.5 * (u_m1 + u_p1) - u)
    r = r_all = u[:, :RW_W]
    k = u[:, RW_W:2 * RW_W]
    v = v_all = u[:, 2 * RW_W:3 * RW_W]
    o = 3 * RW_W
    wd = _bf(jnp.tanh(u[:, o:o + 2 * RW_DECAY_R]))
    o += 2 * RW_DECAY_R
    ad = _bf(u[:, o:o + 2 * RW_ICLR_R])
    o += 2 * RW_ICLR_R
    gd = u[:, o:o + RW_GATE_R]
    kk_gain, ka, rk = vec_ref[0:1, :], vec_ref[1:2, :], vec_ref[2:3, :]
    seg = seg_ref[...]
    lw_all = -math.exp(-0.5) * _sigmoid(w0_ref[0] + _dot(wd, w2_ref[0]))
    a_all = _sigmoid(a0_ref[di:di + 1, :] + _dot(ad, a2_ref[di]))
    kk_all = k * kk_gain
    kk_all = kk_all * lax.rsqrt(jnp.maximum(_seg_sum(kk_all * kk_all, seg), 1e-24))
    kd_all = k * (1.0 + (a_all - 1.0) * ka)
    if fwd:
        a_other = _sigmoid(a0_ref[1:2, :] + _dot(ad, a2_ref[1]))
        k_bonus = k * (1.0 + (0.5 * (a_all + a_other) - 1.0) * ka)
        aux_ref[0, 0] = _bf(_seg_sum(r * k_bonus * rk, seg) * v)
    else:
        aux_ref[0, 0] = _bf(_dot(_bf(_sigmoid(gd)), g2_ref[...]))

    sign = 1 if fwd else -1
    row = lax.broadcasted_iota(jnp.int32, (t, t), 0)
    col = lax.broadcasted_iota(jnp.int32, (t, t), 1)
    tri_b = ((row - col) * sign >= 0).astype(BF16)
    row2 = lax.broadcasted_iota(jnp.int32, (t, 2 * t), 0)
    col2 = lax.broadcasted_iota(jnp.int32, (t, 2 * t), 1)
    col2 = jnp.where(col2 >= t, col2 - t, col2)
    delta2 = (row2 - col2) * sign
    strict2 = delta2 > 0
    incl2 = delta2 >= 0
    colh = lax.broadcasted_iota(jnp.int32, (1, 2 * t), 1)
    cm0 = (colh < t).astype(F32)
    cm1 = 1.0 - cm0
    lane = lax.broadcasted_iota(jnp.int32, (1, LANES), 1)
    m0 = (lane < RW_HD).astype(F32)
    m1 = 1.0 - m0
    rb = lax.broadcasted_iota(jnp.int32, (LANES, LANES), 0) // RW_HD
    cb = lax.broadcasted_iota(jnp.int32, (LANES, LANES), 1) // RW_HD
    bd = (rb == cb).astype(F32)
    eye = (lax.broadcasted_iota(jnp.int32, (2 * t, 2 * t), 0)
           == lax.broadcasted_iota(jnp.int32, (2 * t, 2 * t), 1)).astype(F32)

    def stack2(x):
        return _bf(jnp.concatenate([x * m0, x * m1], axis=0))

    nhp = RW_W // LANES
    inst = [(j, i) for i in range(nper) for j in range(nhp)]

    def rows_of(i):
        ip = i if fwd else nper - 1 - i
        return slice(ip * t, (ip + 1) * t)

    def sl_of(j):
        return slice(j * LANES, (j + 1) * LANES)

    lws = [lw_all[rows_of(i), sl_of(j)] for j, i in inst]
    cums = [_cumsum_rows(tri_b, lw) for lw in lws]
    pre = []
    for (j, i), lw, cum in zip(inst, lws, cums):
        rows, sl = rows_of(i), sl_of(j)
        r = r_all[rows, sl]
        v = v_all[rows, sl]
        kk = kk_all[rows, sl]
        kd = kd_all[rows, sl]
        a = a_all[rows, sl]
        tot = jnp.sum(lw, axis=0, keepdims=True)
        g_inv = jnp.exp(-cum)
        g_rem = jnp.exp(tot - cum)
        rt = r * jnp.exp(cum)
        kt = _bf(kk * jnp.exp(cum - lw))
        b = kk * a
        bh = b * g_inv
        kh = kd * g_inv
        ystack = _bf(jnp.concatenate([bh * m0, bh * m1, kh * m0, kh * m1], axis=0))
        pre.append(dict(v=v, rt=rt, kt=kt, ystack=ystack, g_tot=jnp.exp(tot), vst=stack2(v),
                        rhs_r=_bf(jnp.concatenate([kd * g_rem, b * g_rem], axis=0))))
    ggs = [_dot_nt(jnp.concatenate([p['kt'], _bf(p['rt'])], axis=0), p['ystack']) for p in pre]
    for p, gg in zip(pre, ggs):
        mcat = jnp.where(strict2, gg[:t, :2 * t], 0.0)
        p['ncat'] = _bf(jnp.where(strict2, gg[:t, 2 * t:], 0.0))
        p['qcat'] = _bf(jnp.where(incl2, gg[t:, :2 * t], 0.0))
        p['pcat'] = _bf(jnp.where(incl2, gg[t:, 2 * t:], 0.0))
        p['x0'] = -jnp.concatenate([mcat * cm0, mcat * cm1], axis=0)
    n_dbl = int(math.log2(t))
    ws = [eye + p['x0'] for p in pre]
    xs = [_dot(_bf(p['x0']), _bf(p['x0'])) for p in pre]
    nvs = [_dot(p['ncat'], p['vst']) for p in pre]
    for step in range(1, n_dbl):
        last = step == n_dbl - 1
        outs = [_dot(_bf(w if last else jnp.concatenate([w, x], axis=0)), _bf(x)) for w, x in zip(ws, xs)]
        ws = [w + o[:2 * t] for w, o in zip(ws, outs)]
        if not last:
            xs = [o[2 * t:] for o in outs]
    errs = [eye - _dot3(eye - p['x0'], w) for p, w in zip(pre, ws)]
    ws = [w + _dot(_bf(w), _bf(e)) for w, e in zip(ws, errs)]
    wcats = [_bf(w[:t] + w[t:]) for w in ws]
    uks = [_dot(wc, jnp.concatenate([stack2(nv), stack2(p['kt'].astype(F32))], axis=1))
           for wc, nv, p in zip(wcats, nvs, pre)]
    qus = [_dot(p['qcat'], jnp.concatenate([stack2(uk[:, :LANES]), stack2(uk[:, LANES:])], axis=1))
           for p, uk in zip(pre, uks)]
    pvs = [_dot(p['pcat'], p['vst']) for p in pre]
    zeros_t = jnp.zeros((t, LANES), F32)
    dcs = [_dot_tn(_bf(jnp.concatenate([jnp.concatenate([p['v'], -uk[:, :LANES]], axis=0),
                                        jnp.concatenate([zeros_t, uk[:, LANES:]], axis=0)], axis=1)), p['rhs_r'])
           for p, uk in zip(pre, uks)]
    for p, qu, pv, dc in zip(pre, qus, pvs, dcs):
        p['y0'] = pv - qu[:, :LANES]
        p['rp'] = _bf(p['rt'] - qu[:, LANES:])
        p['d0'] = bd * dc[:LANES]
        p['c0'] = _bf(bd * dc[LANES:])
    states = [s_scr[j] for j in range(nhp)]
    for i in range(nper):
        ps = [pre[i * nhp + j] for j in range(nhp)]
        sbs = [_bf(sbd) for sbd in states]
        ys = [_dot_nt(p['rp'], sb) for p, sb in zip(ps, sbs)]
        scs = [_dot(sb, p['c0']) for p, sb in zip(ps, sbs)]
        for j in range(nhp):
            y_ref[0, 0, rows_of(i), sl_of(j)] = _bf(ps[j]['y0'] + ys[j])
        states = [sbd * p['g_tot'] - sc + p['d0'] for sbd, p, sc in zip(states, ps, scs)]
    for j in range(nhp):
        s_scr[j] = states[j]


def _rwkv_scan(proj, batch, seq_len, rwp, s0):
    n = proj.shape[0]
    tb = min(RW_BLOCK, seq_len)
    nc = seq_len // tb
    nhp = RW_W // LANES
    cur, prev, nxt = _halo_specs(RW_BLK, C_RW // RW_BLK, seq_len, tb, nc, n)
    full2 = lambda d, bi, c: (0, 0)
    full3 = lambda d, bi, c: (0, 0, 0)
    dmap = lambda d, bi, c: (d, bi, c + d * (nc - 1 - 2 * c), 0)
    smap = lambda d, bi, c: (d, bi, 0, 0, 0)
    return pl.pallas_call(
        _rwkv_kernel,
        grid=(2, batch, nc),
        in_specs=[
            cur, prev, nxt,
            pl.BlockSpec((1, RW_BLK), full2),
            pl.BlockSpec((1, 1, RW_W), lambda d, bi, c: (d, 0, 0)),
            pl.BlockSpec((1, 2 * RW_DECAY_R, RW_W), lambda d, bi, c: (d, 0, 0)),
            pl.BlockSpec((2, RW_W), full2),
            pl.BlockSpec((2, 2 * RW_ICLR_R, RW_W), full3),
            pl.BlockSpec((RW_GATE_R, RW_W), full2),
            pl.BlockSpec((3, RW_W), full2),
            pl.BlockSpec((RW_W, RW_W), full2),
            pl.BlockSpec((1, 1, nhp, LANES, LANES), smap),
        ],
        out_specs=[
            pl.BlockSpec((1, 1, tb, RW_W), dmap),
            pl.BlockSpec((1, 1, tb, RW_W), dmap),
            pl.BlockSpec((1, 1, nhp, LANES, LANES), smap),
        ],
        out_shape=[
            jax.ShapeDtypeStruct((2, batch, seq_len, RW_W), BF16),
            jax.ShapeDtypeStruct((2, batch, seq_len, RW_W), BF16),
            jax.ShapeDtypeStruct((2, batch, nhp, LANES, LANES), F32),
        ],
        scratch_shapes=[pltpu.VMEM((nhp, LANES, LANES), F32)],
        compiler_params=_cparams(("arbitrary", "arbitrary", "arbitrary")),
        name="rwkv_scan",
    )(proj, proj, proj, rwp['mu'], rwp['w0'], rwp['w2'], rwp['a0'], rwp['a2'], rwp['g2'], rwp['vec'],
      rwp['seg'], s0)


def _ssd_kernel(x_ref, xp_ref, xn_ref, dt_ref, cw_ref, cb_ref, dtb_ref, alog_ref, dskip_ref, s0_ref,
                y_ref, sf_ref, s_scr):
    d = pl.program_id(0)
    c = pl.program_id(2)
    nc = pl.num_programs(2)

    @pl.when(c == 0)
    def _():
        s_scr[...] = s0_ref[0, 0]

    refs = (x_ref, xp_ref, xn_ref, dt_ref, cw_ref, cb_ref, dtb_ref, alog_ref, dskip_ref, y_ref, s_scr)

    @pl.when(d == 0)
    def _():
        _ssd_step(True, c, nc, *refs)

    @pl.when(d == 1)
    def _():
        _ssd_step(False, nc - 1 - c, nc, *refs)

    @pl.when(c == nc - 1)
    def _():
        sf_ref[0, 0] = s_scr[...]


def _ssd_step(fwd, cc, nc, x_ref, xp_ref, xn_ref, dt_ref, cw_ref, cb_ref, dtb_ref, alog_ref, dskip_ref,
              y_ref, s_scr):
    t = SSD_CHUNK
    gw = SSM_W // SSM_GROUPS
    hpg = SSM_HEADS // SSM_GROUPS
    nh = SSM_HEADS

    x = x_ref[...].astype(F32)
    prev = xp_ref[HALO_ROWS - 1:HALO_ROWS, :].astype(F32) * (cc > 0).astype(F32)
    nxt = xn_ref[0:1, :].astype(F32) * (cc < nc - 1).astype(F32)
    x_m1, x_p1 = _neighbour_rows(x, prev, nxt)
    xc = x_m1 * cw_ref[0:1, :] + x * cw_ref[1:2, :] + x_p1 * cw_ref[2:3, :] + cb_ref[...]
    xc = xc * _sigmoid(xc)
    xs = xc[:, :SSM_W]
    bm = xc[:, SSM_W:SSM_W + SSM_GROUPS * SSM_STATE]
    cm = xc[:, SSM_W + SSM_GROUPS * SSM_STATE:]
    dt2 = _softplus(dt_ref[...].astype(F32)[:, :2 * nh] + dtb_ref[...])
    a2 = dt2 * (-jnp.exp(alog_ref[...]))
    dt_all = dt2[:, :nh] if fwd else dt2[:, nh:]
    a_all = a2[:, :nh] if fwd else a2[:, nh:]

    sign = 1 if fwd else -1
    row = lax.broadcasted_iota(jnp.int32, (t, t), 0)
    col = lax.broadcasted_iota(jnp.int32, (t, t), 1)
    delta = (row - col) * sign
    incl = delta >= 0
    tri_b = incl.astype(BF16)
    tri_t_b = (delta <= 0).astype(BF16)
    lane = lax.broadcasted_iota(jnp.int32, (1, LANES), 1)
    m0 = (lane < SSM_HD).astype(F32)
    m1 = 1.0 - m0
    lane_g = lax.broadcasted_iota(jnp.int32, (1, gw), 1) // SSM_HD

    def expand(z, g):
        out = jnp.zeros((z.shape[0], gw), F32)
        for hh in range(hpg):
            h = g * hpg + hh
            out = jnp.where(lane_g == hh, z[:, h:h + 1], out)
        return out

    nper = x_ref.shape[0] // t
    states = [s_scr[g] for g in range(SSM_GROUPS)]
    for i in range(nper):
        ip = i if fwd else nper - 1 - i
        rows = slice(ip * t, (ip + 1) * t)
        dt = dt_all[rows]
        a = a_all[rows]
        a3 = jnp.concatenate(_split3(a), axis=1)
        cs3 = _dot(tri_b, a3)
        cs = cs3[:, :nh] + cs3[:, nh:2 * nh] + cs3[:, 2 * nh:]
        cs3_t = _dot_tn(a3, tri_t_b)
        cs_t = cs3_t[:nh] + cs3_t[nh:2 * nh] + cs3_t[2 * nh:]
        tot = jnp.sum(a, axis=0, keepdims=True)
        e_cs = jnp.exp(cs)
        e_rem = jnp.exp(tot - cs)
        e_tot = jnp.exp(tot)
        for g in range(SSM_GROUPS):
            cg = cm[rows, g * SSM_STATE:(g + 1) * SSM_STATE]
            bg = bm[rows, g * SSM_STATE:(g + 1) * SSM_STATE]
            xsg = xs[rows, g * gw:(g + 1) * gw]
            xg = xsg * expand(dt, g)
            cgb = _bf(cg)
            bgb = _bf(bg)
            gm = _dot_nt(cgb, bgb)
            st = states[g]
            y_off = _dot(cgb, _bf(st)) * expand(e_cs, g)
            ys = []
            for hp in range(hpg // 2):
                scs = []
                for hh in range(2):
                    h = g * hpg + 2 * hp + hh
                    diff = jnp.where(incl, cs[:, h:h + 1] - cs_t[h:h + 1, :], 0.0)
                    scs.append(gm * jnp.where(incl, jnp.exp(diff), 0.0))
                xp = xg[:, hp * LANES:(hp + 1) * LANES]
                xst = _bf(jnp.concatenate([xp * m0, xp * m1], axis=0))
                ys.append(_dot(_bf(jnp.concatenate(scs, axis=1)), xst))
            y = jnp.concatenate(ys, axis=1) + y_off
            if fwd:
                y = y + dskip_ref[:, g * gw:(g + 1) * gw] * xsg
            y_ref[0, 0, rows, g * gw:(g + 1) * gw] = _bf(y)
            states[g] = st * expand(e_tot, g) + _dot_tn(bgb, _bf(xg * expand(e_rem, g)))
    for g in range(SSM_GROUPS):
        s_scr[g] = states[g]


def _ssd_scan(proj, batch, seq_len, ssp, s0):
    n = proj.shape[0]
    t = min(SSD_BLOCK, seq_len)
    nc = seq_len // t
    gw = SSM_W // SSM_GROUPS
    cur, prev, nxt = _halo_specs(SSM_XBC, C_XBC // SSM_XBC, seq_len, t, nc, n)
    dtspec, _, _ = _halo_specs(LANES, C_DT // LANES, seq_len, t, nc, n)
    full2 = lambda d, bi, c: (0, 0)
    dmap = lambda d, bi, c: (d, bi, c + d * (nc - 1 - 2 * c), 0)
    smap = lambda d, bi, c: (d, bi, 0, 0, 0)
    return pl.pallas_call(
        _ssd_kernel,
        grid=(2, batch, nc),
        in_specs=[
            cur, prev, nxt, dtspec,
            pl.BlockSpec((3, SSM_XBC), full2),
            pl.BlockSpec((1, SSM_XBC), full2),
            pl.BlockSpec((1, 2 * SSM_HEADS), full2),
            pl.BlockSpec((1, 2 * SSM_HEADS), full2),
            pl.BlockSpec((1, SSM_W), full2),
            pl.BlockSpec((1, 1, SSM_GROUPS, SSM_STATE, gw), smap),
        ],
        out_specs=[
            pl.BlockSpec((1, 1, t, SSM_W), dmap),
            pl.BlockSpec((1, 1, SSM_GROUPS, SSM_STATE, gw), smap),
        ],
        out_shape=[
            jax.ShapeDtypeStruct((2, batch, seq_len, SSM_W), BF16),
            jax.ShapeDtypeStruct((2, batch, SSM_GROUPS, SSM_STATE, gw), F32),
        ],
        scratch_shapes=[pltpu.VMEM((SSM_GROUPS, SSM_STATE, gw), F32)],
        compiler_params=_cparams(("arbitrary", "arbitrary", "arbitrary")),
        name="ssd_scan",
    )(proj, proj, proj, proj, ssp['conv_w'], ssp['conv_b'], ssp['dt_bias'], ssp['a_log'], ssp['dskip'], s0)


def _merge_kernel(ya_ref, yr_ref, aux_ref, ys_ref, z_ref, ga_ref, gr_ref, gs_ref, x_ref, g1_ref,
                  seg_ref, rg_ref, rb_ref, sg_ref, pa_ref, pr_ref, ps_ref, wo_ref, lg_ref, lb_ref, o_ref):
    seg = seg_ref[...]
    yr = yr_ref[0].astype(F32) + yr_ref[1].astype(F32)
    mu = _seg_sum(yr, seg) * (1.0 / RW_HD)
    dv = yr - mu
    var = _seg_sum(dv * dv, seg) * (1.0 / RW_HD)
    yr = (dv * lax.rsqrt(var + RW_GN_EPS) * rg_ref[...] + rb_ref[...] + aux_ref[0].astype(F32)) * aux_ref[1].astype(F32)
    z = z_ref[...].astype(F32)
    ys = (ys_ref[0].astype(F32) + ys_ref[1].astype(F32)) * (z * _sigmoid(z))
    gw = SSM_W // SSM_GROUPS
    parts = []
    for g in range(SSM_GROUPS):
        yg = ys[:, g * gw:(g + 1) * gw]
        parts.append(yg * lax.rsqrt(jnp.mean(yg * yg, -1, keepdims=True) + NORM_EPS))
    ys = jnp.concatenate(parts, axis=1) * sg_ref[...]
    m = (_sigmoid(ga_ref[...].astype(F32)) * _dot(ya_ref[...], pa_ref[...])
         + _sigmoid(gr_ref[...].astype(F32)) * _dot(_bf(yr), pr_ref[...])
         + _sigmoid(gs_ref[...].astype(F32)) * _dot(_bf(ys), ps_ref[...]))
    out = _dot(_bf(m), wo_ref[...])
    o_ref[...] = _layernorm_rows(ALPHA * x_ref[...] + g1_ref[0] * out, lg_ref[...], lb_ref[...])


def _merge(ya, yr2, aux, ys2, proj, x2d, g1, lp, seq_len):
    n = x2d.shape[0]
    tm = min(512, seq_len)
    per_b = seq_len // tm
    row = lambda i: (i, 0)
    row3 = lambda i: (0, i, 0)
    full = lambda i: (0, 0)
    gate = lambda col: pl.BlockSpec((tm, D_MODEL), lambda i: (i, col // D_MODEL))
    vec512 = pl.BlockSpec((1, RW_W), full)
    vec = pl.BlockSpec((1, D_MODEL), full)
    return pl.pallas_call(
        _merge_kernel,
        grid=(n // tm,),
        in_specs=[
            pl.BlockSpec((tm, DA_W), row),
            pl.BlockSpec((2, tm, RW_W), row3), pl.BlockSpec((2, tm, RW_W), row3),
            pl.BlockSpec((2, tm, SSM_W), row3),
            pl.BlockSpec((tm, SSM_W), lambda i: (i, C_Z // SSM_W)),
            gate(C_GA), gate(C_GR), gate(C_GS),
            pl.BlockSpec((tm, D_MODEL), row),
            pl.BlockSpec((1, 1, D_MODEL), lambda i: (i // per_b, 0, 0)),
            pl.BlockSpec((RW_W, RW_W), full), vec512, vec512, vec512,
            pl.BlockSpec((DA_W, D_MODEL), full), pl.BlockSpec((RW_W, D_MODEL), full),
            pl.BlockSpec((SSM_W, D_MODEL), full), pl.BlockSpec((D_MODEL, D_MODEL), full),
            vec, vec,
        ],
        out_specs=pl.BlockSpec((tm, D_MODEL), row),
        out_shape=jax.ShapeDtypeStruct((n, D_MODEL), F32),
        compiler_params=_cparams(("arbitrary",)),
        name="merge_ln",
    )(ya, yr2, aux, ys2, proj, proj, proj, proj, x2d, g1, lp['rw']['seg'], lp['rw_ln_g'], lp['rw_ln_b'],
      lp['ssm_norm_g'], lp['p_attn'], lp['p_rwkv'], lp['p_ssm'], lp['w_out'], lp['ln1_g'], lp['ln1_b'])


def _ffn_kernel(x_ref, sc_ref, sh_ref, g2_ref, w1_ref, w3_ref, w2_ref, lg_ref, lb_ref, o_ref, h_scr, acc_scr):
    j = pl.program_id(1)

    @pl.when(j == 0)
    def _():
        h_scr[...] = (x_ref[...] * (1.0 + sc_ref[0]) + sh_ref[0]).astype(BF16)
        acc_scr[...] = jnp.zeros_like(acc_scr)

    h = h_scr[...]
    u = jax.nn.silu(_dot(h, w1_ref[...])) * _dot(h, w3_ref[...])
    acc_scr[...] += _dot(u.astype(BF16), w2_ref[...])

    @pl.when(j == pl.num_programs(1) - 1)
    def _():
        o_ref[...] = _layernorm_rows(ALPHA * x_ref[...] + g2_ref[0] * acc_scr[...], lg_ref[...], lb_ref[...])


def _ffn(x2d, sc, sh, g2, w1, w3, w2, lg, lb, seq_len):
    n = x2d.shape[0]
    tm = min(512, seq_len)
    th = FFN_HIDDEN // 2
    per_b = seq_len // tm
    mod = pl.BlockSpec((1, 1, D_MODEL), lambda i, j: (i // per_b, 0, 0))
    return pl.pallas_call(
        _ffn_kernel,
        grid=(n // tm, FFN_HIDDEN // th),
        in_specs=[
            pl.BlockSpec((tm, D_MODEL), lambda i, j: (i, 0)),
            mod, mod, mod,
            pl.BlockSpec((D_MODEL, th), lambda i, j: (0, j)),
            pl.BlockSpec((D_MODEL, th), lambda i, j: (0, j)),
            pl.BlockSpec((th, D_MODEL), lambda i, j: (j, 0)),
            pl.BlockSpec((1, D_MODEL), lambda i, j: (0, 0)),
            pl.BlockSpec((1, D_MODEL), lambda i, j: (0, 0)),
        ],
        out_specs=pl.BlockSpec((tm, D_MODEL), lambda i, j: (i, 0)),
        out_shape=jax.ShapeDtypeStruct((n, D_MODEL), F32),
        scratch_shapes=[pltpu.VMEM((tm, D_MODEL), BF16), pltpu.VMEM((tm, D_MODEL), F32)],
        compiler_params=_cparams(("arbitrary", "arbitrary")),
        name="ffn_ln",
    )(x2d, sc, sh, g2, w1, w3, w2, lg, lb)


def _rope_tables(n_tok):
    rows = n_tok // GRID_W
    row = jnp.repeat(jnp.arange(rows), GRID_W).astype(F32)
    col = jnp.tile(jnp.arange(GRID_W), rows).astype(F32)
    nf = DA_HD // 4
    inv = ROPE_BASE ** (-jnp.arange(nf, dtype=F32) / nf)
    ang = jnp.concatenate([row[:, None] * inv] * 2 + [col[:, None] * inv] * 2, axis=-1)
    cos = jnp.cos(ang)
    sin = jnp.sin(ang)
    first = (jnp.arange(DA_HD) % (2 * nf)) < nf
    s_dn = jnp.where(first, -sin, 0.0)
    s_up = jnp.where(first, 0.0, sin)
    reps = LANES // DA_HD
    return jnp.tile(cos, (1, reps)), jnp.tile(s_dn, (1, reps)), jnp.tile(s_up, (1, reps))


def _permute_w_in(w_in):
    parts, acc = [], 0
    for s in W_IN_SPLIT:
        parts.append(w_in[:, acc:acc + s])
        acc += s
    q, k, v, rw, z, xbc, dt, ga, gr, gs = parts
    pad = jnp.zeros((w_in.shape[0], RW_BLK - RW_COLS - dt.shape[1]), w_in.dtype)
    return jnp.concatenate([rw, dt, pad, ga, gr, gs, q, k, v, z, xbc], axis=1).astype(BF16)


def _pad_lora(w, rank):
    z = jnp.zeros_like(w[0])
    return jnp.stack([jnp.concatenate([w[0], z], axis=0), jnp.concatenate([z, w[1]], axis=0)]).astype(BF16)


def _mixer(x, mod, lp, layer, side, want_out, rope_tabs):
    b, l, _ = x.shape
    sh1, sc1, g1 = mod
    n = b * l
    x2d = x.reshape(n, D_MODEL)
    proj = _inproj(x2d, sc1, sh1, lp['w_in_p'], l)
    if side is None:
        proj_ctx, ctx_len = None, 0
        s0_r = jnp.zeros((2, b, RW_W // LANES, LANES, LANES), F32)
        s0_s = jnp.zeros((2, b, SSM_GROUPS, SSM_STATE, SSM_W // SSM_GROUPS), F32)
    else:
        (proj_ctx, ctx_len), s0_r, s0_s = side
    y_r2, aux, sf_r = _rwkv_scan(proj, b, l, lp['rw'], s0_r)
    y_s2, sf_s = _ssd_scan(proj, b, l, lp['ssm'], s0_s)
    new_side = ((proj, l), sf_r, sf_s)
    if not want_out:
        return None, new_side
    lam_init = 0.8 - 0.6 * math.exp(-0.3 * layer)
    lam = (jnp.exp(jnp.sum(lp['da_lq1'] * lp['da_lk1'])) - jnp.exp(jnp.sum(lp['da_lq2'] * lp['da_lk2']))
           + lam_init).reshape(1)
    y_a = _diff_attention(lam, proj, b, l, lp['da_norm_g'].reshape(1, 2 * DA_HD), 1.0 - lam_init,
                          proj_ctx, ctx_len, rope_tabs)
    x1 = _merge(y_a, y_r2.reshape(2, n, RW_W), aux.reshape(2, n, RW_W),
                y_s2.reshape(2, n, SSM_W), proj, x2d, g1, lp, l)
    return x1, new_side


def kernel(x, c, ctx, c_ctx, ada_w, ada_b, w_in, da_lq1, da_lk1, da_lq2, da_lk2, da_norm_g,
           rw_mu, rw_w0, rw_w2, rw_a0, rw_a2, rw_g2, rw_kk, rw_ka, rw_rk, rw_ln_g, rw_ln_b,
           ssm_conv_w, ssm_conv_b, ssm_dt_bias, ssm_a_log, ssm_d, ssm_norm_g,
           p_attn, p_rwkv, p_ssm, w_out, ln1_g, ln1_b, ffn_w1, ffn_w3, ffn_w2, ln2_g, ln2_b):
    b, l, _ = x.shape
    lc = ctx.shape[1]
    rope_tabs = _rope_tables(l)
    seg = (jnp.arange(RW_W)[:, None] // RW_HD == jnp.arange(RW_W)[None, :] // RW_HD).astype(BF16)
    xc = ctx
    for i in range(DEPTH):
        rw = {
            'mu': jnp.pad(rw_mu[i], (0, RW_BLK - RW_COLS)).reshape(1, RW_BLK),
            'w0': rw_w0[i].reshape(2, 1, RW_W),
            'w2': _pad_lora(rw_w2[i], RW_DECAY_R),
            'a0': rw_a0[i],
            'a2': _pad_lora(rw_a2[i], RW_ICLR_R),
            'g2': rw_g2[i].astype(BF16),
            'vec': jnp.stack([rw_kk[i], rw_ka[i], rw_rk[i].reshape(RW_W)]),
            'seg': seg,
        }
        ssm = {
            'conv_w': ssm_conv_w[i], 'conv_b': ssm_conv_b[i].reshape(1, SSM_XBC),
            'dt_bias': ssm_dt_bias[i].reshape(1, 2 * SSM_HEADS), 'a_log': ssm_a_log[i].reshape(1, 2 * SSM_HEADS),
            'dskip': jnp.repeat(ssm_d[i], SSM_HD).reshape(1, SSM_W),
        }
        lp = {
            'w_in_p': _permute_w_in(w_in[i]), 'rw': rw, 'ssm': ssm,
            'da_lq1': da_lq1[i], 'da_lk1': da_lk1[i], 'da_lq2': da_lq2[i], 'da_lk2': da_lk2[i],
            'da_norm_g': da_norm_g[i],
            'rw_ln_g': rw_ln_g[i].reshape(1, RW_W), 'rw_ln_b': rw_ln_b[i].reshape(1, RW_W),
            'ssm_norm_g': ssm_norm_g[i].reshape(1, SSM_W),
            'p_attn': p_attn[i].astype(BF16), 'p_rwkv': p_rwkv[i].astype(BF16),
            'p_ssm': p_ssm[i].astype(BF16), 'w_out': w_out[i].astype(BF16),
            'ln1_g': ln1_g[i].reshape(1, D_MODEL), 'ln1_b': ln1_b[i].reshape(1, D_MODEL),
        }
        w1, w3, w2 = ffn_w1[i].astype(BF16), ffn_w3[i].astype(BF16), ffn_w2[i].astype(BF16)
        l2g, l2b = ln2_g[i].reshape(1, D_MODEL), ln2_b[i].reshape(1, D_MODEL)
        last = i == DEPTH - 1
        mod_x = (jax.nn.silu(c) @ ada_w[i] + ada_b[i])[:, None, :]
        mod_c = jnp.broadcast_to((jax.nn.silu(c_ctx) @ ada_w[i] + ada_b[i])[None, None, :], (b, 1, 6 * D_MODEL))
        sh1, sc1, g1, sh2, sc2, g2 = jnp.split(mod_x, 6, axis=-1)
        csh1, csc1, cg1, csh2, csc2, cg2 = jnp.split(mod_c, 6, axis=-1)
        xc1, side = _mixer(xc, (csh1, csc1, cg1), lp, i, None, not last, None)
        x1, _ = _mixer(x, (sh1, sc1, g1), lp, i, side, True, rope_tabs)
        x = _ffn(x1, sc2, sh2, g2, w1, w3, w2, l2g, l2b, l).reshape(b, l, D_MODEL)
        if not last:
            xc = _ffn(xc1, csc2, csh2, cg2, w1, w3, w2, l2g, l2b, lc).reshape(b, lc, D_MODEL)
    return x
```

```python
import functools
import math

import jax
import jax.numpy as jnp
from jax import lax
from jax.experimental import pallas as pl
from jax.experimental.pallas import tpu as pltpu

F32 = jnp.float32
BF16 = jnp.bfloat16

D_MODEL = 1024
DEPTH = 4
GRID_W = 64
DA_HEADS = 4
DA_HD = 64
DA_W = DA_HEADS * 2 * DA_HD
ROPE_BASE = 10000.0
RW_HEADS = 8
RW_HD = 64
RW_W = RW_HEADS * RW_HD
RW_DECAY_R = 64
RW_ICLR_R = 64
RW_GATE_R = 128
RW_GN_EPS = 64e-5
RW_COLS = 3 * RW_W + 2 * RW_DECAY_R + 2 * RW_ICLR_R + RW_GATE_R
SSM_HEADS = 8
SSM_HD = 64
SSM_W = SSM_HEADS * SSM_HD
SSM_GROUPS = 2
SSM_STATE = 128
SSM_XBC = SSM_W + 2 * SSM_GROUPS * SSM_STATE
FFN_HIDDEN = (8 * D_MODEL + 3 * 256 - 1) // (3 * 256) * 256
NORM_EPS = 1e-5
W_IN_SPLIT = (DA_W, DA_W, DA_W, RW_COLS, SSM_W, SSM_XBC, 2 * SSM_HEADS, D_MODEL, D_MODEL, D_MODEL)
ALPHA = (2.0 * DEPTH) ** 0.25

LANES = 128
SUBLANES = 8
HALO_ROWS = 16
VMEM_LIMIT_BYTES = 56 * 1024 * 1024

PROJ_COLS = 8192
RW_BLK = 2048
C_RW = 0
C_DT = RW_COLS
C_GA, C_GR, C_GS = RW_BLK, RW_BLK + D_MODEL, RW_BLK + 2 * D_MODEL
C_Q = RW_BLK + 3 * D_MODEL
C_K = C_Q + DA_W
C_V = C_K + DA_W
C_Z = C_V + DA_W
C_XBC = C_Z + SSM_W

Q_SCALE = math.log2(math.e) * DA_HD ** -0.5
ATTN_SUBTILE = 128
ATTN_BLOCK = 512
RW_CHUNK = 64
RW_BLOCK = 256
SSD_CHUNK = 128
SSD_BLOCK = 256


def _cparams(sem):
    return pltpu.CompilerParams(dimension_semantics=sem, vmem_limit_bytes=VMEM_LIMIT_BYTES)


def _dot(a, b):
    return jnp.dot(a, b, preferred_element_type=F32)


def _dot_nt(a, b):
    return lax.dot_general(a, b, (((1,), (1,)), ((), ())), preferred_element_type=F32)


def _dot_tn(a, b):
    return lax.dot_general(a, b, (((0,), (0,)), ((), ())), preferred_element_type=F32)


def _bf(x):
    return x.astype(BF16)


def _split3(x):
    hi = _bf(x)
    r1 = x - hi.astype(F32)
    mid = _bf(r1)
    lo = _bf(r1 - mid.astype(F32))
    return hi, mid, lo


def _cumsum_rows(tri_b, x):
    n = x.shape[1]
    hi, mid, lo = _split3(x)
    o = _dot(tri_b, jnp.concatenate([hi, mid, lo], axis=1))
    return o[:, :n] + o[:, n:2 * n] + o[:, 2 * n:]


def _seg_sum(x, seg_b):
    m = x.shape[0]
    hi = _bf(x)
    lo = _bf(x - hi.astype(F32))
    o = _dot(jnp.concatenate([hi, lo], axis=0), seg_b)
    return o[:m] + o[m:]


def _sigmoid(x):
    return 1.0 / (1.0 + jnp.exp(-x))


def _softplus(x):
    return jnp.maximum(x, 0.0) + jnp.log(1.0 + jnp.exp(-jnp.abs(x)))


def _layernorm_rows(t, g, b):
    mu = jnp.mean(t, -1, keepdims=True)
    d = t - mu
    var = jnp.mean(d * d, -1, keepdims=True)
    return d * lax.rsqrt(var + NORM_EPS) * g + b


def _neighbour_rows(x, prev_row, next_row):
    t = x.shape[0]
    rowi = lax.broadcasted_iota(jnp.int32, (t, 1), 0)
    x_m1 = jnp.where(rowi == 0, prev_row, pltpu.roll(x, 1, 0))
    x_p1 = jnp.where(rowi == t - 1, next_row, pltpu.roll(x, t - 1, 0))
    return x_m1, x_p1


def _halo_specs(width, col_blk, seq_len, tb, nc, n_rows):
    per8 = tb // HALO_ROWS
    last8 = n_rows // HALO_ROWS - 1

    def blk(d, bi, c):
        return bi * nc + c + d * (nc - 1 - 2 * c)

    prev = pl.BlockSpec((HALO_ROWS, width), lambda d, bi, c: (jnp.maximum(blk(d, bi, c) * per8 - 1, 0), col_blk))
    nxt = pl.BlockSpec((HALO_ROWS, width),
                       lambda d, bi, c: (jnp.minimum((blk(d, bi, c) + 1) * per8, last8), col_blk))
    cur = pl.BlockSpec((tb, width), lambda d, bi, c: (blk(d, bi, c), col_blk))
    return cur, prev, nxt


def _inproj_kernel(x_ref, sc_ref, sh_ref, w_ref, o_ref, h_scr):
    @pl.when(pl.program_id(1) == 0)
    def _():
        h = x_ref[...] * (1.0 + sc_ref[0]) + sh_ref[0]
        h_scr[...] = h.astype(BF16)

    o_ref[...] = _bf(_dot(h_scr[...], w_ref[...]))


def _inproj(x2d, sc, sh, w, seq_len):
    n = x2d.shape[0]
    tm = min(1024, seq_len)
    tn = 2048
    per_b = seq_len // tm
    return pl.pallas_call(
        _inproj_kernel,
        grid=(n // tm, PROJ_COLS // tn),
        in_specs=[
            pl.BlockSpec((tm, D_MODEL), lambda i, j: (i, 0)),
            pl.BlockSpec((1, 1, D_MODEL), lambda i, j: (i // per_b, 0, 0)),
            pl.BlockSpec((1, 1, D_MODEL), lambda i, j: (i // per_b, 0, 0)),
            pl.BlockSpec((D_MODEL, tn), lambda i, j: (0, j)),
        ],
        out_specs=pl.BlockSpec((tm, tn), lambda i, j: (i, j)),
        out_shape=jax.ShapeDtypeStruct((n, PROJ_COLS), BF16),
        scratch_shapes=[pltpu.VMEM((tm, D_MODEL), BF16)],
        compiler_params=_cparams(("arbitrary", "arbitrary")),
        name="inproj",
    )(x2d, sc, sh, w)


def _rope_lanes(x, cos, s_dn, s_up):
    nf = DA_HD // 4
    return x * cos + pltpu.roll(x, LANES - nf, 1) * s_dn + pltpu.roll(x, nf, 1) * s_up


def _attn_core(lam, q, k_scr, v_scr, g_ref, o_ref, out_scale):
    tq = ATTN_SUBTILE
    nsub = q.shape[0] // tq
    lk = k_scr.shape[0]
    lane = lax.broadcasted_iota(jnp.int32, (1, LANES), 1)
    mid = (lk // 2) // LANES * LANES
    cuts = (0, mid, lk) if mid else (0, lk)
    spans = list(zip(cuts[:-1], cuts[1:]))
    scores = []
    for t in range(nsub):
        qt = q[t * tq:(t + 1) * tq]
        zero = jnp.zeros_like(qt)
        qq = jnp.concatenate([jnp.where(lane < DA_HD, qt, zero), jnp.where(lane >= DA_HD, qt, zero)], axis=0)
        scores.append([_dot_nt(qq, k_scr[lo:hi, :]) for lo, hi in spans])
    for t, ss in enumerate(scores):
        m = functools.reduce(jnp.maximum, [jnp.max(s, -1, keepdims=True) for s in ss])
        ov = functools.reduce(jnp.add, [_dot(_bf(jnp.exp2(s - m)), v_scr[lo:hi, :]) for s, (lo, hi) in zip(ss, spans)])
        on = ov[:, :LANES] / ov[:, LANES:]
        o = on[:tq] - lam * on[tq:]
        ms = jnp.mean(o * o, -1, keepdims=True)
        o_ref[t * tq:(t + 1) * tq, :] = _bf(o * lax.rsqrt(ms + NORM_EPS) * (g_ref[...] * out_scale))


def _attn_latent_kernel(lam_ref, q_ref, kl_ref, vl_ref, kc_ref, vc_ref, cos_ref, sdn_ref, sup_ref, g_ref,
                        o_ref, k_scr, v_scr, *, out_scale):
    i = pl.program_id(2)
    tq = q_ref.shape[0]
    lc = kc_ref.shape[0]

    @pl.when(i == 0)
    def _():
        k_scr[:lc, :] = kc_ref[...]
        k_scr[lc:, :] = _bf(_rope_lanes(kl_ref[...].astype(F32), cos_ref[...], sdn_ref[...], sup_ref[...]))
        v_scr[:lc, :LANES] = vc_ref[...]
        v_scr[lc:, :LANES] = vl_ref[...]
        v_scr[:, LANES:] = jnp.ones((v_scr.shape[0], LANES), BF16)

    rows = pl.ds(pl.multiple_of(i * tq, tq), tq)
    q = _rope_lanes(q_ref[...].astype(F32), cos_ref[rows, :], sdn_ref[rows, :], sup_ref[rows, :])
    _attn_core(lam_ref[0], _bf(q * Q_SCALE), k_scr, v_scr, g_ref, o_ref, out_scale)


def _attn_ctx_kernel(lam_ref, q_ref, kl_ref, vl_ref, g_ref, o_ref, k_scr, v_scr, *, out_scale):
    @pl.when(pl.program_id(2) == 0)
    def _():
        k_scr[...] = kl_ref[...]
        v_scr[:, :LANES] = vl_ref[...]
        v_scr[:, LANES:] = jnp.ones((v_scr.shape[0], LANES), BF16)

    _attn_core(lam_ref[0], _bf(q_ref[...].astype(F32) * Q_SCALE), k_scr, v_scr, g_ref, o_ref, out_scale)


def _diff_attention(lam, proj, batch, seq_len, g, out_scale, proj_ctx=None, ctx_len=0, rope_tabs=None):
    tq = min(ATTN_BLOCK, seq_len)
    nq = seq_len // tq
    lk = seq_len + ctx_len
    cq, ck, cv = C_Q // LANES, C_K // LANES, C_V // LANES
    smem = pl.BlockSpec(memory_space=pltpu.SMEM)
    qspec = pl.BlockSpec((tq, LANES), lambda bi, h, i: (bi * nq + i, cq + h))
    kspec = pl.BlockSpec((seq_len, LANES), lambda bi, h, i: (bi, ck + h))
    vspec = pl.BlockSpec((seq_len, LANES), lambda bi, h, i: (bi, cv + h))
    gspec = pl.BlockSpec((1, LANES), lambda bi, h, i: (0, 0))
    common = dict(
        grid=(batch, DA_HEADS, nq),
        out_specs=pl.BlockSpec((tq, LANES), lambda bi, h, i: (bi * nq + i, h)),
        out_shape=jax.ShapeDtypeStruct((batch * seq_len, DA_W), BF16),
        scratch_shapes=[pltpu.VMEM((lk, LANES), BF16), pltpu.VMEM((lk, 2 * LANES), BF16)],
        compiler_params=_cparams(("arbitrary", "arbitrary", "arbitrary")),
    )
    if proj_ctx is None:
        return pl.pallas_call(
            functools.partial(_attn_ctx_kernel, out_scale=out_scale),
            in_specs=[smem, qspec, kspec, vspec, gspec], name="diff_attn_ctx", **common,
        )(lam, proj, proj, proj, g)
    tab = pl.BlockSpec((seq_len, LANES), lambda bi, h, i: (0, 0))
    return pl.pallas_call(
        functools.partial(_attn_latent_kernel, out_scale=out_scale),
        in_specs=[smem, qspec, kspec, vspec,
                  pl.BlockSpec((ctx_len, LANES), lambda bi, h, i: (bi, ck + h)),
                  pl.BlockSpec((ctx_len, LANES), lambda bi, h, i: (bi, cv + h)),
                  tab, tab, tab, gspec],
        name="diff_attn", **common,
    )(lam, proj, proj, proj, proj_ctx, proj_ctx, *rope_tabs, g)


def _rwkv_kernel(u_ref, up_ref, un_ref, mu_ref, w0_ref, w2_ref, a0_ref, a2_ref, g2_ref, vec_ref, seg_ref,
                 s0_ref, y_ref, aux_ref, sf_ref, s_scr):
    d = pl.program_id(0)
    c = pl.program_id(2)
    nc = pl.num_programs(2)

    @pl.when(c == 0)
    def _():
        s_scr[...] = s0_ref[0, 0]

    refs = (u_ref, up_ref, un_ref, mu_ref, w0_ref, w2_ref, a0_ref, a2_ref, g2_ref, vec_ref, seg_ref,
            y_ref, aux_ref, s_scr)

    @pl.when(d == 0)
    def _():
        _rwkv_step(True, c, nc, *refs)

    @pl.when(d == 1)
    def _():
        _rwkv_step(False, nc - 1 - c, nc, *refs)

    @pl.when(c == nc - 1)
    def _():
        sf_ref[0, 0] = s_scr[...]


def _rwkv_step(fwd, cc, nc, u_ref, up_ref, un_ref, mu_ref, w0_ref, w2_ref, a0_ref, a2_ref, g2_ref, vec_ref,
               seg_ref, y_ref, aux_ref, s_scr):
    t = RW_CHUNK
    tb = u_ref.shape[0]
    nper = tb // t
    di = 0 if fwd else 1

    u = u_ref[:, :RW_COLS].astype(F32)
    prev = up_ref[HALO_ROWS - 1:HALO_ROWS, :RW_COLS].astype(F32) * (cc > 0).astype(F32)
    nxt = un_ref[0:1, :RW_COLS].astype(F32) * (cc < nc - 1).astype(F32)
    u_m1, u_p1 = _neighbour_rows(u, prev, nxt)
    u = u + mu_ref[:, :RW_COLS] * (0.5 * (u_m1 + u_p1) - u)
    r = r_all = u[:, :RW_W]
    k = u[:, RW_W:2 * RW_W]
    v = v_all = u[:, 2 * RW_W:3 * RW_W]
    o = 3 * RW_W
    wd = _bf(jnp.tanh(u[:, o:o + 2 * RW_DECAY_R]))
    o += 2 * RW_DECAY_R
    ad = _bf(u[:, o:o + 2 * RW_ICLR_R])
    o += 2 * RW_ICLR_R
    gd = u[:, o:o + RW_GATE_R]
    kk_gain, ka, rk = vec_ref[0:1, :], vec_ref[1:2, :], vec_ref[2:3, :]
    seg = seg_ref[...]
    lw_all = -math.exp(-0.5) * _sigmoid(w0_ref[0] + _dot(wd, w2_ref[0]))
    a_all = _sigmoid(a0_ref[di:di + 1, :] + _dot(ad, a2_ref[di]))
    kk_all = k * kk_gain
    kk_all = kk_all * lax.rsqrt(jnp.maximum(_seg_sum(kk_all * kk_all, seg), 1e-24))
    kd_all = k * (1.0 + (a_all - 1.0) * ka)
    if fwd:
        a_other = _sigmoid(a0_ref[1:2, :] + _dot(ad, a2_ref[1]))
        k_bonus = k * (1.0 + (0.5 * (a_all + a_other) - 1.0) * ka)
        aux_ref[0, 0] = _bf(_seg_sum(r * k_bonus * rk, seg) * v)
    else:
        aux_ref[0, 0] = _bf(_dot(_bf(_sigmoid(gd)), g2_ref[...]))

    sign = 1 if fwd else -1
    row = lax.broadcasted_iota(jnp.int32, (t, t), 0)
    col = lax.broadcasted_iota(jnp.int32, (t, t), 1)
    tri_b = ((row - col) * sign >= 0).astype(BF16)
    row2 = lax.broadcasted_iota(jnp.int32, (t, 2 * t), 0)
    col2 = lax.broadcasted_iota(jnp.int32, (t, 2 * t), 1)
    col2 = jnp.where(col2 >= t, col2 - t, col2)
    delta2 = (row2 - col2) * sign
    strict2 = delta2 > 0
    incl2 = delta2 >= 0
    colh = lax.broadcasted_iota(jnp.int32, (1, 2 * t), 1)
    cm0 = (colh < t).astype(F32)
    cm1 = 1.0 - cm0
    lane = lax.broadcasted_iota(jnp.int32, (1, LANES), 1)
    m0 = (lane < RW_HD).astype(F32)
    m1 = 1.0 - m0
    rb = lax.broadcasted_iota(jnp.int32, (LANES, LANES), 0) // RW_HD
    cb = lax.broadcasted_iota(jnp.int32, (LANES, LANES), 1) // RW_HD
    bd = (rb == cb).astype(F32)
    eye = (lax.broadcasted_iota(jnp.int32, (2 * t, 2 * t), 0)
           == lax.broadcasted_iota(jnp.int32, (2 * t, 2 * t), 1)).astype(F32)

    def stack2(x):
        return _bf(jnp.concatenate([x * m0, x * m1], axis=0))

    nhp = RW_W // LANES
    inst = [(j, i) for i in range(nper) for j in range(nhp)]

    def rows_of(i):
        ip = i if fwd else nper - 1 - i
        return slice(ip * t, (ip + 1) * t)

    def sl_of(j):
        return slice(j * LANES, (j + 1) * LANES)

    lws = [lw_all[rows_of(i), sl_of(j)] for j, i in inst]
    cums = [_cumsum_rows(tri_b, lw) for lw in lws]
    pre = []
    for (j, i), lw, cum in zip(inst, lws, cums):
        rows, sl = rows_of(i), sl_of(j)
        r = r_all[rows, sl]
        v = v_all[rows, sl]
        kk = kk_all[rows, sl]
        kd = kd_all[rows, sl]
        a = a_all[rows, sl]
        tot = jnp.sum(lw, axis=0, keepdims=True)
        g_inv = jnp.exp(-cum)
        g_rem = jnp.exp(tot - cum)
        rt = r * jnp.exp(cum)
        kt = _bf(kk * jnp.exp(cum - lw))
        b = kk * a
        bh = b * g_inv
        kh = kd * g_inv
        ystack = _bf(jnp.concatenate([bh * m0, bh * m1, kh * m0, kh * m1], axis=0))
        pre.append(dict(v=v, rt=rt, kt=kt, ystack=ystack, g_tot=jnp.exp(tot), vst=stack2(v),
                        rhs_r=_bf(jnp.concatenate([kd * g_rem, b * g_rem], axis=0))))
    ggs = [_dot_nt(jnp.concatenate([p['kt'], _bf(p['rt'])], axis=0), p['ystack']) for p in pre]
    for p, gg in zip(pre, ggs):
        mcat = jnp.where(strict2, gg[:t, :2 * t], 0.0)
        p['ncat'] = _bf(jnp.where(strict2, gg[:t, 2 * t:], 0.0))
        p['qcat'] = _bf(jnp.where(incl2, gg[t:, :2 * t], 0.0))
        p['pcat'] = _bf(jnp.where(incl2, gg[t:, 2 * t:], 0.0))
        p['m'] = jnp.concatenate([mcat * cm0, mcat * cm1], axis=0)
    ri = lax.broadcasted_iota(jnp.int32, (2 * t, 2 * t), 0)
    ci = lax.broadcasted_iota(jnp.int32, (2 * t, 2 * t), 1)
    same_head = (ri // t) == (ci // t)
    lo_i, hi_i = (ci, ri) if fwd else (ri, ci)

    def level_mask(sz):
        return (same_head & ((ri % t) // (2 * sz) == (ci % t) // (2 * sz))
                & ((hi_i % (2 * sz)) >= sz) & ((lo_i % (2 * sz)) < sz))

    nvs = [_dot(p['ncat'], p['vst']) for p in pre]
    ws = [eye - jnp.where(level_mask(1), p['m'], 0.0) for p in pre]
    sz = 2
    while sz < t:
        lm = level_mask(sz)
        zs = [_dot(_bf(w), _bf(jnp.where(lm, p['m'], 0.0))) for w, p in zip(ws, pre)]
        ws = [w - _dot(_bf(z), _bf(w)) for w, z in zip(ws, zs)]
        sz *= 2
    wcats = [_bf(w[:t] + w[t:]) for w in ws]
    uks = [_dot(wc, jnp.concatenate([stack2(nv), stack2(p['kt'].astype(F32))], axis=1))
           for wc, nv, p in zip(wcats, nvs, pre)]
    qus = [_dot(p['qcat'], jnp.concatenate([stack2(uk[:, :LANES]), stack2(uk[:, LANES:])], axis=1))
           for p, uk in zip(pre, uks)]
    pvs = [_dot(p['pcat'], p['vst']) for p in pre]
    zeros_t = jnp.zeros((t, LANES), F32)
    dcs = [_dot_tn(_bf(jnp.concatenate([jnp.concatenate([p['v'], -uk[:, :LANES]], axis=0),
                                        jnp.concatenate([zeros_t, uk[:, LANES:]], axis=0)], axis=1)), p['rhs_r'])
           for p, uk in zip(pre, uks)]
    for p, qu, pv, dc in zip(pre, qus, pvs, dcs):
        p['y0'] = pv - qu[:, :LANES]
        p['rp'] = _bf(p['rt'] - qu[:, LANES:])
        p['d0'] = bd * dc[:LANES]
        p['c0'] = _bf(bd * dc[LANES:])
    states = [s_scr[j] for j in range(nhp)]
    for i in range(nper):
        ps = [pre[i * nhp + j] for j in range(nhp)]
        sbs = [_bf(sbd) for sbd in states]
        ys = [_dot_nt(p['rp'], sb) for p, sb in zip(ps, sbs)]
        scs = [_dot(sb, p['c0']) for p, sb in zip(ps, sbs)]
        for j in range(nhp):
            y_ref[0, 0, rows_of(i), sl_of(j)] = _bf(ps[j]['y0'] + ys[j])
        states = [sbd * p['g_tot'] - sc + p['d0'] for sbd, p, sc in zip(states, ps, scs)]
    for j in range(nhp):
        s_scr[j] = states[j]


def _rwkv_scan(proj, batch, seq_len, rwp, s0):
    n = proj.shape[0]
    tb = min(RW_BLOCK, seq_len)
    nc = seq_len // tb
    nhp = RW_W // LANES
    cur, prev, nxt = _halo_specs(RW_BLK, C_RW // RW_BLK, seq_len, tb, nc, n)
    full2 = lambda d, bi, c: (0, 0)
    full3 = lambda d, bi, c: (0, 0, 0)
    dmap = lambda d, bi, c: (d, bi, c + d * (nc - 1 - 2 * c), 0)
    smap = lambda d, bi, c: (d, bi, 0, 0, 0)
    return pl.pallas_call(
        _rwkv_kernel,
        grid=(2, batch, nc),
        in_specs=[
            cur, prev, nxt,
            pl.BlockSpec((1, RW_BLK), full2),
            pl.BlockSpec((1, 1, RW_W), lambda d, bi, c: (d, 0, 0)),
            pl.BlockSpec((1, 2 * RW_DECAY_R, RW_W), lambda d, bi, c: (d, 0, 0)),
            pl.BlockSpec((2, RW_W), full2),
            pl.BlockSpec((2, 2 * RW_ICLR_R, RW_W), full3),
            pl.BlockSpec((RW_GATE_R, RW_W), full2),
            pl.BlockSpec((3, RW_W), full2),
            pl.BlockSpec((RW_W, RW_W), full2),
            pl.BlockSpec((1, 1, nhp, LANES, LANES), smap),
        ],
        out_specs=[
            pl.BlockSpec((1, 1, tb, RW_W), dmap),
            pl.BlockSpec((1, 1, tb, RW_W), dmap),
            pl.BlockSpec((1, 1, nhp, LANES, LANES), smap),
        ],
        out_shape=[
            jax.ShapeDtypeStruct((2, batch, seq_len, RW_W), BF16),
            jax.ShapeDtypeStruct((2, batch, seq_len, RW_W), BF16),
            jax.ShapeDtypeStruct((2, batch, nhp, LANES, LANES), F32),
        ],
        scratch_shapes=[pltpu.VMEM((nhp, LANES, LANES), F32)],
        compiler_params=_cparams(("arbitrary", "arbitrary", "arbitrary")),
        name="rwkv_scan",
    )(proj, proj, proj, rwp['mu'], rwp['w0'], rwp['w2'], rwp['a0'], rwp['a2'], rwp['g2'], rwp['vec'],
      rwp['seg'], s0)


def _ssd_kernel(x_ref, xp_ref, xn_ref, dt_ref, cw_ref, cb_ref, dtb_ref, alog_ref, dskip_ref, s0_ref,
                y_ref, sf_ref, s_scr):
    d = pl.program_id(0)
    c = pl.program_id(2)
    nc = pl.num_programs(2)

    @pl.when(c == 0)
    def _():
        s_scr[...] = s0_ref[0, 0]

    refs = (x_ref, xp_ref, xn_ref, dt_ref, cw_ref, cb_ref, dtb_ref, alog_ref, dskip_ref, y_ref, s_scr)

    @pl.when(d == 0)
    def _():
        _ssd_step(True, c, nc, *refs)

    @pl.when(d == 1)
    def _():
        _ssd_step(False, nc - 1 - c, nc, *refs)

    @pl.when(c == nc - 1)
    def _():
        sf_ref[0, 0] = s_scr[...]


def _ssd_step(fwd, cc, nc, x_ref, xp_ref, xn_ref, dt_ref, cw_ref, cb_ref, dtb_ref, alog_ref, dskip_ref,
              y_ref, s_scr):
    t = SSD_CHUNK
    gw = SSM_W // SSM_GROUPS
    hpg = SSM_HEADS // SSM_GROUPS
    nh = SSM_HEADS

    x = x_ref[...].astype(F32)
    prev = xp_ref[HALO_ROWS - 1:HALO_ROWS, :].astype(F32) * (cc > 0).astype(F32)
    nxt = xn_ref[0:1, :].astype(F32) * (cc < nc - 1).astype(F32)
    x_m1, x_p1 = _neighbour_rows(x, prev, nxt)
    xc = x_m1 * cw_ref[0:1, :] + x * cw_ref[1:2, :] + x_p1 * cw_ref[2:3, :] + cb_ref[...]
    xc = xc * _sigmoid(xc)
    xs = xc[:, :SSM_W]
    bm = xc[:, SSM_W:SSM_W + SSM_GROUPS * SSM_STATE]
    cm = xc[:, SSM_W + SSM_GROUPS * SSM_STATE:]
    dt2 = _softplus(dt_ref[...].astype(F32)[:, :2 * nh] + dtb_ref[...])
    a2 = dt2 * (-jnp.exp(alog_ref[...]))
    dt_all = dt2[:, :nh] if fwd else dt2[:, nh:]
    a_all = a2[:, :nh] if fwd else a2[:, nh:]

    sign = 1 if fwd else -1
    row = lax.broadcasted_iota(jnp.int32, (t, t), 0)
    col = lax.broadcasted_iota(jnp.int32, (t, t), 1)
    delta = (row - col) * sign
    incl = delta >= 0
    tri_b = incl.astype(BF16)
    tri_t_b = (delta <= 0).astype(BF16)
    lane = lax.broadcasted_iota(jnp.int32, (1, LANES), 1)
    m0 = (lane < SSM_HD).astype(F32)
    m1 = 1.0 - m0
    lane_g = lax.broadcasted_iota(jnp.int32, (1, gw), 1) // SSM_HD

    def expand(z, g):
        out = jnp.zeros((z.shape[0], gw), F32)
        for hh in range(hpg):
            h = g * hpg + hh
            out = jnp.where(lane_g == hh, z[:, h:h + 1], out)
        return out

    nper = x_ref.shape[0] // t
    states = [s_scr[g] for g in range(SSM_GROUPS)]
    for i in range(nper):
        ip = i if fwd else nper - 1 - i
        rows = slice(ip * t, (ip + 1) * t)
        dt = dt_all[rows]
        a = a_all[rows]
        a3 = jnp.concatenate(_split3(a), axis=1)
        cs3 = _dot(tri_b, a3)
        cs = cs3[:, :nh] + cs3[:, nh:2 * nh] + cs3[:, 2 * nh:]
        cs3_t = _dot_tn(a3, tri_t_b)
        cs_t = cs3_t[:nh] + cs3_t[nh:2 * nh] + cs3_t[2 * nh:]
        tot = jnp.sum(a, axis=0, keepdims=True)
        e_cs = jnp.exp(cs)
        e_rem = jnp.exp(tot - cs)
        e_tot = jnp.exp(tot)
        for g in range(SSM_GROUPS):
            cg = cm[rows, g * SSM_STATE:(g + 1) * SSM_STATE]
            bg = bm[rows, g * SSM_STATE:(g + 1) * SSM_STATE]
            xsg = xs[rows, g * gw:(g + 1) * gw]
            xg = xsg * expand(dt, g)
            cgb = _bf(cg)
            bgb = _bf(bg)
            gm = _dot_nt(cgb, bgb)
            st = states[g]
            y_off = _dot(cgb, _bf(st)) * expand(e_cs, g)
            ys = []
            for hp in range(hpg // 2):
                scs = []
                for hh in range(2):
                    h = g * hpg + 2 * hp + hh
                    diff = jnp.where(incl, cs[:, h:h + 1] - cs_t[h:h + 1, :], 0.0)
                    scs.append(gm * jnp.where(incl, jnp.exp(diff), 0.0))
                xp = xg[:, hp * LANES:(hp + 1) * LANES]
                xst = _bf(jnp.concatenate([xp * m0, xp * m1], axis=0))
                ys.append(_dot(_bf(jnp.concatenate(scs, axis=1)), xst))
            y = jnp.concatenate(ys, axis=1) + y_off
            if fwd:
                y = y + dskip_ref[:, g * gw:(g + 1) * gw] * xsg
            y_ref[0, 0, rows, g * gw:(g + 1) * gw] = _bf(y)
            states[g] = st * expand(e_tot, g) + _dot_tn(bgb, _bf(xg * expand(e_rem, g)))
    for g in range(SSM_GROUPS):
        s_scr[g] = states[g]


def _ssd_scan(proj, batch, seq_len, ssp, s0):
    n = proj.shape[0]
    t = min(SSD_BLOCK, seq_len)
    nc = seq_len // t
    gw = SSM_W // SSM_GROUPS
    cur, prev, nxt = _halo_specs(SSM_XBC, C_XBC // SSM_XBC, seq_len, t, nc, n)
    dtspec, _, _ = _halo_specs(LANES, C_DT // LANES, seq_len, t, nc, n)
    full2 = lambda d, bi, c: (0, 0)
    dmap = lambda d, bi, c: (d, bi, c + d * (nc - 1 - 2 * c), 0)
    smap = lambda d, bi, c: (d, bi, 0, 0, 0)
    return pl.pallas_call(
        _ssd_kernel,
        grid=(2, batch, nc),
        in_specs=[
            cur, prev, nxt, dtspec,
            pl.BlockSpec((3, SSM_XBC), full2),
            pl.BlockSpec((1, SSM_XBC), full2),
            pl.BlockSpec((1, 2 * SSM_HEADS), full2),
            pl.BlockSpec((1, 2 * SSM_HEADS), full2),
            pl.BlockSpec((1, SSM_W), full2),
            pl.BlockSpec((1, 1, SSM_GROUPS, SSM_STATE, gw), smap),
        ],
        out_specs=[
            pl.BlockSpec((1, 1, t, SSM_W), dmap),
            pl.BlockSpec((1, 1, SSM_GROUPS, SSM_STATE, gw), smap),
        ],
        out_shape=[
            jax.ShapeDtypeStruct((2, batch, seq_len, SSM_W), BF16),
            jax.ShapeDtypeStruct((2, batch, SSM_GROUPS, SSM_STATE, gw), F32),
        ],
        scratch_shapes=[pltpu.VMEM((SSM_GROUPS, SSM_STATE, gw), F32)],
        compiler_params=_cparams(("arbitrary", "arbitrary", "arbitrary")),
        name="ssd_scan",
    )(proj, proj, proj, proj, ssp['conv_w'], ssp['conv_b'], ssp['dt_bias'], ssp['a_log'], ssp['dskip'], s0)


def _merge_kernel(ya_ref, yr_ref, aux_ref, ys_ref, z_ref, ga_ref, gr_ref, gs_ref, x_ref, g1_ref,
                  seg_ref, rg_ref, rb_ref, sg_ref, pa_ref, pr_ref, ps_ref, wo_ref, lg_ref, lb_ref, o_ref):
    seg = seg_ref[...]
    yr = yr_ref[0].astype(F32) + yr_ref[1].astype(F32)
    mu = _seg_sum(yr, seg) * (1.0 / RW_HD)
    dv = yr - mu
    var = _seg_sum(dv * dv, seg) * (1.0 / RW_HD)
    yr = (dv * lax.rsqrt(var + RW_GN_EPS) * rg_ref[...] + rb_ref[...] + aux_ref[0].astype(F32)) * aux_ref[1].astype(F32)
    z = z_ref[...].astype(F32)
    ys = (ys_ref[0].astype(F32) + ys_ref[1].astype(F32)) * (z * _sigmoid(z))
    gw = SSM_W // SSM_GROUPS
    parts = []
    for g in range(SSM_GROUPS):
        yg = ys[:, g * gw:(g + 1) * gw]
        parts.append(yg * lax.rsqrt(jnp.mean(yg * yg, -1, keepdims=True) + NORM_EPS))
    ys = jnp.concatenate(parts, axis=1) * sg_ref[...]
    m = (_sigmoid(ga_ref[...].astype(F32)) * _dot(ya_ref[...], pa_ref[...])
         + _sigmoid(gr_ref[...].astype(F32)) * _dot(_bf(yr), pr_ref[...])
         + _sigmoid(gs_ref[...].astype(F32)) * _dot(_bf(ys), ps_ref[...]))
    out = _dot(_bf(m), wo_ref[...])
    o_ref[...] = _layernorm_rows(ALPHA * x_ref[...] + g1_ref[0] * out, lg_ref[...], lb_ref[...])


def _merge(ya, yr2, aux, ys2, proj, x2d, g1, lp, seq_len):
    n = x2d.shape[0]
    tm = min(512, seq_len)
    per_b = seq_len // tm
    row = lambda i: (i, 0)
    row3 = lambda i: (0, i, 0)
    full = lambda i: (0, 0)
    gate = lambda col: pl.BlockSpec((tm, D_MODEL), lambda i: (i, col // D_MODEL))
    vec512 = pl.BlockSpec((1, RW_W), full)
    vec = pl.BlockSpec((1, D_MODEL), full)
    return pl.pallas_call(
        _merge_kernel,
        grid=(n // tm,),
        in_specs=[
            pl.BlockSpec((tm, DA_W), row),
            pl.BlockSpec((2, tm, RW_W), row3), pl.BlockSpec((2, tm, RW_W), row3),
            pl.BlockSpec((2, tm, SSM_W), row3),
            pl.BlockSpec((tm, SSM_W), lambda i: (i, C_Z // SSM_W)),
            gate(C_GA), gate(C_GR), gate(C_GS),
            pl.BlockSpec((tm, D_MODEL), row),
            pl.BlockSpec((1, 1, D_MODEL), lambda i: (i // per_b, 0, 0)),
            pl.BlockSpec((RW_W, RW_W), full), vec512, vec512, vec512,
            pl.BlockSpec((DA_W, D_MODEL), full), pl.BlockSpec((RW_W, D_MODEL), full),
            pl.BlockSpec((SSM_W, D_MODEL), full), pl.BlockSpec((D_MODEL, D_MODEL), full),
            vec, vec,
        ],
        out_specs=pl.BlockSpec((tm, D_MODEL), row),
        out_shape=jax.ShapeDtypeStruct((n, D_MODEL), F32),
        compiler_params=_cparams(("arbitrary",)),
        name="merge_ln",
    )(ya, yr2, aux, ys2, proj, proj, proj, proj, x2d, g1, lp['rw']['seg'], lp['rw_ln_g'], lp['rw_ln_b'],
      lp['ssm_norm_g'], lp['p_attn'], lp['p_rwkv'], lp['p_ssm'], lp['w_out'], lp['ln1_g'], lp['ln1_b'])


def _ffn_kernel(x_ref, sc_ref, sh_ref, g2_ref, w1_ref, w3_ref, w2_ref, lg_ref, lb_ref, o_ref, h_scr, acc_scr):
    j = pl.program_id(1)

    @pl.when(j == 0)
    def _():
        h_scr[...] = (x_ref[...] * (1.0 + sc_ref[0]) + sh_ref[0]).astype(BF16)
        acc_scr[...] = jnp.zeros_like(acc_scr)

    h = h_scr[...]
    u = jax.nn.silu(_dot(h, w1_ref[...])) * _dot(h, w3_ref[...])
    acc_scr[...] += _dot(u.astype(BF16), w2_ref[...])

    @pl.when(j == pl.num_programs(1) - 1)
    def _():
        o_ref[...] = _layernorm_rows(ALPHA * x_ref[...] + g2_ref[0] * acc_scr[...], lg_ref[...], lb_ref[...])


def _ffn(x2d, sc, sh, g2, w1, w3, w2, lg, lb, seq_len):
    n = x2d.shape[0]
    tm = min(512, seq_len)
    th = FFN_HIDDEN // 2
    per_b = seq_len // tm
    mod = pl.BlockSpec((1, 1, D_MODEL), lambda i, j: (i // per_b, 0, 0))
    return pl.pallas_call(
        _ffn_kernel,
        grid=(n // tm, FFN_HIDDEN // th),
        in_specs=[
            pl.BlockSpec((tm, D_MODEL), lambda i, j: (i, 0)),
            mod, mod, mod,
            pl.BlockSpec((D_MODEL, th), lambda i, j: (0, j)),
            pl.BlockSpec((D_MODEL, th), lambda i, j: (0, j)),
            pl.BlockSpec((th, D_MODEL), lambda i, j: (j, 0)),
            pl.BlockSpec((1, D_MODEL), lambda i, j: (0, 0)),
            pl.BlockSpec((1, D_MODEL), lambda i, j: (0, 0)),
        ],
        out_specs=pl.BlockSpec((tm, D_MODEL), lambda i, j: (i, 0)),
        out_shape=jax.ShapeDtypeStruct((n, D_MODEL), F32),
        scratch_shapes=[pltpu.VMEM((tm, D_MODEL), BF16), pltpu.VMEM((tm, D_MODEL), F32)],
        compiler_params=_cparams(("arbitrary", "arbitrary")),
        name="ffn_ln",
    )(x2d, sc, sh, g2, w1, w3, w2, lg, lb)


def _rope_tables(n_tok):
    rows = n_tok // GRID_W
    row = jnp.repeat(jnp.arange(rows), GRID_W).astype(F32)
    col = jnp.tile(jnp.arange(GRID_W), rows).astype(F32)
    nf = DA_HD // 4
    inv = ROPE_BASE ** (-jnp.arange(nf, dtype=F32) / nf)
    ang = jnp.concatenate([row[:, None] * inv] * 2 + [col[:, None] * inv] * 2, axis=-1)
    cos = jnp.cos(ang)
    sin = jnp.sin(ang)
    first = (jnp.arange(DA_HD) % (2 * nf)) < nf
    s_dn = jnp.where(first, -sin, 0.0)
    s_up = jnp.where(first, 0.0, sin)
    reps = LANES // DA_HD
    return jnp.tile(cos, (1, reps)), jnp.tile(s_dn, (1, reps)), jnp.tile(s_up, (1, reps))


def _permute_w_in(w_in):
    parts, acc = [], 0
    for s in W_IN_SPLIT:
        parts.append(w_in[:, acc:acc + s])
        acc += s
    q, k, v, rw, z, xbc, dt, ga, gr, gs = parts
    pad = jnp.zeros((w_in.shape[0], RW_BLK - RW_COLS - dt.shape[1]), w_in.dtype)
    return jnp.concatenate([rw, dt, pad, ga, gr, gs, q, k, v, z, xbc], axis=1).astype(BF16)


def _pad_lora(w, rank):
    z = jnp.zeros_like(w[0])
    return jnp.stack([jnp.concatenate([w[0], z], axis=0), jnp.concatenate([z, w[1]], axis=0)]).astype(BF16)


def _mixer(x, mod, lp, layer, side, want_out, rope_tabs):
    b, l, _ = x.shape
    sh1, sc1, g1 = mod
    n = b * l
    x2d = x.reshape(n, D_MODEL)
    proj = _inproj(x2d, sc1, sh1, lp['w_in_p'], l)
    if side is None:
        proj_ctx, ctx_len = None, 0
        s0_r = jnp.zeros((2, b, RW_W // LANES, LANES, LANES), F32)
        s0_s = jnp.zeros((2, b, SSM_GROUPS, SSM_STATE, SSM_W // SSM_GROUPS), F32)
    else:
        (proj_ctx, ctx_len), s0_r, s0_s = side
    y_r2, aux, sf_r = _rwkv_scan(proj, b, l, lp['rw'], s0_r)
    y_s2, sf_s = _ssd_scan(proj, b, l, lp['ssm'], s0_s)
    new_side = ((proj, l), sf_r, sf_s)
    if not want_out:
        return None, new_side
    lam_init = 0.8 - 0.6 * math.exp(-0.3 * layer)
    lam = (jnp.exp(jnp.sum(lp['da_lq1'] * lp['da_lk1'])) - jnp.exp(jnp.sum(lp['da_lq2'] * lp['da_lk2']))
           + lam_init).reshape(1)
    y_a = _diff_attention(lam, proj, b, l, lp['da_norm_g'].reshape(1, 2 * DA_HD), 1.0 - lam_init,
                          proj_ctx, ctx_len, rope_tabs)
    x1 = _merge(y_a, y_r2.reshape(2, n, RW_W), aux.reshape(2, n, RW_W),
                y_s2.reshape(2, n, SSM_W), proj, x2d, g1, lp, l)
    return x1, new_side


def kernel(x, c, ctx, c_ctx, ada_w, ada_b, w_in, da_lq1, da_lk1, da_lq2, da_lk2, da_norm_g,
           rw_mu, rw_w0, rw_w2, rw_a0, rw_a2, rw_g2, rw_kk, rw_ka, rw_rk, rw_ln_g, rw_ln_b,
           ssm_conv_w, ssm_conv_b, ssm_dt_bias, ssm_a_log, ssm_d, ssm_norm_g,
           p_attn, p_rwkv, p_ssm, w_out, ln1_g, ln1_b, ffn_w1, ffn_w3, ffn_w2, ln2_g, ln2_b):
    b, l, _ = x.shape
    lc = ctx.shape[1]
    rope_tabs = _rope_tables(l)
    seg = (jnp.arange(RW_W)[:, None] // RW_HD == jnp.arange(RW_W)[None, :] // RW_HD).astype(BF16)
    xc = ctx
    for i in range(DEPTH):
        rw = {
            'mu': jnp.pad(rw_mu[i], (0, RW_BLK - RW_COLS)).reshape(1, RW_BLK),
            'w0': rw_w0[i].reshape(2, 1, RW_W),
            'w2': _pad_lora(rw_w2[i], RW_DECAY_R),
            'a0': rw_a0[i],
            'a2': _pad_lora(rw_a2[i], RW_ICLR_R),
            'g2': rw_g2[i].astype(BF16),
            'vec': jnp.stack([rw_kk[i], rw_ka[i], rw_rk[i].reshape(RW_W)]),
            'seg': seg,
        }
        ssm = {
            'conv_w': ssm_conv_w[i], 'conv_b': ssm_conv_b[i].reshape(1, SSM_XBC),
            'dt_bias': ssm_dt_bias[i].reshape(1, 2 * SSM_HEADS), 'a_log': ssm_a_log[i].reshape(1, 2 * SSM_HEADS),
            'dskip': jnp.repeat(ssm_d[i], SSM_HD).reshape(1, SSM_W),
        }
        lp = {
            'w_in_p': _permute_w_in(w_in[i]), 'rw': rw, 'ssm': ssm,
            'da_lq1': da_lq1[i], 'da_lk1': da_lk1[i], 'da_lq2': da_lq2[i], 'da_lk2': da_lk2[i],
            'da_norm_g': da_norm_g[i],
            'rw_ln_g': rw_ln_g[i].reshape(1, RW_W), 'rw_ln_b': rw_ln_b[i].reshape(1, RW_W),
            'ssm_norm_g': ssm_norm_g[i].reshape(1, SSM_W),
            'p_attn': p_attn[i].astype(BF16), 'p_rwkv': p_rwkv[i].astype(BF16),
            'p_ssm': p_ssm[i].astype(BF16), 'w_out': w_out[i].astype(BF16),
            'ln1_g': ln1_g[i].reshape(1, D_MODEL), 'ln1_b': ln1_b[i].reshape(1, D_MODEL),
        }
        w1, w3, w2 = ffn_w1[i].astype(BF16), ffn_w3[i].astype(BF16), ffn_w2[i].astype(BF16)
        l2g, l2b = ln2_g[i].reshape(1, D_MODEL), ln2_b[i].reshape(1, D_MODEL)
        last = i == DEPTH - 1
        mod_x = (jax.nn.silu(c) @ ada_w[i] + ada_b[i])[:, None, :]
        mod_c = jnp.broadcast_to((jax.nn.silu(c_ctx) @ ada_w[i] + ada_b[i])[None, None, :], (b, 1, 6 * D_MODEL))
        sh1, sc1, g1, sh2, sc2, g2 = jnp.split(mod_x, 6, axis=-1)
        csh1, csc1, cg1, csh2, csc2, cg2 = jnp.split(mod_c, 6, axis=-1)
        xc1, side = _mixer(xc, (csh1, csc1, cg1), lp, i, None, not last, None)
        x1, _ = _mixer(x, (sh1, sc1, g1), lp, i, side, True, rope_tabs)
        x = _ffn(x1, sc2, sh2, g2, w1, w3, w2, l2g, l2b, l).reshape(b, l, D_MODEL)
        if not last:
            xc = _ffn(xc1, csc2, csh2, cg2, w1, w3, w2, l2g, l2b, lc).reshape(b, lc, D_MODEL)
    return x
```

```python
import functools
import math

import jax
import jax.numpy as jnp
from jax import lax
from jax.experimental import pallas as pl
from jax.experimental.pallas import tpu as pltpu

F32 = jnp.float32
BF16 = jnp.bfloat16

D_MODEL = 1024
DEPTH = 4
GRID_W = 64
DA_HEADS = 4
DA_HD = 64
DA_W = DA_HEADS * 2 * DA_HD
ROPE_BASE = 10000.0
RW_HEADS = 8
RW_HD = 64
RW_W = RW_HEADS * RW_HD
RW_DECAY_R = 64
RW_ICLR_R = 64
RW_GATE_R = 128
RW_GN_EPS = 64e-5
RW_COLS = 3 * RW_W + 2 * RW_DECAY_R + 2 * RW_ICLR_R + RW_GATE_R
SSM_HEADS = 8
SSM_HD = 64
SSM_W = SSM_HEADS * SSM_HD
SSM_GROUPS = 2
SSM_STATE = 128
SSM_XBC = SSM_W + 2 * SSM_GROUPS * SSM_STATE
FFN_HIDDEN = (8 * D_MODEL + 3 * 256 - 1) // (3 * 256) * 256
NORM_EPS = 1e-5
W_IN_SPLIT = (DA_W, DA_W, DA_W, RW_COLS, SSM_W, SSM_XBC, 2 * SSM_HEADS, D_MODEL, D_MODEL, D_MODEL)
ALPHA = (2.0 * DEPTH) ** 0.25

LANES = 128
SUBLANES = 8
HALO_ROWS = 16
VMEM_LIMIT_BYTES = 56 * 1024 * 1024

PROJ_COLS = 8192
RW_BLK = 2048
C_RW = 0
C_DT = RW_COLS
C_GA, C_GR, C_GS = RW_BLK, RW_BLK + D_MODEL, RW_BLK + 2 * D_MODEL
C_Q = RW_BLK + 3 * D_MODEL
C_K = C_Q + DA_W
C_V = C_K + DA_W
C_Z = C_V + DA_W
C_XBC = C_Z + SSM_W

Q_SCALE = math.log2(math.e) * DA_HD ** -0.5
ATTN_SUBTILE = 128
ATTN_BLOCK = 512
RW_CHUNK = 64
RW_BLOCK = 256
SSD_CHUNK = 128
SSD_BLOCK = 256


def _cparams(sem):
    return pltpu.CompilerParams(dimension_semantics=sem, vmem_limit_bytes=VMEM_LIMIT_BYTES)


def _dot(a, b):
    return jnp.dot(a, b, preferred_element_type=F32)


def _dot_nt(a, b):
    return lax.dot_general(a, b, (((1,), (1,)), ((), ())), preferred_element_type=F32)


def _dot_tn(a, b):
    return lax.dot_general(a, b, (((0,), (0,)), ((), ())), preferred_element_type=F32)


def _bf(x):
    return x.astype(BF16)


def _split3(x):
    hi = _bf(x)
    r1 = x - hi.astype(F32)
    mid = _bf(r1)
    lo = _bf(r1 - mid.astype(F32))
    return hi, mid, lo


def _cumsum_rows(tri_b, x):
    n = x.shape[1]
    hi, mid, lo = _split3(x)
    o = _dot(tri_b, jnp.concatenate([hi, mid, lo], axis=1))
    return o[:, :n] + o[:, n:2 * n] + o[:, 2 * n:]


def _seg_sum(x, seg_b):
    m = x.shape[0]
    hi = _bf(x)
    lo = _bf(x - hi.astype(F32))
    o = _dot(jnp.concatenate([hi, lo], axis=0), seg_b)
    return o[:m] + o[m:]


def _sigmoid(x):
    return 0.5 * jnp.tanh(0.5 * x) + 0.5


def _softplus(x):
    return jnp.maximum(x, 0.0) + jnp.log(1.0 + jnp.exp(-jnp.abs(x)))


def _layernorm_rows(t, g, b):
    mu = jnp.mean(t, -1, keepdims=True)
    d = t - mu
    var = jnp.mean(d * d, -1, keepdims=True)
    return d * lax.rsqrt(var + NORM_EPS) * g + b


def _neighbour_rows(x, prev_row, next_row):
    t = x.shape[0]
    rowi = lax.broadcasted_iota(jnp.int32, (t, 1), 0)
    x_m1 = jnp.where(rowi == 0, prev_row, pltpu.roll(x, 1, 0))
    x_p1 = jnp.where(rowi == t - 1, next_row, pltpu.roll(x, t - 1, 0))
    return x_m1, x_p1


def _halo_specs(width, col_blk, seq_len, tb, nc, n_rows):
    per8 = tb // HALO_ROWS
    last8 = n_rows // HALO_ROWS - 1

    def blk(d, bi, c):
        return bi * nc + c + d * (nc - 1 - 2 * c)

    prev = pl.BlockSpec((HALO_ROWS, width), lambda d, bi, c: (jnp.maximum(blk(d, bi, c) * per8 - 1, 0), col_blk))
    nxt = pl.BlockSpec((HALO_ROWS, width),
                       lambda d, bi, c: (jnp.minimum((blk(d, bi, c) + 1) * per8, last8), col_blk))
    cur = pl.BlockSpec((tb, width), lambda d, bi, c: (blk(d, bi, c), col_blk))
    return cur, prev, nxt


def _inproj_kernel(x_ref, sc_ref, sh_ref, w_ref, o_ref, h_scr):
    @pl.when(pl.program_id(1) == 0)
    def _():
        h = x_ref[...] * (1.0 + sc_ref[0]) + sh_ref[0]
        h_scr[...] = h.astype(BF16)

    o_ref[...] = _bf(_dot(h_scr[...], w_ref[...]))


def _inproj(x2d, sc, sh, w, seq_len):
    n = x2d.shape[0]
    tm = min(1024, seq_len)
    tn = 2048
    per_b = seq_len // tm
    return pl.pallas_call(
        _inproj_kernel,
        grid=(n // tm, PROJ_COLS // tn),
        in_specs=[
            pl.BlockSpec((tm, D_MODEL), lambda i, j: (i, 0)),
            pl.BlockSpec((1, 1, D_MODEL), lambda i, j: (i // per_b, 0, 0)),
            pl.BlockSpec((1, 1, D_MODEL), lambda i, j: (i // per_b, 0, 0)),
            pl.BlockSpec((D_MODEL, tn), lambda i, j: (0, j)),
        ],
        out_specs=pl.BlockSpec((tm, tn), lambda i, j: (i, j)),
        out_shape=jax.ShapeDtypeStruct((n, PROJ_COLS), BF16),
        scratch_shapes=[pltpu.VMEM((tm, D_MODEL), BF16)],
        compiler_params=_cparams(("arbitrary", "arbitrary")),
        name="inproj",
    )(x2d, sc, sh, w)


def _rope_lanes(x, cos, s_dn, s_up):
    nf = DA_HD // 4
    return x * cos + pltpu.roll(x, LANES - nf, 1) * s_dn + pltpu.roll(x, nf, 1) * s_up


def _attn_core(lam, q, k_scr, v_scr, g_ref, o_ref, out_scale):
    tq = ATTN_SUBTILE
    nsub = q.shape[0] // tq
    lk = k_scr.shape[0]
    lane = lax.broadcasted_iota(jnp.int32, (1, LANES), 1)
    mid = (lk // 2) // LANES * LANES
    cuts = (0, mid, lk) if mid else (0, lk)
    spans = list(zip(cuts[:-1], cuts[1:]))
    scores = []
    for t in range(nsub):
        qt = q[t * tq:(t + 1) * tq]
        zero = jnp.zeros_like(qt)
        qq = jnp.concatenate([jnp.where(lane < DA_HD, qt, zero), jnp.where(lane >= DA_HD, qt, zero)], axis=0)
        scores.append([_dot_nt(qq, k_scr[lo:hi, :]) for lo, hi in spans])
    for t, ss in enumerate(scores):
        m = functools.reduce(jnp.maximum, [jnp.max(s, -1, keepdims=True) for s in ss])
        ov = functools.reduce(jnp.add, [_dot(_bf(jnp.exp2(s - m)), v_scr[lo:hi, :]) for s, (lo, hi) in zip(ss, spans)])
        on = ov[:, :LANES] / ov[:, LANES:]
        o = on[:tq] - lam * on[tq:]
        ms = jnp.mean(o * o, -1, keepdims=True)
        o_ref[t * tq:(t + 1) * tq, :] = _bf(o * lax.rsqrt(ms + NORM_EPS) * (g_ref[...] * out_scale))


def _attn_latent_kernel(lam_ref, q_ref, kl_ref, vl_ref, kc_ref, vc_ref, cos_ref, sdn_ref, sup_ref, g_ref,
                        o_ref, k_scr, v_scr, *, out_scale):
    i = pl.program_id(2)
    tq = q_ref.shape[0]
    lc = kc_ref.shape[0]

    @pl.when(i == 0)
    def _():
        k_scr[:lc, :] = kc_ref[...]
        k_scr[lc:, :] = _bf(_rope_lanes(kl_ref[...].astype(F32), cos_ref[...], sdn_ref[...], sup_ref[...]))
        v_scr[:lc, :LANES] = vc_ref[...]
        v_scr[lc:, :LANES] = vl_ref[...]
        v_scr[:, LANES:] = jnp.ones((v_scr.shape[0], LANES), BF16)

    rows = pl.ds(pl.multiple_of(i * tq, tq), tq)
    q = _rope_lanes(q_ref[...].astype(F32), cos_ref[rows, :], sdn_ref[rows, :], sup_ref[rows, :])
    _attn_core(lam_ref[0], _bf(q * Q_SCALE), k_scr, v_scr, g_ref, o_ref, out_scale)


def _attn_ctx_kernel(lam_ref, q_ref, kl_ref, vl_ref, g_ref, o_ref, k_scr, v_scr, *, out_scale):
    @pl.when(pl.program_id(2) == 0)
    def _():
        k_scr[...] = kl_ref[...]
        v_scr[:, :LANES] = vl_ref[...]
        v_scr[:, LANES:] = jnp.ones((v_scr.shape[0], LANES), BF16)

    _attn_core(lam_ref[0], _bf(q_ref[...].astype(F32) * Q_SCALE), k_scr, v_scr, g_ref, o_ref, out_scale)


def _diff_attention(lam, proj, batch, seq_len, g, out_scale, proj_ctx=None, ctx_len=0, rope_tabs=None):
    tq = min(ATTN_BLOCK, seq_len)
    nq = seq_len // tq
    lk = seq_len + ctx_len
    cq, ck, cv = C_Q // LANES, C_K // LANES, C_V // LANES
    smem = pl.BlockSpec(memory_space=pltpu.SMEM)
    qspec = pl.BlockSpec((tq, LANES), lambda bi, h, i: (bi * nq + i, cq + h))
    kspec = pl.BlockSpec((seq_len, LANES), lambda bi, h, i: (bi, ck + h))
    vspec = pl.BlockSpec((seq_len, LANES), lambda bi, h, i: (bi, cv + h))
    gspec = pl.BlockSpec((1, LANES), lambda bi, h, i: (0, 0))
    common = dict(
        grid=(batch, DA_HEADS, nq),
        out_specs=pl.BlockSpec((tq, LANES), lambda bi, h, i: (bi * nq + i, h)),
        out_shape=jax.ShapeDtypeStruct((batch * seq_len, DA_W), BF16),
        scratch_shapes=[pltpu.VMEM((lk, LANES), BF16), pltpu.VMEM((lk, 2 * LANES), BF16)],
        compiler_params=_cparams(("arbitrary", "arbitrary", "arbitrary")),
    )
    if proj_ctx is None:
        return pl.pallas_call(
            functools.partial(_attn_ctx_kernel, out_scale=out_scale),
            in_specs=[smem, qspec, kspec, vspec, gspec], name="diff_attn_ctx", **common,
        )(lam, proj, proj, proj, g)
    tab = pl.BlockSpec((seq_len, LANES), lambda bi, h, i: (0, 0))
    return pl.pallas_call(
        functools.partial(_attn_latent_kernel, out_scale=out_scale),
        in_specs=[smem, qspec, kspec, vspec,
                  pl.BlockSpec((ctx_len, LANES), lambda bi, h, i: (bi, ck + h)),
                  pl.BlockSpec((ctx_len, LANES), lambda bi, h, i: (bi, cv + h)),
                  tab, tab, tab, gspec],
        name="diff_attn", **common,
    )(lam, proj, proj, proj, proj_ctx, proj_ctx, *rope_tabs, g)


def _rwkv_kernel(u_ref, up_ref, un_ref, mu_ref, w0_ref, w2_ref, a0_ref, a2_ref, g2_ref, vec_ref, seg_ref,
                 s0_ref, y_ref, aux_ref, sf_ref, s_scr):
    d = pl.program_id(0)
    c = pl.program_id(2)
    nc = pl.num_programs(2)

    @pl.when(c == 0)
    def _():
        s_scr[...] = s0_ref[0, 0]

    refs = (u_ref, up_ref, un_ref, mu_ref, w0_ref, w2_ref, a0_ref, a2_ref, g2_ref, vec_ref, seg_ref,
            y_ref, aux_ref, s_scr)

    @pl.when(d == 0)
    def _():
        _rwkv_step(True, c, nc, *refs)

    @pl.when(d == 1)
    def _():
        _rwkv_step(False, nc - 1 - c, nc, *refs)

    @pl.when(c == nc - 1)
    def _():
        sf_ref[0, 0] = s_scr[...]


def _rwkv_step(fwd, cc, nc, u_ref, up_ref, un_ref, mu_ref, w0_ref, w2_ref, a0_ref, a2_ref, g2_ref, vec_ref,
               seg_ref, y_ref, aux_ref, s_scr):
    t = RW_CHUNK
    tb = u_ref.shape[0]
    nper = tb // t
    di = 0 if fwd else 1

    u = u_ref[:, :RW_COLS].astype(F32)
    prev = up_ref[HALO_ROWS - 1:HALO_ROWS, :RW_COLS].astype(F32) * (cc > 0).astype(F32)
    nxt = un_ref[0:1, :RW_COLS].astype(F32) * (cc < nc - 1).astype(F32)
    u_m1, u_p1 = _neighbour_rows(u, prev, nxt)
    u = u + mu_ref[:, :RW_COLS] * (0.5 * (u_m1 + u_p1) - u)
    r = r_all = u[:, :RW_W]
    k = u[:, RW_W:2 * RW_W]
    v = v_all = u[:, 2 * RW_W:3 * RW_W]
    o = 3 * RW_W
    wd = _bf(jnp.tanh(u[:, o:o + 2 * RW_DECAY_R]))
    o += 2 * RW_DECAY_R
    ad = _bf(u[:, o:o + 2 * RW_ICLR_R])
    o += 2 * RW_ICLR_R
    gd = u[:, o:o + RW_GATE_R]
    kk_gain, ka, rk = vec_ref[0:1, :], vec_ref[1:2, :], vec_ref[2:3, :]
    seg = seg_ref[...]
    lw_all = -math.exp(-0.5) * _sigmoid(w0_ref[0] + _dot(wd, w2_ref[0]))
    a_all = _sigmoid(a0_ref[di:di + 1, :] + _dot(ad, a2_ref[di]))
    kk_all = k * kk_gain
    kk_all = kk_all * lax.rsqrt(jnp.maximum(_seg_sum(kk_all * kk_all, seg), 1e-24))
    kd_all = k * (1.0 + (a_all - 1.0) * ka)
    if fwd:
        a_other = _sigmoid(a0_ref[1:2, :] + _dot(ad, a2_ref[1]))
        k_bonus = k * (1.0 + (0.5 * (a_all + a_other) - 1.0) * ka)
        aux_ref[0, 0] = _bf(_seg_sum(r * k_bonus * rk, seg) * v)
    else:
        aux_ref[0, 0] = _bf(_dot(_bf(_sigmoid(gd)), g2_ref[...]))

    sign = 1 if fwd else -1
    row = lax.broadcasted_iota(jnp.int32, (t, t), 0)
    col = lax.broadcasted_iota(jnp.int32, (t, t), 1)
    tri_b = ((row - col) * sign >= 0).astype(BF16)
    row2 = lax.broadcasted_iota(jnp.int32, (t, 2 * t), 0)
    col2 = lax.broadcasted_iota(jnp.int32, (t, 2 * t), 1)
    col2 = jnp.where(col2 >= t, col2 - t, col2)
    delta2 = (row2 - col2) * sign
    strict2 = delta2 > 0
    incl2 = delta2 >= 0
    colh = lax.broadcasted_iota(jnp.int32, (1, 2 * t), 1)
    cm0 = (colh < t).astype(F32)
    cm1 = 1.0 - cm0
    lane = lax.broadcasted_iota(jnp.int32, (1, LANES), 1)
    m0 = (lane < RW_HD).astype(F32)
    m1 = 1.0 - m0
    rb = lax.broadcasted_iota(jnp.int32, (LANES, LANES), 0) // RW_HD
    cb = lax.broadcasted_iota(jnp.int32, (LANES, LANES), 1) // RW_HD
    bd = (rb == cb).astype(F32)
    eye = (lax.broadcasted_iota(jnp.int32, (2 * t, 2 * t), 0)
           == lax.broadcasted_iota(jnp.int32, (2 * t, 2 * t), 1)).astype(F32)

    def stack2(x):
        return _bf(jnp.concatenate([x * m0, x * m1], axis=0))

    nhp = RW_W // LANES
    inst = [(j, i) for i in range(nper) for j in range(nhp)]

    def rows_of(i):
        ip = i if fwd else nper - 1 - i
        return slice(ip * t, (ip + 1) * t)

    def sl_of(j):
        return slice(j * LANES, (j + 1) * LANES)

    lws = [lw_all[rows_of(i), sl_of(j)] for j, i in inst]
    cums = [_cumsum_rows(tri_b, lw) for lw in lws]
    pre = []
    for (j, i), lw, cum in zip(inst, lws, cums):
        rows, sl = rows_of(i), sl_of(j)
        r = r_all[rows, sl]
        v = v_all[rows, sl]
        kk = kk_all[rows, sl]
        kd = kd_all[rows, sl]
        a = a_all[rows, sl]
        tot = jnp.sum(lw, axis=0, keepdims=True)
        g_inv = jnp.exp(-cum)
        g_rem = jnp.exp(tot - cum)
        rt = r * jnp.exp(cum)
        kt = _bf(kk * jnp.exp(cum - lw))
        b = kk * a
        bh = b * g_inv
        kh = kd * g_inv
        ystack = _bf(jnp.concatenate([bh * m0, bh * m1, kh * m0, kh * m1], axis=0))
        pre.append(dict(v=v, rt=rt, kt=kt, ystack=ystack, g_tot=jnp.exp(tot), vst=stack2(v),
                        rhs_r=_bf(jnp.concatenate([kd * g_rem, b * g_rem], axis=0))))
    ggs = [_dot_nt(jnp.concatenate([p['kt'], _bf(p['rt'])], axis=0), p['ystack']) for p in pre]
    for p, gg in zip(pre, ggs):
        mcat = jnp.where(strict2, gg[:t, :2 * t], 0.0)
        p['ncat'] = _bf(jnp.where(strict2, gg[:t, 2 * t:], 0.0))
        p['qcat'] = _bf(jnp.where(incl2, gg[t:, :2 * t], 0.0))
        p['pcat'] = _bf(jnp.where(incl2, gg[t:, 2 * t:], 0.0))
        p['m'] = jnp.concatenate([mcat * cm0, mcat * cm1], axis=0)
    ri = lax.broadcasted_iota(jnp.int32, (2 * t, 2 * t), 0)
    ci = lax.broadcasted_iota(jnp.int32, (2 * t, 2 * t), 1)
    same_head = (ri // t) == (ci // t)
    lo_i, hi_i = (ci, ri) if fwd else (ri, ci)

    def level_mask(sz):
        return (same_head & ((ri % t) // (2 * sz) == (ci % t) // (2 * sz))
                & ((hi_i % (2 * sz)) >= sz) & ((lo_i % (2 * sz)) < sz))

    nvs = [_dot(p['ncat'], p['vst']) for p in pre]
    ws = [eye - jnp.where(level_mask(1), p['m'], 0.0) for p in pre]
    sz = 2
    while sz < t:
        lm = level_mask(sz)
        zs = [_dot(_bf(w), _bf(jnp.where(lm, p['m'], 0.0))) for w, p in zip(ws, pre)]
        ws = [w - _dot(_bf(z), _bf(w)) for w, z in zip(ws, zs)]
        sz *= 2
    wcats = [_bf(w[:t] + w[t:]) for w in ws]
    uks = [_dot(wc, jnp.concatenate([stack2(nv), stack2(p['kt'].astype(F32))], axis=1))
           for wc, nv, p in zip(wcats, nvs, pre)]
    qus = [_dot(p['qcat'], jnp.concatenate([stack2(uk[:, :LANES]), stack2(uk[:, LANES:])], axis=1))
           for p, uk in zip(pre, uks)]
    pvs = [_dot(p['pcat'], p['vst']) for p in pre]
    zeros_t = jnp.zeros((t, LANES), F32)
    dcs = [_dot_tn(_bf(jnp.concatenate([jnp.concatenate([p['v'], -uk[:, :LANES]], axis=0),
                                        jnp.concatenate([zeros_t, uk[:, LANES:]], axis=0)], axis=1)), p['rhs_r'])
           for p, uk in zip(pre, uks)]
    for p, qu, pv, dc in zip(pre, qus, pvs, dcs):
        p['y0'] = pv - qu[:, :LANES]
        p['rp'] = _bf(p['rt'] - qu[:, LANES:])
        p['d0'] = bd * dc[:LANES]
        p['c0'] = _bf(bd * dc[LANES:])
    states = [s_scr[j] for j in range(nhp)]
    for i in range(nper):
        ps = [pre[i * nhp + j] for j in range(nhp)]
        sbs = [_bf(sbd) for sbd in states]
        ys = [_dot_nt(p['rp'], sb) for p, sb in zip(ps, sbs)]
        scs = [_dot(sb, p['c0']) for p, sb in zip(ps, sbs)]
        for j in range(nhp):
            y_ref[0, 0, rows_of(i), sl_of(j)] = _bf(ps[j]['y0'] + ys[j])
        states = [sbd * p['g_tot'] - sc + p['d0'] for sbd, p, sc in zip(states, ps, scs)]
    for j in range(nhp):
        s_scr[j] = states[j]


def _rwkv_scan(proj, batch, seq_len, rwp, s0):
    n = proj.shape[0]
    tb = min(RW_BLOCK, seq_len)
    nc = seq_len // tb
    nhp = RW_W // LANES
    cur, prev, nxt = _halo_specs(RW_BLK, C_RW // RW_BLK, seq_len, tb, nc, n)
    full2 = lambda d, bi, c: (0, 0)
    full3 = lambda d, bi, c: (0, 0, 0)
    dmap = lambda d, bi, c: (d, bi, c + d * (nc - 1 - 2 * c), 0)
    smap = lambda d, bi, c: (d, bi, 0, 0, 0)
    return pl.pallas_call(
        _rwkv_kernel,
        grid=(2, batch, nc),
        in_specs=[
            cur, prev, nxt,
            pl.BlockSpec((1, RW_BLK), full2),
            pl.BlockSpec((1, 1, RW_W), lambda d, bi, c: (d, 0, 0)),
            pl.BlockSpec((1, 2 * RW_DECAY_R, RW_W), lambda d, bi, c: (d, 0, 0)),
            pl.BlockSpec((2, RW_W), full2),
            pl.BlockSpec((2, 2 * RW_ICLR_R, RW_W), full3),
            pl.BlockSpec((RW_GATE_R, RW_W), full2),
            pl.BlockSpec((3, RW_W), full2),
            pl.BlockSpec((RW_W, RW_W), full2),
            pl.BlockSpec((1, 1, nhp, LANES, LANES), smap),
        ],
        out_specs=[
            pl.BlockSpec((1, 1, tb, RW_W), dmap),
            pl.BlockSpec((1, 1, tb, RW_W), dmap),
            pl.BlockSpec((1, 1, nhp, LANES, LANES), smap),
        ],
        out_shape=[
            jax.ShapeDtypeStruct((2, batch, seq_len, RW_W), BF16),
            jax.ShapeDtypeStruct((2, batch, seq_len, RW_W), BF16),
            jax.ShapeDtypeStruct((2, batch, nhp, LANES, LANES), F32),
        ],
        scratch_shapes=[pltpu.VMEM((nhp, LANES, LANES), F32)],
        compiler_params=_cparams(("arbitrary", "arbitrary", "arbitrary")),
        name="rwkv_scan",
    )(proj, proj, proj, rwp['mu'], rwp['w0'], rwp['w2'], rwp['a0'], rwp['a2'], rwp['g2'], rwp['vec'],
      rwp['seg'], s0)


def _ssd_kernel(x_ref, xp_ref, xn_ref, dt_ref, cw_ref, cb_ref, dtb_ref, alog_ref, dskip_ref, s0_ref,
                y_ref, sf_ref, s_scr):
    d = pl.program_id(0)
    c = pl.program_id(2)
    nc = pl.num_programs(2)

    @pl.when(c == 0)
    def _():
        s_scr[...] = s0_ref[0, 0]

    refs = (x_ref, xp_ref, xn_ref, dt_ref, cw_ref, cb_ref, dtb_ref, alog_ref, dskip_ref, y_ref, s_scr)

    @pl.when(d == 0)
    def _():
        _ssd_step(True, c, nc, *refs)

    @pl.when(d == 1)
    def _():
        _ssd_step(False, nc - 1 - c, nc, *refs)

    @pl.when(c == nc - 1)
    def _():
        sf_ref[0, 0] = s_scr[...]


def _ssd_step(fwd, cc, nc, x_ref, xp_ref, xn_ref, dt_ref, cw_ref, cb_ref, dtb_ref, alog_ref, dskip_ref,
              y_ref, s_scr):
    t = SSD_CHUNK
    gw = SSM_W // SSM_GROUPS
    hpg = SSM_HEADS // SSM_GROUPS
    nh = SSM_HEADS

    x = x_ref[...].astype(F32)
    prev = xp_ref[HALO_ROWS - 1:HALO_ROWS, :].astype(F32) * (cc > 0).astype(F32)
    nxt = xn_ref[0:1, :].astype(F32) * (cc < nc - 1).astype(F32)
    x_m1, x_p1 = _neighbour_rows(x, prev, nxt)
    xc = x_m1 * cw_ref[0:1, :] + x * cw_ref[1:2, :] + x_p1 * cw_ref[2:3, :] + cb_ref[...]
    xc = xc * _sigmoid(xc)
    xs = xc[:, :SSM_W]
    bm = xc[:, SSM_W:SSM_W + SSM_GROUPS * SSM_STATE]
    cm = xc[:, SSM_W + SSM_GROUPS * SSM_STATE:]
    dt2 = _softplus(dt_ref[...].astype(F32)[:, :2 * nh] + dtb_ref[...])
    a2 = dt2 * (-jnp.exp(alog_ref[...]))
    dt_all = dt2[:, :nh] if fwd else dt2[:, nh:]
    a_all = a2[:, :nh] if fwd else a2[:, nh:]

    sign = 1 if fwd else -1
    row = lax.broadcasted_iota(jnp.int32, (t, t), 0)
    col = lax.broadcasted_iota(jnp.int32, (t, t), 1)
    delta = (row - col) * sign
    incl = delta >= 0
    tri_b = incl.astype(BF16)
    tri_t_b = (delta <= 0).astype(BF16)
    lane = lax.broadcasted_iota(jnp.int32, (1, LANES), 1)
    m0 = (lane < SSM_HD).astype(F32)
    m1 = 1.0 - m0
    lane_g = lax.broadcasted_iota(jnp.int32, (1, gw), 1) // SSM_HD

    def expand(z, g):
        out = jnp.zeros((z.shape[0], gw), F32)
        for hh in range(hpg):
            h = g * hpg + hh
            out = jnp.where(lane_g == hh, z[:, h:h + 1], out)
        return out

    nper = x_ref.shape[0] // t
    states = [s_scr[g] for g in range(SSM_GROUPS)]
    for i in range(nper):
        ip = i if fwd else nper - 1 - i
        rows = slice(ip * t, (ip + 1) * t)
        dt = dt_all[rows]
        a = a_all[rows]
        a3 = jnp.concatenate(_split3(a), axis=1)
        cs3 = _dot(tri_b, a3)
        cs = cs3[:, :nh] + cs3[:, nh:2 * nh] + cs3[:, 2 * nh:]
        cs3_t = _dot_tn(a3, tri_t_b)
        cs_t = cs3_t[:nh] + cs3_t[nh:2 * nh] + cs3_t[2 * nh:]
        tot = jnp.sum(a, axis=0, keepdims=True)
        e_cs = jnp.exp(cs)
        e_rem = jnp.exp(tot - cs)
        e_tot = jnp.exp(tot)
        for g in range(SSM_GROUPS):
            cg = cm[rows, g * SSM_STATE:(g + 1) * SSM_STATE]
            bg = bm[rows, g * SSM_STATE:(g + 1) * SSM_STATE]
            xsg = xs[rows, g * gw:(g + 1) * gw]
            xg = xsg * expand(dt, g)
            cgb = _bf(cg)
            bgb = _bf(bg)
            gm = _dot_nt(cgb, bgb)
            st = states[g]
            y_off = _dot(cgb, _bf(st)) * expand(e_cs, g)
            ys = []
            for hp in range(hpg // 2):
                scs = []
                for hh in range(2):
                    h = g * hpg + 2 * hp + hh
                    diff = jnp.where(incl, cs[:, h:h + 1] - cs_t[h:h + 1, :], 0.0)
                    scs.append(gm * jnp.where(incl, jnp.exp(diff), 0.0))
                xp = xg[:, hp * LANES:(hp + 1) * LANES]
                xst = _bf(jnp.concatenate([xp * m0, xp * m1], axis=0))
                ys.append(_dot(_bf(jnp.concatenate(scs, axis=1)), xst))
            y = jnp.concatenate(ys, axis=1) + y_off
            if fwd:
                y = y + dskip_ref[:, g * gw:(g + 1) * gw] * xsg
            y_ref[0, 0, rows, g * gw:(g + 1) * gw] = _bf(y)
            states[g] = st * expand(e_tot, g) + _dot_tn(bgb, _bf(xg * expand(e_rem, g)))
    for g in range(SSM_GROUPS):
        s_scr[g] = states[g]


def _ssd_scan(proj, batch, seq_len, ssp, s0):
    n = proj.shape[0]
    t = min(SSD_BLOCK, seq_len)
    nc = seq_len // t
    gw = SSM_W // SSM_GROUPS
    cur, prev, nxt = _halo_specs(SSM_XBC, C_XBC // SSM_XBC, seq_len, t, nc, n)
    dtspec, _, _ = _halo_specs(LANES, C_DT // LANES, seq_len, t, nc, n)
    full2 = lambda d, bi, c: (0, 0)
    dmap = lambda d, bi, c: (d, bi, c + d * (nc - 1 - 2 * c), 0)
    smap = lambda d, bi, c: (d, bi, 0, 0, 0)
    return pl.pallas_call(
        _ssd_kernel,
        grid=(2, batch, nc),
        in_specs=[
            cur, prev, nxt, dtspec,
            pl.BlockSpec((3, SSM_XBC), full2),
            pl.BlockSpec((1, SSM_XBC), full2),
            pl.BlockSpec((1, 2 * SSM_HEADS), full2),
            pl.BlockSpec((1, 2 * SSM_HEADS), full2),
            pl.BlockSpec((1, SSM_W), full2),
            pl.BlockSpec((1, 1, SSM_GROUPS, SSM_STATE, gw), smap),
        ],
        out_specs=[
            pl.BlockSpec((1, 1, t, SSM_W), dmap),
            pl.BlockSpec((1, 1, SSM_GROUPS, SSM_STATE, gw), smap),
        ],
        out_shape=[
            jax.ShapeDtypeStruct((2, batch, seq_len, SSM_W), BF16),
            jax.ShapeDtypeStruct((2, batch, SSM_GROUPS, SSM_STATE, gw), F32),
        ],
        scratch_shapes=[pltpu.VMEM((SSM_GROUPS, SSM_STATE, gw), F32)],
        compiler_params=_cparams(("arbitrary", "arbitrary", "arbitrary")),
        name="ssd_scan",
    )(proj, proj, proj, proj, ssp['conv_w'], ssp['conv_b'], ssp['dt_bias'], ssp['a_log'], ssp['dskip'], s0)


def _merge_kernel(ya_ref, yr_ref, aux_ref, ys_ref, z_ref, ga_ref, gr_ref, gs_ref, x_ref, g1_ref,
                  seg_ref, rg_ref, rb_ref, sg_ref, pa_ref, pr_ref, ps_ref, wo_ref, lg_ref, lb_ref, o_ref):
    seg = seg_ref[...]
    yr = yr_ref[0].astype(F32) + yr_ref[1].astype(F32)
    mu = _seg_sum(yr, seg) * (1.0 / RW_HD)
    dv = yr - mu
    var = _seg_sum(dv * dv, seg) * (1.0 / RW_HD)
    yr = (dv * lax.rsqrt(var + RW_GN_EPS) * rg_ref[...] + rb_ref[...] + aux_ref[0].astype(F32)) * aux_ref[1].astype(F32)
    z = z_ref[...].astype(F32)
    ys = (ys_ref[0].astype(F32) + ys_ref[1].astype(F32)) * (z * _sigmoid(z))
    gw = SSM_W // SSM_GROUPS
    parts = []
    for g in range(SSM_GROUPS):
        yg = ys[:, g * gw:(g + 1) * gw]
        parts.append(yg * lax.rsqrt(jnp.mean(yg * yg, -1, keepdims=True) + NORM_EPS))
    ys = jnp.concatenate(parts, axis=1) * sg_ref[...]
    m = (_sigmoid(ga_ref[...].astype(F32)) * _dot(ya_ref[...], pa_ref[...])
         + _sigmoid(gr_ref[...].astype(F32)) * _dot(_bf(yr), pr_ref[...])
         + _sigmoid(gs_ref[...].astype(F32)) * _dot(_bf(ys), ps_ref[...]))
    out = _dot(_bf(m), wo_ref[...])
    o_ref[...] = _layernorm_rows(ALPHA * x_ref[...] + g1_ref[0] * out, lg_ref[...], lb_ref[...])


def _merge(ya, yr2, aux, ys2, proj, x2d, g1, lp, seq_len):
    n = x2d.shape[0]
    tm = min(512, seq_len)
    per_b = seq_len // tm
    row = lambda i: (i, 0)
    row3 = lambda i: (0, i, 0)
    full = lambda i: (0, 0)
    gate = lambda col: pl.BlockSpec((tm, D_MODEL), lambda i: (i, col // D_MODEL))
    vec512 = pl.BlockSpec((1, RW_W), full)
    vec = pl.BlockSpec((1, D_MODEL), full)
    return pl.pallas_call(
        _merge_kernel,
        grid=(n // tm,),
        in_specs=[
            pl.BlockSpec((tm, DA_W), row),
            pl.BlockSpec((2, tm, RW_W), row3), pl.BlockSpec((2, tm, RW_W), row3),
            pl.BlockSpec((2, tm, SSM_W), row3),
            pl.BlockSpec((tm, SSM_W), lambda i: (i, C_Z // SSM_W)),
            gate(C_GA), gate(C_GR), gate(C_GS),
            pl.BlockSpec((tm, D_MODEL), row),
            pl.BlockSpec((1, 1, D_MODEL), lambda i: (i // per_b, 0, 0)),
            pl.BlockSpec((RW_W, RW_W), full), vec512, vec512, vec512,
            pl.BlockSpec((DA_W, D_MODEL), full), pl.BlockSpec((RW_W, D_MODEL), full),
            pl.BlockSpec((SSM_W, D_MODEL), full), pl.BlockSpec((D_MODEL, D_MODEL), full),
            vec, vec,
        ],
        out_specs=pl.BlockSpec((tm, D_MODEL), row),
        out_shape=jax.ShapeDtypeStruct((n, D_MODEL), F32),
        compiler_params=_cparams(("arbitrary",)),
        name="merge_ln",
    )(ya, yr2, aux, ys2, proj, proj, proj, proj, x2d, g1, lp['rw']['seg'], lp['rw_ln_g'], lp['rw_ln_b'],
      lp['ssm_norm_g'], lp['p_attn'], lp['p_rwkv'], lp['p_ssm'], lp['w_out'], lp['ln1_g'], lp['ln1_b'])


def _ffn_kernel(x_ref, sc_ref, sh_ref, g2_ref, w1_ref, w3_ref, w2_ref, lg_ref, lb_ref, o_ref, h_scr, acc_scr):
    j = pl.program_id(1)

    @pl.when(j == 0)
    def _():
        h_scr[...] = (x_ref[...] * (1.0 + sc_ref[0]) + sh_ref[0]).astype(BF16)
        acc_scr[...] = jnp.zeros_like(acc_scr)

    h = h_scr[...]
    a = _dot(h, w1_ref[...])
    u = a * _sigmoid(a) * _dot(h, w3_ref[...])
    acc_scr[...] += _dot(u.astype(BF16), w2_ref[...])

    @pl.when(j == pl.num_programs(1) - 1)
    def _():
        o_ref[...] = _layernorm_rows(ALPHA * x_ref[...] + g2_ref[0] * acc_scr[...], lg_ref[...], lb_ref[...])


def _ffn(x2d, sc, sh, g2, w1, w3, w2, lg, lb, seq_len):
    n = x2d.shape[0]
    tm = min(512, seq_len)
    th = FFN_HIDDEN // 2
    per_b = seq_len // tm
    mod = pl.BlockSpec((1, 1, D_MODEL), lambda i, j: (i // per_b, 0, 0))
    return pl.pallas_call(
        _ffn_kernel,
        grid=(n // tm, FFN_HIDDEN // th),
        in_specs=[
            pl.BlockSpec((tm, D_MODEL), lambda i, j: (i, 0)),
            mod, mod, mod,
            pl.BlockSpec((D_MODEL, th), lambda i, j: (0, j)),
            pl.BlockSpec((D_MODEL, th), lambda i, j: (0, j)),
            pl.BlockSpec((th, D_MODEL), lambda i, j: (j, 0)),
            pl.BlockSpec((1, D_MODEL), lambda i, j: (0, 0)),
            pl.BlockSpec((1, D_MODEL), lambda i, j: (0, 0)),
        ],
        out_specs=pl.BlockSpec((tm, D_MODEL), lambda i, j: (i, 0)),
        out_shape=jax.ShapeDtypeStruct((n, D_MODEL), F32),
        scratch_shapes=[pltpu.VMEM((tm, D_MODEL), BF16), pltpu.VMEM((tm, D_MODEL), F32)],
        compiler_params=_cparams(("arbitrary", "arbitrary")),
        name="ffn_ln",
    )(x2d, sc, sh, g2, w1, w3, w2, lg, lb)


def _rope_tables(n_tok):
    rows = n_tok // GRID_W
    row = jnp.repeat(jnp.arange(rows), GRID_W).astype(F32)
    col = jnp.tile(jnp.arange(GRID_W), rows).astype(F32)
    nf = DA_HD // 4
    inv = ROPE_BASE ** (-jnp.arange(nf, dtype=F32) / nf)
    ang = jnp.concatenate([row[:, None] * inv] * 2 + [col[:, None] * inv] * 2, axis=-1)
    cos = jnp.cos(ang)
    sin = jnp.sin(ang)
    first = (jnp.arange(DA_HD) % (2 * nf)) < nf
    s_dn = jnp.where(first, -sin, 0.0)
    s_up = jnp.where(first, 0.0, sin)
    reps = LANES // DA_HD
    return jnp.tile(cos, (1, reps)), jnp.tile(s_dn, (1, reps)), jnp.tile(s_up, (1, reps))


def _permute_w_in(w_in):
    parts, acc = [], 0
    for s in W_IN_SPLIT:
        parts.append(w_in[:, acc:acc + s])
        acc += s
    q, k, v, rw, z, xbc, dt, ga, gr, gs = parts
    pad = jnp.zeros((w_in.shape[0], RW_BLK - RW_COLS - dt.shape[1]), w_in.dtype)
    return jnp.concatenate([rw, dt, pad, ga, gr, gs, q, k, v, z, xbc], axis=1).astype(BF16)


def _pad_lora(w, rank):
    z = jnp.zeros_like(w[0])
    return jnp.stack([jnp.concatenate([w[0], z], axis=0), jnp.concatenate([z, w[1]], axis=0)]).astype(BF16)


def _mixer(x, mod, lp, layer, side, want_out, rope_tabs):
    b, l, _ = x.shape
    sh1, sc1, g1 = mod
    n = b * l
    x2d = x.reshape(n, D_MODEL)
    proj = _inproj(x2d, sc1, sh1, lp['w_in_p'], l)
    if side is None:
        proj_ctx, ctx_len = None, 0
        s0_r = jnp.zeros((2, b, RW_W // LANES, LANES, LANES), F32)
        s0_s = jnp.zeros((2, b, SSM_GROUPS, SSM_STATE, SSM_W // SSM_GROUPS), F32)
    else:
        (proj_ctx, ctx_len), s0_r, s0_s = side
    y_r2, aux, sf_r = _rwkv_scan(proj, b, l, lp['rw'], s0_r)
    y_s2, sf_s = _ssd_scan(proj, b, l, lp['ssm'], s0_s)
    new_side = ((proj, l), sf_r, sf_s)
    if not want_out:
        return None, new_side
    lam_init = 0.8 - 0.6 * math.exp(-0.3 * layer)
    lam = (jnp.exp(jnp.sum(lp['da_lq1'] * lp['da_lk1'])) - jnp.exp(jnp.sum(lp['da_lq2'] * lp['da_lk2']))
           + lam_init).reshape(1)
    y_a = _diff_attention(lam, proj, b, l, lp['da_norm_g'].reshape(1, 2 * DA_HD), 1.0 - lam_init,
                          proj_ctx, ctx_len, rope_tabs)
    x1 = _merge(y_a, y_r2.reshape(2, n, RW_W), aux.reshape(2, n, RW_W),
                y_s2.reshape(2, n, SSM_W), proj, x2d, g1, lp, l)
    return x1, new_side


def kernel(x, c, ctx, c_ctx, ada_w, ada_b, w_in, da_lq1, da_lk1, da_lq2, da_lk2, da_norm_g,
           rw_mu, rw_w0, rw_w2, rw_a0, rw_a2, rw_g2, rw_kk, rw_ka, rw_rk, rw_ln_g, rw_ln_b,
           ssm_conv_w, ssm_conv_b, ssm_dt_bias, ssm_a_log, ssm_d, ssm_norm_g,
           p_attn, p_rwkv, p_ssm, w_out, ln1_g, ln1_b, ffn_w1, ffn_w3, ffn_w2, ln2_g, ln2_b):
    b, l, _ = x.shape
    lc = ctx.shape[1]
    rope_tabs = _rope_tables(l)
    seg = (jnp.arange(RW_W)[:, None] // RW_HD == jnp.arange(RW_W)[None, :] // RW_HD).astype(BF16)
    xc = ctx
    for i in range(DEPTH):
        rw = {
            'mu': jnp.pad(rw_mu[i], (0, RW_BLK - RW_COLS)).reshape(1, RW_BLK),
            'w0': rw_w0[i].reshape(2, 1, RW_W),
            'w2': _pad_lora(rw_w2[i], RW_DECAY_R),
            'a0': rw_a0[i],
            'a2': _pad_lora(rw_a2[i], RW_ICLR_R),
            'g2': rw_g2[i].astype(BF16),
            'vec': jnp.stack([rw_kk[i], rw_ka[i], rw_rk[i].reshape(RW_W)]),
            'seg': seg,
        }
        ssm = {
            'conv_w': ssm_conv_w[i], 'conv_b': ssm_conv_b[i].reshape(1, SSM_XBC),
            'dt_bias': ssm_dt_bias[i].reshape(1, 2 * SSM_HEADS), 'a_log': ssm_a_log[i].reshape(1, 2 * SSM_HEADS),
            'dskip': jnp.repeat(ssm_d[i], SSM_HD).reshape(1, SSM_W),
        }
        lp = {
            'w_in_p': _permute_w_in(w_in[i]), 'rw': rw, 'ssm': ssm,
            'da_lq1': da_lq1[i], 'da_lk1': da_lk1[i], 'da_lq2': da_lq2[i], 'da_lk2': da_lk2[i],
            'da_norm_g': da_norm_g[i],
            'rw_ln_g': rw_ln_g[i].reshape(1, RW_W), 'rw_ln_b': rw_ln_b[i].reshape(1, RW_W),
            'ssm_norm_g': ssm_norm_g[i].reshape(1, SSM_W),
            'p_attn': p_attn[i].astype(BF16), 'p_rwkv': p_rwkv[i].astype(BF16),
            'p_ssm': p_ssm[i].astype(BF16), 'w_out': w_out[i].astype(BF16),
            'ln1_g': ln1_g[i].reshape(1, D_MODEL), 'ln1_b': ln1_b[i].reshape(1, D_MODEL),
        }
        w1, w3, w2 = ffn_w1[i].astype(BF16), ffn_w3[i].astype(BF16), ffn_w2[i].astype(BF16)
        l2g, l2b = ln2_g[i].reshape(1, D_MODEL), ln2_b[i].reshape(1, D_MODEL)
        last = i == DEPTH - 1
        mod_x = (jax.nn.silu(c) @ ada_w[i] + ada_b[i])[:, None, :]
        mod_c = jnp.broadcast_to((jax.nn.silu(c_ctx) @ ada_w[i] + ada_b[i])[None, None, :], (b, 1, 6 * D_MODEL))
        sh1, sc1, g1, sh2, sc2, g2 = jnp.split(mod_x, 6, axis=-1)
        csh1, csc1, cg1, csh2, csc2, cg2 = jnp.split(mod_c, 6, axis=-1)
        xc1, side = _mixer(xc, (csh1, csc1, cg1), lp, i, None, not last, None)
        x1, _ = _mixer(x, (sh1, sc1, g1), lp, i, side, True, rope_tabs)
        x = _ffn(x1, sc2, sh2, g2, w1, w3, w2, l2g, l2b, l).reshape(b, l, D_MODEL)
        if not last:
            xc = _ffn(xc1, csc2, csh2, cg2, w1, w3, w2, l2g, l2b, lc).reshape(b, lc, D_MODEL)
    return x
```

```python
import functools
import math

import jax
import jax.numpy as jnp
from jax import lax
from jax.experimental import pallas as pl
from jax.experimental.pallas import tpu as pltpu

F32 = jnp.float32
BF16 = jnp.bfloat16

D_MODEL = 1024
DEPTH = 4
GRID_W = 64
DA_HEADS = 4
DA_HD = 64
DA_W = DA_HEADS * 2 * DA_HD
ROPE_BASE = 10000.0
RW_HEADS = 8
RW_HD = 64
RW_W = RW_HEADS * RW_HD
RW_DECAY_R = 64
RW_ICLR_R = 64
RW_GATE_R = 128
RW_GN_EPS = 64e-5
RW_COLS = 3 * RW_W + 2 * RW_DECAY_R + 2 * RW_ICLR_R + RW_GATE_R
SSM_HEADS = 8
SSM_HD = 64
SSM_W = SSM_HEADS * SSM_HD
SSM_GROUPS = 2
SSM_STATE = 128
SSM_XBC = SSM_W + 2 * SSM_GROUPS * SSM_STATE
FFN_HIDDEN = (8 * D_MODEL + 3 * 256 - 1) // (3 * 256) * 256
NORM_EPS = 1e-5
W_IN_SPLIT = (DA_W, DA_W, DA_W, RW_COLS, SSM_W, SSM_XBC, 2 * SSM_HEADS, D_MODEL, D_MODEL, D_MODEL)
ALPHA = (2.0 * DEPTH) ** 0.25

LANES = 128
SUBLANES = 8
HALO_ROWS = 16
VMEM_LIMIT_BYTES = 56 * 1024 * 1024

PROJ_COLS = 8192
RW_BLK = 2048
C_RW = 0
C_DT = RW_COLS
C_GA, C_GR, C_GS = RW_BLK, RW_BLK + D_MODEL, RW_BLK + 2 * D_MODEL
C_Q = RW_BLK + 3 * D_MODEL
C_K = C_Q + DA_W
C_V = C_K + DA_W
C_Z = C_V + DA_W
C_XBC = C_Z + SSM_W

Q_SCALE = math.log2(math.e) * DA_HD ** -0.5
ATTN_SUBTILE = 128
ATTN_BLOCK = 512
RW_CHUNK = 64
RW_BLOCK = 256
SSD_CHUNK = 128
SSD_BLOCK = 256


def _cparams(sem):
    return pltpu.CompilerParams(dimension_semantics=sem, vmem_limit_bytes=VMEM_LIMIT_BYTES)


def _dot(a, b):
    return jnp.dot(a, b, preferred_element_type=F32)


def _dot_nt(a, b):
    return lax.dot_general(a, b, (((1,), (1,)), ((), ())), preferred_element_type=F32)


def _dot_tn(a, b):
    return lax.dot_general(a, b, (((0,), (0,)), ((), ())), preferred_element_type=F32)


def _bf(x):
    return x.astype(BF16)


def _split3(x):
    hi = _bf(x)
    r1 = x - hi.astype(F32)
    mid = _bf(r1)
    lo = _bf(r1 - mid.astype(F32))
    return hi, mid, lo


def _cumsum_rows(tri_b, x):
    n = x.shape[1]
    hi, mid, lo = _split3(x)
    o = _dot(tri_b, jnp.concatenate([hi, mid, lo], axis=1))
    return o[:, :n] + o[:, n:2 * n] + o[:, 2 * n:]


def _seg_sum(x, seg_b):
    m = x.shape[0]
    hi = _bf(x)
    lo = _bf(x - hi.astype(F32))
    o = _dot(jnp.concatenate([hi, lo], axis=0), seg_b)
    return o[:m] + o[m:]


def _sigmoid(x):
    return 0.5 * jnp.tanh(0.5 * x) + 0.5


def _softplus(x):
    return jnp.maximum(x, 0.0) + jnp.log(1.0 + jnp.exp(-jnp.abs(x)))


def _layernorm_rows(t, g, b):
    mu = jnp.mean(t, -1, keepdims=True)
    d = t - mu
    var = jnp.mean(d * d, -1, keepdims=True)
    return d * lax.rsqrt(var + NORM_EPS) * g + b


def _neighbour_rows(x, prev_row, next_row):
    t = x.shape[0]
    rowi = lax.broadcasted_iota(jnp.int32, (t, 1), 0)
    x_m1 = jnp.where(rowi == 0, prev_row, pltpu.roll(x, 1, 0))
    x_p1 = jnp.where(rowi == t - 1, next_row, pltpu.roll(x, t - 1, 0))
    return x_m1, x_p1


def _halo_specs(width, col_blk, seq_len, tb, nc, n_rows):
    per8 = tb // HALO_ROWS
    last8 = n_rows // HALO_ROWS - 1

    def blk(d, bi, c):
        return bi * nc + c + d * (nc - 1 - 2 * c)

    prev = pl.BlockSpec((HALO_ROWS, width), lambda d, bi, c: (jnp.maximum(blk(d, bi, c) * per8 - 1, 0), col_blk))
    nxt = pl.BlockSpec((HALO_ROWS, width),
                       lambda d, bi, c: (jnp.minimum((blk(d, bi, c) + 1) * per8, last8), col_blk))
    cur = pl.BlockSpec((tb, width), lambda d, bi, c: (blk(d, bi, c), col_blk))
    return cur, prev, nxt


def _inproj_kernel(x_ref, sc_ref, sh_ref, w_ref, o_ref, h_scr):
    @pl.when(pl.program_id(1) == 0)
    def _():
        h = x_ref[...] * (1.0 + sc_ref[0]) + sh_ref[0]
        h_scr[...] = h.astype(BF16)

    o_ref[...] = _bf(_dot(h_scr[...], w_ref[...]))


def _inproj(x2d, sc, sh, w, seq_len):
    n = x2d.shape[0]
    tm = min(1024, seq_len)
    tn = 2048
    per_b = seq_len // tm
    return pl.pallas_call(
        _inproj_kernel,
        grid=(n // tm, PROJ_COLS // tn),
        in_specs=[
            pl.BlockSpec((tm, D_MODEL), lambda i, j: (i, 0)),
            pl.BlockSpec((1, 1, D_MODEL), lambda i, j: (i // per_b, 0, 0)),
            pl.BlockSpec((1, 1, D_MODEL), lambda i, j: (i // per_b, 0, 0)),
            pl.BlockSpec((D_MODEL, tn), lambda i, j: (0, j)),
        ],
        out_specs=pl.BlockSpec((tm, tn), lambda i, j: (i, j)),
        out_shape=jax.ShapeDtypeStruct((n, PROJ_COLS), BF16),
        scratch_shapes=[pltpu.VMEM((tm, D_MODEL), BF16)],
        compiler_params=_cparams(("arbitrary", "arbitrary")),
        name="inproj",
    )(x2d, sc, sh, w)


def _rope_lanes(x, cos, s_dn, s_up):
    nf = DA_HD // 4
    return x * cos + pltpu.roll(x, LANES - nf, 1) * s_dn + pltpu.roll(x, nf, 1) * s_up


def _attn_core(lam, q, k_scr, v_scr, g_ref, o_ref, out_scale):
    tq = ATTN_SUBTILE
    nsub = q.shape[0] // tq
    lk = k_scr.shape[0]
    lane = lax.broadcasted_iota(jnp.int32, (1, LANES), 1)
    mid = (lk // 2) // LANES * LANES
    cuts = (0, mid, lk) if mid else (0, lk)
    spans = list(zip(cuts[:-1], cuts[1:]))
    scores = []
    for t in range(nsub):
        qt = q[t * tq:(t + 1) * tq]
        zero = jnp.zeros_like(qt)
        qq = jnp.concatenate([jnp.where(lane < DA_HD, qt, zero), jnp.where(lane >= DA_HD, qt, zero)], axis=0)
        scores.append([_dot_nt(qq, k_scr[lo:hi, :]) for lo, hi in spans])
    for t, ss in enumerate(scores):
        m = functools.reduce(jnp.maximum, [jnp.max(s, -1, keepdims=True) for s in ss])
        ov = functools.reduce(jnp.add, [_dot(_bf(jnp.exp2(s - m)), v_scr[lo:hi, :]) for s, (lo, hi) in zip(ss, spans)])
        on = ov[:, :LANES] / ov[:, LANES:]
        o = on[:tq] - lam * on[tq:]
        ms = jnp.mean(o * o, -1, keepdims=True)
        o_ref[t * tq:(t + 1) * tq, :] = _bf(o * lax.rsqrt(ms + NORM_EPS) * (g_ref[...] * out_scale))


def _attn_latent_kernel(lam_ref, q_ref, kl_ref, vl_ref, kc_ref, vc_ref, cos_ref, sdn_ref, sup_ref, g_ref,
                        o_ref, k_scr, v_scr, *, out_scale):
    i = pl.program_id(2)
    tq = q_ref.shape[0]
    lc = kc_ref.shape[0]

    @pl.when(i == 0)
    def _():
        k_scr[:lc, :] = kc_ref[...]
        k_scr[lc:, :] = _bf(_rope_lanes(kl_ref[...].astype(F32), cos_ref[...], sdn_ref[...], sup_ref[...]))
        v_scr[:lc, :LANES] = vc_ref[...]
        v_scr[lc:, :LANES] = vl_ref[...]
        v_scr[:, LANES:] = jnp.ones((v_scr.shape[0], LANES), BF16)

    rows = pl.ds(pl.multiple_of(i * tq, tq), tq)
    q = _rope_lanes(q_ref[...].astype(F32), cos_ref[rows, :], sdn_ref[rows, :], sup_ref[rows, :])
    _attn_core(lam_ref[0], _bf(q * Q_SCALE), k_scr, v_scr, g_ref, o_ref, out_scale)


def _attn_ctx_kernel(lam_ref, q_ref, kl_ref, vl_ref, g_ref, o_ref, k_scr, v_scr, *, out_scale):
    @pl.when(pl.program_id(2) == 0)
    def _():
        k_scr[...] = kl_ref[...]
        v_scr[:, :LANES] = vl_ref[...]
        v_scr[:, LANES:] = jnp.ones((v_scr.shape[0], LANES), BF16)

    _attn_core(lam_ref[0], _bf(q_ref[...].astype(F32) * Q_SCALE), k_scr, v_scr, g_ref, o_ref, out_scale)


def _diff_attention(lam, proj, batch, seq_len, g, out_scale, proj_ctx=None, ctx_len=0, rope_tabs=None):
    tq = min(ATTN_BLOCK, seq_len)
    nq = seq_len // tq
    lk = seq_len + ctx_len
    cq, ck, cv = C_Q // LANES, C_K // LANES, C_V // LANES
    smem = pl.BlockSpec(memory_space=pltpu.SMEM)
    qspec = pl.BlockSpec((tq, LANES), lambda bi, h, i: (bi * nq + i, cq + h))
    kspec = pl.BlockSpec((seq_len, LANES), lambda bi, h, i: (bi, ck + h))
    vspec = pl.BlockSpec((seq_len, LANES), lambda bi, h, i: (bi, cv + h))
    gspec = pl.BlockSpec((1, LANES), lambda bi, h, i: (0, 0))
    common = dict(
        grid=(batch, DA_HEADS, nq),
        out_specs=pl.BlockSpec((tq, LANES), lambda bi, h, i: (bi * nq + i, h)),
        out_shape=jax.ShapeDtypeStruct((batch * seq_len, DA_W), BF16),
        scratch_shapes=[pltpu.VMEM((lk, LANES), BF16), pltpu.VMEM((lk, 2 * LANES), BF16)],
        compiler_params=_cparams(("arbitrary", "arbitrary", "arbitrary")),
    )
    if proj_ctx is None:
        return pl.pallas_call(
            functools.partial(_attn_ctx_kernel, out_scale=out_scale),
            in_specs=[smem, qspec, kspec, vspec, gspec], name="diff_attn_ctx", **common,
        )(lam, proj, proj, proj, g)
    tab = pl.BlockSpec((seq_len, LANES), lambda bi, h, i: (0, 0))
    return pl.pallas_call(
        functools.partial(_attn_latent_kernel, out_scale=out_scale),
        in_specs=[smem, qspec, kspec, vspec,
                  pl.BlockSpec((ctx_len, LANES), lambda bi, h, i: (bi, ck + h)),
                  pl.BlockSpec((ctx_len, LANES), lambda bi, h, i: (bi, cv + h)),
                  tab, tab, tab, gspec],
        name="diff_attn", **common,
    )(lam, proj, proj, proj, proj_ctx, proj_ctx, *rope_tabs, g)


def _rwkv_kernel(u_ref, up_ref, un_ref, mu_ref, w0_ref, w2_ref, a0_ref, a2_ref, g2_ref, vec_ref, seg_ref,
                 s0_ref, y_ref, aux_ref, sf_ref, s_scr):
    d = pl.program_id(0)
    c = pl.program_id(2)
    nc = pl.num_programs(2)

    @pl.when(c == 0)
    def _():
        s_scr[...] = s0_ref[0, 0]

    refs = (u_ref, up_ref, un_ref, mu_ref, w0_ref, w2_ref, a0_ref, a2_ref, g2_ref, vec_ref, seg_ref,
            y_ref, aux_ref, s_scr)

    @pl.when(d == 0)
    def _():
        _rwkv_step(True, c, nc, *refs)

    @pl.when(d == 1)
    def _():
        _rwkv_step(False, nc - 1 - c, nc, *refs)

    @pl.when(c == nc - 1)
    def _():
        sf_ref[0, 0] = s_scr[...]


def _rwkv_step(fwd, cc, nc, u_ref, up_ref, un_ref, mu_ref, w0_ref, w2_ref, a0_ref, a2_ref, g2_ref, vec_ref,
               seg_ref, y_ref, aux_ref, s_scr):
    t = RW_CHUNK
    tb = u_ref.shape[0]
    nper = tb // t
    di = 0 if fwd else 1

    u = u_ref[:, :RW_COLS].astype(F32)
    prev = up_ref[HALO_ROWS - 1:HALO_ROWS, :RW_COLS].astype(F32) * (cc > 0).astype(F32)
    nxt = un_ref[0:1, :RW_COLS].astype(F32) * (cc < nc - 1).astype(F32)
    u_m1, u_p1 = _neighbour_rows(u, prev, nxt)
    u = u + mu_ref[:, :RW_COLS] * (0.5 * (u_m1 + u_p1) - u)
    r = r_all = u[:, :RW_W]
    k = u[:, RW_W:2 * RW_W]
    v = v_all = u[:, 2 * RW_W:3 * RW_W]
    o = 3 * RW_W
    wd = _bf(jnp.tanh(u[:, o:o + 2 * RW_DECAY_R]))
    o += 2 * RW_DECAY_R
    ad = _bf(u[:, o:o + 2 * RW_ICLR_R])
    o += 2 * RW_ICLR_R
    gd = u[:, o:o + RW_GATE_R]
    kk_gain, ka, rk = vec_ref[0:1, :], vec_ref[1:2, :], vec_ref[2:3, :]
    seg = seg_ref[...]
    lw_all = -math.exp(-0.5) * _sigmoid(w0_ref[0] + _dot(wd, w2_ref[0]))
    a_all = _sigmoid(a0_ref[di:di + 1, :] + _dot(ad, a2_ref[di]))
    kk_all = k * kk_gain
    kk_all = kk_all * lax.rsqrt(jnp.maximum(_seg_sum(kk_all * kk_all, seg), 1e-24))
    kd_all = k * (1.0 + (a_all - 1.0) * ka)
    if fwd:
        a_other = _sigmoid(a0_ref[1:2, :] + _dot(ad, a2_ref[1]))
        k_bonus = k * (1.0 + (0.5 * (a_all + a_other) - 1.0) * ka)
        aux_ref[0, 0] = _bf(_seg_sum(r * k_bonus * rk, seg) * v)
    else:
        aux_ref[0, 0] = _bf(_dot(_bf(_sigmoid(gd)), g2_ref[...]))

    sign = 1 if fwd else -1
    row = lax.broadcasted_iota(jnp.int32, (t, t), 0)
    col = lax.broadcasted_iota(jnp.int32, (t, t), 1)
    tri_b = ((row - col) * sign >= 0).astype(BF16)
    row2 = lax.broadcasted_iota(jnp.int32, (t, 2 * t), 0)
    col2 = lax.broadcasted_iota(jnp.int32, (t, 2 * t), 1)
    col2 = jnp.where(col2 >= t, col2 - t, col2)
    delta2 = (row2 - col2) * sign
    strict2 = delta2 > 0
    incl2 = delta2 >= 0
    colh = lax.broadcasted_iota(jnp.int32, (1, 2 * t), 1)
    cm0 = (colh < t).astype(F32)
    cm1 = 1.0 - cm0
    lane = lax.broadcasted_iota(jnp.int32, (1, LANES), 1)
    m0 = (lane < RW_HD).astype(F32)
    m1 = 1.0 - m0
    rb = lax.broadcasted_iota(jnp.int32, (LANES, LANES), 0) // RW_HD
    cb = lax.broadcasted_iota(jnp.int32, (LANES, LANES), 1) // RW_HD
    bd = (rb == cb).astype(F32)
    eye = (lax.broadcasted_iota(jnp.int32, (2 * t, 2 * t), 0)
           == lax.broadcasted_iota(jnp.int32, (2 * t, 2 * t), 1)).astype(F32)

    def stack2(x):
        return _bf(jnp.concatenate([x * m0, x * m1], axis=0))

    nhp = RW_W // LANES
    inst = [(j, i) for i in range(nper) for j in range(nhp)]

    def rows_of(i):
        ip = i if fwd else nper - 1 - i
        return slice(ip * t, (ip + 1) * t)

    def sl_of(j):
        return slice(j * LANES, (j + 1) * LANES)

    lws = [lw_all[rows_of(i), sl_of(j)] for j, i in inst]
    cums = [_cumsum_rows(tri_b, lw) for lw in lws]
    pre = []
    for (j, i), lw, cum in zip(inst, lws, cums):
        rows, sl = rows_of(i), sl_of(j)
        r = r_all[rows, sl]
        v = v_all[rows, sl]
        kk = kk_all[rows, sl]
        kd = kd_all[rows, sl]
        a = a_all[rows, sl]
        tot = jnp.sum(lw, axis=0, keepdims=True)
        g_inv = jnp.exp(-cum)
        g_rem = jnp.exp(tot - cum)
        rt = r * jnp.exp(cum)
        kt = _bf(kk * jnp.exp(cum - lw))
        b = kk * a
        bh = b * g_inv
        kh = kd * g_inv
        ystack = _bf(jnp.concatenate([bh * m0, bh * m1, kh * m0, kh * m1], axis=0))
        pre.append(dict(v=v, rt=rt, kt=kt, ystack=ystack, g_tot=jnp.exp(tot), vst=stack2(v),
                        rhs_r=_bf(jnp.concatenate([kd * g_rem, b * g_rem], axis=0))))
    ggs = [_dot_nt(jnp.concatenate([p['kt'], _bf(p['rt'])], axis=0), p['ystack']) for p in pre]
    for p, gg in zip(pre, ggs):
        mcat = jnp.where(strict2, gg[:t, :2 * t], 0.0)
        p['ncat'] = _bf(jnp.where(strict2, gg[:t, 2 * t:], 0.0))
        p['qcat'] = _bf(jnp.where(incl2, gg[t:, :2 * t], 0.0))
        p['pcat'] = _bf(jnp.where(incl2, gg[t:, 2 * t:], 0.0))
        p['m'] = jnp.concatenate([mcat * cm0, mcat * cm1], axis=0)
    ri = lax.broadcasted_iota(jnp.int32, (2 * t, 2 * t), 0)
    ci = lax.broadcasted_iota(jnp.int32, (2 * t, 2 * t), 1)
    same_head = (ri // t) == (ci // t)
    lo_i, hi_i = (ci, ri) if fwd else (ri, ci)

    def level_mask(sz):
        return (same_head & ((ri % t) // (2 * sz) == (ci % t) // (2 * sz))
                & ((hi_i % (2 * sz)) >= sz) & ((lo_i % (2 * sz)) < sz))

    nvs = [_dot(p['ncat'], p['vst']) for p in pre]
    ws = [eye - jnp.where(level_mask(1), p['m'], 0.0) for p in pre]
    sz = 2
    while sz < t:
        lm = level_mask(sz)
        zs = [_dot(_bf(w), _bf(jnp.where(lm, p['m'], 0.0))) for w, p in zip(ws, pre)]
        ws = [w - _dot(_bf(z), _bf(w)) for w, z in zip(ws, zs)]
        sz *= 2
    wcats = [_bf(w[:t] + w[t:]) for w in ws]
    uks = [_dot(wc, jnp.concatenate([stack2(nv), stack2(p['kt'].astype(F32))], axis=1))
           for wc, nv, p in zip(wcats, nvs, pre)]
    qus = [_dot(p['qcat'], jnp.concatenate([stack2(uk[:, :LANES]), stack2(uk[:, LANES:])], axis=1))
           for p, uk in zip(pre, uks)]
    pvs = [_dot(p['pcat'], p['vst']) for p in pre]
    zeros_t = jnp.zeros((t, LANES), F32)
    dcs = [_dot_tn(p['rhs_r'], _bf(jnp.concatenate([jnp.concatenate([p['v'], -uk[:, :LANES]], axis=0),
                                                    jnp.concatenate([zeros_t, uk[:, LANES:]], axis=0)], axis=1)))
           for p, uk in zip(pre, uks)]
    for p, qu, pv, dc in zip(pre, qus, pvs, dcs):
        p['y0'] = pv - qu[:, :LANES]
        p['lhs'] = jnp.concatenate([_bf(p['rt'] - qu[:, LANES:]), _bf(bd * dc[:, LANES:])], axis=0)
        p['d0t'] = bd * dc[:, :LANES]
        p['g_col'] = jnp.broadcast_to(p['g_tot'], (LANES, LANES)).T
    states = [s_scr[j] for j in range(nhp)]
    for i in range(nper):
        ps = [pre[i * nhp + j] for j in range(nhp)]
        outs = [_dot(p['lhs'], _bf(st)) for p, st in zip(ps, states)]
        for j in range(nhp):
            y_ref[0, 0, rows_of(i), sl_of(j)] = _bf(ps[j]['y0'] + outs[j][:t])
        states = [st * p['g_col'] - o[t:] + p['d0t'] for st, p, o in zip(states, ps, outs)]
    for j in range(nhp):
        s_scr[j] = states[j]


def _rwkv_scan(proj, batch, seq_len, rwp, s0):
    n = proj.shape[0]
    tb = min(RW_BLOCK, seq_len)
    nc = seq_len // tb
    nhp = RW_W // LANES
    cur, prev, nxt = _halo_specs(RW_BLK, C_RW // RW_BLK, seq_len, tb, nc, n)
    full2 = lambda d, bi, c: (0, 0)
    full3 = lambda d, bi, c: (0, 0, 0)
    dmap = lambda d, bi, c: (d, bi, c + d * (nc - 1 - 2 * c), 0)
    smap = lambda d, bi, c: (d, bi, 0, 0, 0)
    return pl.pallas_call(
        _rwkv_kernel,
        grid=(2, batch, nc),
        in_specs=[
            cur, prev, nxt,
            pl.BlockSpec((1, RW_BLK), full2),
            pl.BlockSpec((1, 1, RW_W), lambda d, bi, c: (d, 0, 0)),
            pl.BlockSpec((1, 2 * RW_DECAY_R, RW_W), lambda d, bi, c: (d, 0, 0)),
            pl.BlockSpec((2, RW_W), full2),
            pl.BlockSpec((2, 2 * RW_ICLR_R, RW_W), full3),
            pl.BlockSpec((RW_GATE_R, RW_W), full2),
            pl.BlockSpec((3, RW_W), full2),
            pl.BlockSpec((RW_W, RW_W), full2),
            pl.BlockSpec((1, 1, nhp, LANES, LANES), smap),
        ],
        out_specs=[
            pl.BlockSpec((1, 1, tb, RW_W), dmap),
            pl.BlockSpec((1, 1, tb, RW_W), dmap),
            pl.BlockSpec((1, 1, nhp, LANES, LANES), smap),
        ],
        out_shape=[
            jax.ShapeDtypeStruct((2, batch, seq_len, RW_W), BF16),
            jax.ShapeDtypeStruct((2, batch, seq_len, RW_W), BF16),
            jax.ShapeDtypeStruct((2, batch, nhp, LANES, LANES), F32),
        ],
        scratch_shapes=[pltpu.VMEM((nhp, LANES, LANES), F32)],
        compiler_params=_cparams(("arbitrary", "arbitrary", "arbitrary")),
        name="rwkv_scan",
    )(proj, proj, proj, rwp['mu'], rwp['w0'], rwp['w2'], rwp['a0'], rwp['a2'], rwp['g2'], rwp['vec'],
      rwp['seg'], s0)


def _ssd_kernel(x_ref, xp_ref, xn_ref, dt_ref, cw_ref, cb_ref, dtb_ref, alog_ref, dskip_ref, s0_ref,
                y_ref, sf_ref, s_scr):
    d = pl.program_id(0)
    c = pl.program_id(2)
    nc = pl.num_programs(2)

    @pl.when(c == 0)
    def _():
        s_scr[...] = s0_ref[0, 0]

    refs = (x_ref, xp_ref, xn_ref, dt_ref, cw_ref, cb_ref, dtb_ref, alog_ref, dskip_ref, y_ref, s_scr)

    @pl.when(d == 0)
    def _():
        _ssd_step(True, c, nc, *refs)

    @pl.when(d == 1)
    def _():
        _ssd_step(False, nc - 1 - c, nc, *refs)

    @pl.when(c == nc - 1)
    def _():
        sf_ref[0, 0] = s_scr[...]


def _ssd_step(fwd, cc, nc, x_ref, xp_ref, xn_ref, dt_ref, cw_ref, cb_ref, dtb_ref, alog_ref, dskip_ref,
              y_ref, s_scr):
    t = SSD_CHUNK
    gw = SSM_W // SSM_GROUPS
    hpg = SSM_HEADS // SSM_GROUPS
    nh = SSM_HEADS

    x = x_ref[...].astype(F32)
    prev = xp_ref[HALO_ROWS - 1:HALO_ROWS, :].astype(F32) * (cc > 0).astype(F32)
    nxt = xn_ref[0:1, :].astype(F32) * (cc < nc - 1).astype(F32)
    x_m1, x_p1 = _neighbour_rows(x, prev, nxt)
    xc = x_m1 * cw_ref[0:1, :] + x * cw_ref[1:2, :] + x_p1 * cw_ref[2:3, :] + cb_ref[...]
    xc = xc * _sigmoid(xc)
    xs = xc[:, :SSM_W]
    bm = xc[:, SSM_W:SSM_W + SSM_GROUPS * SSM_STATE]
    cm = xc[:, SSM_W + SSM_GROUPS * SSM_STATE:]
    dt2 = _softplus(dt_ref[...].astype(F32)[:, :2 * nh] + dtb_ref[...])
    a2 = dt2 * (-jnp.exp(alog_ref[...]))
    dt_all = dt2[:, :nh] if fwd else dt2[:, nh:]
    a_all = a2[:, :nh] if fwd else a2[:, nh:]

    sign = 1 if fwd else -1
    row = lax.broadcasted_iota(jnp.int32, (t, t), 0)
    col = lax.broadcasted_iota(jnp.int32, (t, t), 1)
    delta = (row - col) * sign
    incl = delta >= 0
    tri_b = incl.astype(BF16)
    tri_t_b = (delta <= 0).astype(BF16)
    lane = lax.broadcasted_iota(jnp.int32, (1, LANES), 1)
    m0 = (lane < SSM_HD).astype(F32)
    m1 = 1.0 - m0
    lane_g = lax.broadcasted_iota(jnp.int32, (1, gw), 1) // SSM_HD

    def expand(z, g):
        out = jnp.zeros((z.shape[0], gw), F32)
        for hh in range(hpg):
            h = g * hpg + hh
            out = jnp.where(lane_g == hh, z[:, h:h + 1], out)
        return out

    nper = x_ref.shape[0] // t
    states = [s_scr[g] for g in range(SSM_GROUPS)]
    for i in range(nper):
        ip = i if fwd else nper - 1 - i
        rows = slice(ip * t, (ip + 1) * t)
        dt = dt_all[rows]
        a = a_all[rows]
        a3 = jnp.concatenate(_split3(a), axis=1)
        cs3 = _dot(tri_b, a3)
        cs = cs3[:, :nh] + cs3[:, nh:2 * nh] + cs3[:, 2 * nh:]
        cs3_t = _dot_tn(a3, tri_t_b)
        cs_t = cs3_t[:nh] + cs3_t[nh:2 * nh] + cs3_t[2 * nh:]
        tot = jnp.sum(a, axis=0, keepdims=True)
        e_cs = jnp.exp(cs)
        e_rem = jnp.exp(tot - cs)
        e_tot = jnp.exp(tot)
        for g in range(SSM_GROUPS):
            cg = cm[rows, g * SSM_STATE:(g + 1) * SSM_STATE]
            bg = bm[rows, g * SSM_STATE:(g + 1) * SSM_STATE]
            xsg = xs[rows, g * gw:(g + 1) * gw]
            xg = xsg * expand(dt, g)
            cgb = _bf(cg)
            bgb = _bf(bg)
            gm = _dot_nt(cgb, bgb)
            st = states[g]
            y_off = _dot(cgb, _bf(st)) * expand(e_cs, g)
            ys = []
            for hp in range(hpg // 2):
                scs = []
                for hh in range(2):
                    h = g * hpg + 2 * hp + hh
                    diff = jnp.where(incl, cs[:, h:h + 1] - cs_t[h:h + 1, :], 0.0)
                    scs.append(gm * jnp.where(incl, jnp.exp(diff), 0.0))
                xp = xg[:, hp * LANES:(hp + 1) * LANES]
                xst = _bf(jnp.concatenate([xp * m0, xp * m1], axis=0))
                ys.append(_dot(_bf(jnp.concatenate(scs, axis=1)), xst))
            y = jnp.concatenate(ys, axis=1) + y_off
            if fwd:
                y = y + dskip_ref[:, g * gw:(g + 1) * gw] * xsg
            y_ref[0, 0, rows, g * gw:(g + 1) * gw] = _bf(y)
            states[g] = st * expand(e_tot, g) + _dot_tn(bgb, _bf(xg * expand(e_rem, g)))
    for g in range(SSM_GROUPS):
        s_scr[g] = states[g]


def _ssd_scan(proj, batch, seq_len, ssp, s0):
    n = proj.shape[0]
    t = min(SSD_BLOCK, seq_len)
    nc = seq_len // t
    gw = SSM_W // SSM_GROUPS
    cur, prev, nxt = _halo_specs(SSM_XBC, C_XBC // SSM_XBC, seq_len, t, nc, n)
    dtspec, _, _ = _halo_specs(LANES, C_DT // LANES, seq_len, t, nc, n)
    full2 = lambda d, bi, c: (0, 0)
    dmap = lambda d, bi, c: (d, bi, c + d * (nc - 1 - 2 * c), 0)
    smap = lambda d, bi, c: (d, bi, 0, 0, 0)
    return pl.pallas_call(
        _ssd_kernel,
        grid=(2, batch, nc),
        in_specs=[
            cur, prev, nxt, dtspec,
            pl.BlockSpec((3, SSM_XBC), full2),
            pl.BlockSpec((1, SSM_XBC), full2),
            pl.BlockSpec((1, 2 * SSM_HEADS), full2),
            pl.BlockSpec((1, 2 * SSM_HEADS), full2),
            pl.BlockSpec((1, SSM_W), full2),
            pl.BlockSpec((1, 1, SSM_GROUPS, SSM_STATE, gw), smap),
        ],
        out_specs=[
            pl.BlockSpec((1, 1, t, SSM_W), dmap),
            pl.BlockSpec((1, 1, SSM_GROUPS, SSM_STATE, gw), smap),
        ],
        out_shape=[
            jax.ShapeDtypeStruct((2, batch, seq_len, SSM_W), BF16),
            jax.ShapeDtypeStruct((2, batch, SSM_GROUPS, SSM_STATE, gw), F32),
        ],
        scratch_shapes=[pltpu.VMEM((SSM_GROUPS, SSM_STATE, gw), F32)],
        compiler_params=_cparams(("arbitrary", "arbitrary", "arbitrary")),
        name="ssd_scan",
    )(proj, proj, proj, proj, ssp['conv_w'], ssp['conv_b'], ssp['dt_bias'], ssp['a_log'], ssp['dskip'], s0)


def _merge_kernel(ya_ref, yr_ref, aux_ref, ys_ref, z_ref, ga_ref, gr_ref, gs_ref, x_ref, g1_ref,
                  seg_ref, rg_ref, rb_ref, sg_ref, pa_ref, pr_ref, ps_ref, wo_ref, lg_ref, lb_ref, o_ref):
    seg = seg_ref[...]
    yr = yr_ref[0].astype(F32) + yr_ref[1].astype(F32)
    mu = _seg_sum(yr, seg) * (1.0 / RW_HD)
    dv = yr - mu
    var = _seg_sum(dv * dv, seg) * (1.0 / RW_HD)
    yr = (dv * lax.rsqrt(var + RW_GN_EPS) * rg_ref[...] + rb_ref[...] + aux_ref[0].astype(F32)) * aux_ref[1].astype(F32)
    z = z_ref[...].astype(F32)
    ys = (ys_ref[0].astype(F32) + ys_ref[1].astype(F32)) * (z * _sigmoid(z))
    gw = SSM_W // SSM_GROUPS
    parts = []
    for g in range(SSM_GROUPS):
        yg = ys[:, g * gw:(g + 1) * gw]
        parts.append(yg * lax.rsqrt(jnp.mean(yg * yg, -1, keepdims=True) + NORM_EPS))
    ys = jnp.concatenate(parts, axis=1) * sg_ref[...]
    m = (_sigmoid(ga_ref[...].astype(F32)) * _dot(ya_ref[...], pa_ref[...])
         + _sigmoid(gr_ref[...].astype(F32)) * _dot(_bf(yr), pr_ref[...])
         + _sigmoid(gs_ref[...].astype(F32)) * _dot(_bf(ys), ps_ref[...]))
    out = _dot(_bf(m), wo_ref[...])
    o_ref[...] = _layernorm_rows(ALPHA * x_ref[...] + g1_ref[0] * out, lg_ref[...], lb_ref[...])


def _merge(ya, yr2, aux, ys2, proj, x2d, g1, lp, seq_len):
    n = x2d.shape[0]
    tm = min(512, seq_len)
    per_b = seq_len // tm
    row = lambda i: (i, 0)
    row3 = lambda i: (0, i, 0)
    full = lambda i: (0, 0)
    gate = lambda col: pl.BlockSpec((tm, D_MODEL), lambda i: (i, col // D_MODEL))
    vec512 = pl.BlockSpec((1, RW_W), full)
    vec = pl.BlockSpec((1, D_MODEL), full)
    return pl.pallas_call(
        _merge_kernel,
        grid=(n // tm,),
        in_specs=[
            pl.BlockSpec((tm, DA_W), row),
            pl.BlockSpec((2, tm, RW_W), row3), pl.BlockSpec((2, tm, RW_W), row3),
            pl.BlockSpec((2, tm, SSM_W), row3),
            pl.BlockSpec((tm, SSM_W), lambda i: (i, C_Z // SSM_W)),
            gate(C_GA), gate(C_GR), gate(C_GS),
            pl.BlockSpec((tm, D_MODEL), row),
            pl.BlockSpec((1, 1, D_MODEL), lambda i: (i // per_b, 0, 0)),
            pl.BlockSpec((RW_W, RW_W), full), vec512, vec512, vec512,
            pl.BlockSpec((DA_W, D_MODEL), full), pl.BlockSpec((RW_W, D_MODEL), full),
            pl.BlockSpec((SSM_W, D_MODEL), full), pl.BlockSpec((D_MODEL, D_MODEL), full),
            vec, vec,
        ],
        out_specs=pl.BlockSpec((tm, D_MODEL), row),
        out_shape=jax.ShapeDtypeStruct((n, D_MODEL), F32),
        compiler_params=_cparams(("arbitrary",)),
        name="merge_ln",
    )(ya, yr2, aux, ys2, proj, proj, proj, proj, x2d, g1, lp['rw']['seg'], lp['rw_ln_g'], lp['rw_ln_b'],
      lp['ssm_norm_g'], lp['p_attn'], lp['p_rwkv'], lp['p_ssm'], lp['w_out'], lp['ln1_g'], lp['ln1_b'])


def _ffn_kernel(x_ref, sc_ref, sh_ref, g2_ref, w1_ref, w3_ref, w2_ref, lg_ref, lb_ref, o_ref, h_scr, acc_scr):
    j = pl.program_id(1)

    @pl.when(j == 0)
    def _():
        h_scr[...] = (x_ref[...] * (1.0 + sc_ref[0]) + sh_ref[0]).astype(BF16)
        acc_scr[...] = jnp.zeros_like(acc_scr)

    h = h_scr[...]
    a = _dot(h, w1_ref[...])
    u = a * _sigmoid(a) * _dot(h, w3_ref[...])
    acc_scr[...] += _dot(u.astype(BF16), w2_ref[...])

    @pl.when(j == pl.num_programs(1) - 1)
    def _():
        o_ref[...] = _layernorm_rows(ALPHA * x_ref[...] + g2_ref[0] * acc_scr[...], lg_ref[...], lb_ref[...])


def _ffn(x2d, sc, sh, g2, w1, w3, w2, lg, lb, seq_len):
    n = x2d.shape[0]
    tm = min(512, seq_len)
    th = FFN_HIDDEN // 2
    per_b = seq_len // tm
    mod = pl.BlockSpec((1, 1, D_MODEL), lambda i, j: (i // per_b, 0, 0))
    return pl.pallas_call(
        _ffn_kernel,
        grid=(n // tm, FFN_HIDDEN // th),
        in_specs=[
            pl.BlockSpec((tm, D_MODEL), lambda i, j: (i, 0)),
            mod, mod, mod,
            pl.BlockSpec((D_MODEL, th), lambda i, j: (0, j)),
            pl.BlockSpec((D_MODEL, th), lambda i, j: (0, j)),
            pl.BlockSpec((th, D_MODEL), lambda i, j: (j, 0)),
            pl.BlockSpec((1, D_MODEL), lambda i, j: (0, 0)),
            pl.BlockSpec((1, D_MODEL), lambda i, j: (0, 0)),
        ],
        out_specs=pl.BlockSpec((tm, D_MODEL), lambda i, j: (i, 0)),
        out_shape=jax.ShapeDtypeStruct((n, D_MODEL), F32),
        scratch_shapes=[pltpu.VMEM((tm, D_MODEL), BF16), pltpu.VMEM((tm, D_MODEL), F32)],
        compiler_params=_cparams(("arbitrary", "arbitrary")),
        name="ffn_ln",
    )(x2d, sc, sh, g2, w1, w3, w2, lg, lb)


def _rope_tables(n_tok):
    rows = n_tok // GRID_W
    row = jnp.repeat(jnp.arange(rows), GRID_W).astype(F32)
    col = jnp.tile(jnp.arange(GRID_W), rows).astype(F32)
    nf = DA_HD // 4
    inv = ROPE_BASE ** (-jnp.arange(nf, dtype=F32) / nf)
    ang = jnp.concatenate([row[:, None] * inv] * 2 + [col[:, None] * inv] * 2, axis=-1)
    cos = jnp.cos(ang)
    sin = jnp.sin(ang)
    first = (jnp.arange(DA_HD) % (2 * nf)) < nf
    s_dn = jnp.where(first, -sin, 0.0)
    s_up = jnp.where(first, 0.0, sin)
    reps = LANES // DA_HD
    return jnp.tile(cos, (1, reps)), jnp.tile(s_dn, (1, reps)), jnp.tile(s_up, (1, reps))


def _permute_w_in(w_in):
    parts, acc = [], 0
    for s in W_IN_SPLIT:
        parts.append(w_in[:, acc:acc + s])
        acc += s
    q, k, v, rw, z, xbc, dt, ga, gr, gs = parts
    pad = jnp.zeros((w_in.shape[0], RW_BLK - RW_COLS - dt.shape[1]), w_in.dtype)
    return jnp.concatenate([rw, dt, pad, ga, gr, gs, q, k, v, z, xbc], axis=1).astype(BF16)


def _pad_lora(w, rank):
    z = jnp.zeros_like(w[0])
    return jnp.stack([jnp.concatenate([w[0], z], axis=0), jnp.concatenate([z, w[1]], axis=0)]).astype(BF16)


def _mixer(x, mod, lp, layer, side, want_out, rope_tabs):
    b, l, _ = x.shape
    sh1, sc1, g1 = mod
    n = b * l
    x2d = x.reshape(n, D_MODEL)
    proj = _inproj(x2d, sc1, sh1, lp['w_in_p'], l)
    if side is None:
        proj_ctx, ctx_len = None, 0
        s0_r = jnp.zeros((2, b, RW_W // LANES, LANES, LANES), F32)
        s0_s = jnp.zeros((2, b, SSM_GROUPS, SSM_STATE, SSM_W // SSM_GROUPS), F32)
    else:
        (proj_ctx, ctx_len), s0_r, s0_s = side
    y_r2, aux, sf_r = _rwkv_scan(proj, b, l, lp['rw'], s0_r)
    y_s2, sf_s = _ssd_scan(proj, b, l, lp['ssm'], s0_s)
    new_side = ((proj, l), sf_r, sf_s)
    if not want_out:
        return None, new_side
    lam_init = 0.8 - 0.6 * math.exp(-0.3 * layer)
    lam = (jnp.exp(jnp.sum(lp['da_lq1'] * lp['da_lk1'])) - jnp.exp(jnp.sum(lp['da_lq2'] * lp['da_lk2']))
           + lam_init).reshape(1)
    y_a = _diff_attention(lam, proj, b, l, lp['da_norm_g'].reshape(1, 2 * DA_HD), 1.0 - lam_init,
                          proj_ctx, ctx_len, rope_tabs)
    x1 = _merge(y_a, y_r2.reshape(2, n, RW_W), aux.reshape(2, n, RW_W),
                y_s2.reshape(2, n, SSM_W), proj, x2d, g1, lp, l)
    return x1, new_side


def kernel(x, c, ctx, c_ctx, ada_w, ada_b, w_in, da_lq1, da_lk1, da_lq2, da_lk2, da_norm_g,
           rw_mu, rw_w0, rw_w2, rw_a0, rw_a2, rw_g2, rw_kk, rw_ka, rw_rk, rw_ln_g, rw_ln_b,
           ssm_conv_w, ssm_conv_b, ssm_dt_bias, ssm_a_log, ssm_d, ssm_norm_g,
           p_attn, p_rwkv, p_ssm, w_out, ln1_g, ln1_b, ffn_w1, ffn_w3, ffn_w2, ln2_g, ln2_b):
    b, l, _ = x.shape
    lc = ctx.shape[1]
    rope_tabs = _rope_tables(l)
    seg = (jnp.arange(RW_W)[:, None] // RW_HD == jnp.arange(RW_W)[None, :] // RW_HD).astype(BF16)
    xc = ctx
    for i in range(DEPTH):
        rw = {
            'mu': jnp.pad(rw_mu[i], (0, RW_BLK - RW_COLS)).reshape(1, RW_BLK),
            'w0': rw_w0[i].reshape(2, 1, RW_W),
            'w2': _pad_lora(rw_w2[i], RW_DECAY_R),
            'a0': rw_a0[i],
            'a2': _pad_lora(rw_a2[i], RW_ICLR_R),
            'g2': rw_g2[i].astype(BF16),
            'vec': jnp.stack([rw_kk[i], rw_ka[i], rw_rk[i].reshape(RW_W)]),
            'seg': seg,
        }
        ssm = {
            'conv_w': ssm_conv_w[i], 'conv_b': ssm_conv_b[i].reshape(1, SSM_XBC),
            'dt_bias': ssm_dt_bias[i].reshape(1, 2 * SSM_HEADS), 'a_log': ssm_a_log[i].reshape(1, 2 * SSM_HEADS),
            'dskip': jnp.repeat(ssm_d[i], SSM_HD).reshape(1, SSM_W),
        }
        lp = {
            'w_in_p': _permute_w_in(w_in[i]), 'rw': rw, 'ssm': ssm,
            'da_lq1': da_lq1[i], 'da_lk1': da_lk1[i], 'da_lq2': da_lq2[i], 'da_lk2': da_lk2[i],
            'da_norm_g': da_norm_g[i],
            'rw_ln_g': rw_ln_g[i].reshape(1, RW_W), 'rw_ln_b': rw_ln_b[i].reshape(1, RW_W),
            'ssm_norm_g': ssm_norm_g[i].reshape(1, SSM_W),
            'p_attn': p_attn[i].astype(BF16), 'p_rwkv': p_rwkv[i].astype(BF16),
            'p_ssm': p_ssm[i].astype(BF16), 'w_out': w_out[i].astype(BF16),
            'ln1_g': ln1_g[i].reshape(1, D_MODEL), 'ln1_b': ln1_b[i].reshape(1, D_MODEL),
        }
        w1, w3, w2 = ffn_w1[i].astype(BF16), ffn_w3[i].astype(BF16), ffn_w2[i].astype(BF16)
        l2g, l2b = ln2_g[i].reshape(1, D_MODEL), ln2_b[i].reshape(1, D_MODEL)
        last = i == DEPTH - 1
        mod_x = (jax.nn.silu(c) @ ada_w[i] + ada_b[i])[:, None, :]
        mod_c = jnp.broadcast_to((jax.nn.silu(c_ctx) @ ada_w[i] + ada_b[i])[None, None, :], (b, 1, 6 * D_MODEL))
        sh1, sc1, g1, sh2, sc2, g2 = jnp.split(mod_x, 6, axis=-1)
        csh1, csc1, cg1, csh2, csc2, cg2 = jnp.split(mod_c, 6, axis=-1)
        xc1, side = _mixer(xc, (csh1, csc1, cg1), lp, i, None, not last, None)
        x1, _ = _mixer(x, (sh1, sc1, g1), lp, i, side, True, rope_tabs)
        x = _ffn(x1, sc2, sh2, g2, w1, w3, w2, l2g, l2b, l).reshape(b, l, D_MODEL)
        if not last:
            xc = _ffn(xc1, csc2, csh2, cg2, w1, w3, w2, l2g, l2b, lc).reshape(b, lc, D_MODEL)
    return x
```

```python
import functools
import math

import jax
import jax.numpy as jnp
from jax import lax
from jax.experimental import pallas as pl
from jax.experimental.pallas import tpu as pltpu

F32 = jnp.float32
BF16 = jnp.bfloat16

D_MODEL = 1024
DEPTH = 4
GRID_W = 64
DA_HEADS = 4
DA_HD = 64
DA_W = DA_HEADS * 2 * DA_HD
ROPE_BASE = 10000.0
RW_HEADS = 8
RW_HD = 64
RW_W = RW_HEADS * RW_HD
RW_DECAY_R = 64
RW_ICLR_R = 64
RW_GATE_R = 128
RW_GN_EPS = 64e-5
RW_COLS = 3 * RW_W + 2 * RW_DECAY_R + 2 * RW_ICLR_R + RW_GATE_R
SSM_HEADS = 8
SSM_HD = 64
SSM_W = SSM_HEADS * SSM_HD
SSM_GROUPS = 2
SSM_STATE = 128
SSM_XBC = SSM_W + 2 * SSM_GROUPS * SSM_STATE
FFN_HIDDEN = (8 * D_MODEL + 3 * 256 - 1) // (3 * 256) * 256
NORM_EPS = 1e-5
W_IN_SPLIT = (DA_W, DA_W, DA_W, RW_COLS, SSM_W, SSM_XBC, 2 * SSM_HEADS, D_MODEL, D_MODEL, D_MODEL)
ALPHA = (2.0 * DEPTH) ** 0.25

LANES = 128
SUBLANES = 8
HALO_ROWS = 16
VMEM_LIMIT_BYTES = 56 * 1024 * 1024

PROJ_COLS = 8192
RW_BLK = 2048
C_RW = 0
C_DT = RW_COLS
C_GA, C_GR, C_GS = RW_BLK, RW_BLK + D_MODEL, RW_BLK + 2 * D_MODEL
C_Q = RW_BLK + 3 * D_MODEL
C_K = C_Q + DA_W
C_V = C_K + DA_W
C_Z = C_V + DA_W
C_XBC = C_Z + SSM_W

Q_SCALE = math.log2(math.e) * DA_HD ** -0.5
ATTN_SUBTILE = 128
ATTN_BLOCK = 512
RW_CHUNK = 64
RW_BLOCK = 256
SSD_CHUNK = 128
SSD_BLOCK = 256


def _cparams(sem):
    return pltpu.CompilerParams(dimension_semantics=sem, vmem_limit_bytes=VMEM_LIMIT_BYTES)


def _dot(a, b):
    return jnp.dot(a, b, preferred_element_type=F32)


def _dot_nt(a, b):
    return lax.dot_general(a, b, (((1,), (1,)), ((), ())), preferred_element_type=F32)


def _dot_tn(a, b):
    return lax.dot_general(a, b, (((0,), (0,)), ((), ())), preferred_element_type=F32)


def _bf(x):
    return x.astype(BF16)


def _split3(x):
    hi = _bf(x)
    r1 = x - hi.astype(F32)
    mid = _bf(r1)
    lo = _bf(r1 - mid.astype(F32))
    return hi, mid, lo


def _cumsum_rows(tri_b, x):
    n = x.shape[1]
    hi, mid, lo = _split3(x)
    o = _dot(tri_b, jnp.concatenate([hi, mid, lo], axis=1))
    return o[:, :n] + o[:, n:2 * n] + o[:, 2 * n:]


def _seg_sum(x, seg_b):
    m = x.shape[0]
    hi = _bf(x)
    lo = _bf(x - hi.astype(F32))
    o = _dot(jnp.concatenate([hi, lo], axis=0), seg_b)
    return o[:m] + o[m:]


def _sigmoid(x):
    return 0.5 * jnp.tanh(0.5 * x) + 0.5


def _softplus(x):
    return jnp.maximum(x, 0.0) + jnp.log(1.0 + jnp.exp(-jnp.abs(x)))


def _layernorm_rows(t, g, b):
    mu = jnp.mean(t, -1, keepdims=True)
    d = t - mu
    var = jnp.mean(d * d, -1, keepdims=True)
    return d * lax.rsqrt(var + NORM_EPS) * g + b


def _neighbour_rows(x, prev_row, next_row):
    t = x.shape[0]
    rowi = lax.broadcasted_iota(jnp.int32, (t, 1), 0)
    x_m1 = jnp.where(rowi == 0, prev_row, pltpu.roll(x, 1, 0))
    x_p1 = jnp.where(rowi == t - 1, next_row, pltpu.roll(x, t - 1, 0))
    return x_m1, x_p1


def _halo_specs(width, col_blk, seq_len, tb, nc, n_rows):
    per8 = tb // HALO_ROWS
    last8 = n_rows // HALO_ROWS - 1

    def blk(d, bi, c):
        return bi * nc + c + d * (nc - 1 - 2 * c)

    prev = pl.BlockSpec((HALO_ROWS, width), lambda d, bi, c: (jnp.maximum(blk(d, bi, c) * per8 - 1, 0), col_blk))
    nxt = pl.BlockSpec((HALO_ROWS, width),
                       lambda d, bi, c: (jnp.minimum((blk(d, bi, c) + 1) * per8, last8), col_blk))
    cur = pl.BlockSpec((tb, width), lambda d, bi, c: (blk(d, bi, c), col_blk))
    return cur, prev, nxt


def _inproj_kernel(x_ref, sc_ref, sh_ref, w_ref, o_ref, h_scr):
    @pl.when(pl.program_id(1) == 0)
    def _():
        h = x_ref[...] * (1.0 + sc_ref[0]) + sh_ref[0]
        h_scr[...] = h.astype(BF16)

    o_ref[...] = _bf(_dot(h_scr[...], w_ref[...]))


def _inproj(x2d, sc, sh, w, seq_len):
    n = x2d.shape[0]
    tm = min(1024, seq_len)
    tn = 2048
    per_b = seq_len // tm
    return pl.pallas_call(
        _inproj_kernel,
        grid=(n // tm, PROJ_COLS // tn),
        in_specs=[
            pl.BlockSpec((tm, D_MODEL), lambda i, j: (i, 0)),
            pl.BlockSpec((1, 1, D_MODEL), lambda i, j: (i // per_b, 0, 0)),
            pl.BlockSpec((1, 1, D_MODEL), lambda i, j: (i // per_b, 0, 0)),
            pl.BlockSpec((D_MODEL, tn), lambda i, j: (0, j)),
        ],
        out_specs=pl.BlockSpec((tm, tn), lambda i, j: (i, j)),
        out_shape=jax.ShapeDtypeStruct((n, PROJ_COLS), BF16),
        scratch_shapes=[pltpu.VMEM((tm, D_MODEL), BF16)],
        compiler_params=_cparams(("arbitrary", "arbitrary")),
        name="inproj",
    )(x2d, sc, sh, w)


def _rope_lanes(x, cos, s_dn, s_up):
    nf = DA_HD // 4
    return x * cos + pltpu.roll(x, LANES - nf, 1) * s_dn + pltpu.roll(x, nf, 1) * s_up


def _attn_core(lam, q, k_scr, v_scr, g_ref, o_ref, out_scale):
    tq = ATTN_SUBTILE
    nsub = q.shape[0] // tq
    lk = k_scr.shape[0]
    lane = lax.broadcasted_iota(jnp.int32, (1, LANES), 1)
    mid = (lk // 2) // LANES * LANES
    cuts = (0, mid, lk) if mid else (0, lk)
    spans = list(zip(cuts[:-1], cuts[1:]))
    scores = []
    for t in range(nsub):
        qt = q[t * tq:(t + 1) * tq]
        zero = jnp.zeros_like(qt)
        qq = jnp.concatenate([jnp.where(lane < DA_HD, qt, zero), jnp.where(lane >= DA_HD, qt, zero)], axis=0)
        scores.append([_dot_nt(qq, k_scr[lo:hi, :]) for lo, hi in spans])
    for t, ss in enumerate(scores):
        m = functools.reduce(jnp.maximum, [jnp.max(s, -1, keepdims=True) for s in ss])
        ov = functools.reduce(jnp.add, [_dot(_bf(jnp.exp2(s - m)), v_scr[lo:hi, :]) for s, (lo, hi) in zip(ss, spans)])
        on = ov[:, :LANES] / ov[:, LANES:]
        o = on[:tq] - lam * on[tq:]
        ms = jnp.mean(o * o, -1, keepdims=True)
        o_ref[t * tq:(t + 1) * tq, :] = _bf(o * lax.rsqrt(ms + NORM_EPS) * (g_ref[...] * out_scale))


def _attn_latent_kernel(lam_ref, q_ref, kl_ref, vl_ref, kc_ref, vc_ref, cos_ref, sdn_ref, sup_ref, g_ref,
                        o_ref, k_scr, v_scr, *, out_scale):
    i = pl.program_id(2)
    tq = q_ref.shape[0]
    lc = kc_ref.shape[0]

    @pl.when(i == 0)
    def _():
        k_scr[:lc, :] = kc_ref[...]
        k_scr[lc:, :] = _bf(_rope_lanes(kl_ref[...].astype(F32), cos_ref[...], sdn_ref[...], sup_ref[...]))
        v_scr[:lc, :LANES] = vc_ref[...]
        v_scr[lc:, :LANES] = vl_ref[...]
        v_scr[:, LANES:] = jnp.ones((v_scr.shape[0], LANES), BF16)

    rows = pl.ds(pl.multiple_of(i * tq, tq), tq)
    q = _rope_lanes(q_ref[...].astype(F32), cos_ref[rows, :], sdn_ref[rows, :], sup_ref[rows, :])
    _attn_core(lam_ref[0], _bf(q * Q_SCALE), k_scr, v_scr, g_ref, o_ref, out_scale)


def _attn_ctx_kernel(lam_ref, q_ref, kl_ref, vl_ref, g_ref, o_ref, k_scr, v_scr, *, out_scale):
    @pl.when(pl.program_id(2) == 0)
    def _():
        k_scr[...] = kl_ref[...]
        v_scr[:, :LANES] = vl_ref[...]
        v_scr[:, LANES:] = jnp.ones((v_scr.shape[0], LANES), BF16)

    _attn_core(lam_ref[0], _bf(q_ref[...].astype(F32) * Q_SCALE), k_scr, v_scr, g_ref, o_ref, out_scale)


def _diff_attention(lam, proj, batch, seq_len, g, out_scale, proj_ctx=None, ctx_len=0, rope_tabs=None):
    tq = min(ATTN_BLOCK, seq_len)
    nq = seq_len // tq
    lk = seq_len + ctx_len
    cq, ck, cv = C_Q // LANES, C_K // LANES, C_V // LANES
    smem = pl.BlockSpec(memory_space=pltpu.SMEM)
    qspec = pl.BlockSpec((tq, LANES), lambda bi, h, i: (bi * nq + i, cq + h))
    kspec = pl.BlockSpec((seq_len, LANES), lambda bi, h, i: (bi, ck + h))
    vspec = pl.BlockSpec((seq_len, LANES), lambda bi, h, i: (bi, cv + h))
    gspec = pl.BlockSpec((1, LANES), lambda bi, h, i: (0, 0))
    common = dict(
        grid=(batch, DA_HEADS, nq),
        out_specs=pl.BlockSpec((tq, LANES), lambda bi, h, i: (bi * nq + i, h)),
        out_shape=jax.ShapeDtypeStruct((batch * seq_len, DA_W), BF16),
        scratch_shapes=[pltpu.VMEM((lk, LANES), BF16), pltpu.VMEM((lk, 2 * LANES), BF16)],
        compiler_params=_cparams(("arbitrary", "arbitrary", "arbitrary")),
    )
    if proj_ctx is None:
        return pl.pallas_call(
            functools.partial(_attn_ctx_kernel, out_scale=out_scale),
            in_specs=[smem, qspec, kspec, vspec, gspec], name="diff_attn_ctx", **common,
        )(lam, proj, proj, proj, g)
    tab = pl.BlockSpec((seq_len, LANES), lambda bi, h, i: (0, 0))
    return pl.pallas_call(
        functools.partial(_attn_latent_kernel, out_scale=out_scale),
        in_specs=[smem, qspec, kspec, vspec,
                  pl.BlockSpec((ctx_len, LANES), lambda bi, h, i: (bi, ck + h)),
                  pl.BlockSpec((ctx_len, LANES), lambda bi, h, i: (bi, cv + h)),
                  tab, tab, tab, gspec],
        name="diff_attn", **common,
    )(lam, proj, proj, proj, proj_ctx, proj_ctx, *rope_tabs, g)


def _rwkv_kernel(u_ref, up_ref, un_ref, mu_ref, w0_ref, w2_ref, a0_ref, a2_ref, g2_ref, vec_ref, seg_ref,
                 s0_ref, y_ref, aux_ref, sf_ref, s_scr):
    d = pl.program_id(0)
    c = pl.program_id(2)
    nc = pl.num_programs(2)

    @pl.when(c == 0)
    def _():
        s_scr[...] = s0_ref[0, 0]

    refs = (u_ref, up_ref, un_ref, mu_ref, w0_ref, w2_ref, a0_ref, a2_ref, g2_ref, vec_ref, seg_ref,
            y_ref, aux_ref, s_scr)

    @pl.when(d == 0)
    def _():
        _rwkv_step(True, c, nc, *refs)

    @pl.when(d == 1)
    def _():
        _rwkv_step(False, nc - 1 - c, nc, *refs)

    @pl.when(c == nc - 1)
    def _():
        sf_ref[0, 0] = s_scr[...]


def _rwkv_step(fwd, cc, nc, u_ref, up_ref, un_ref, mu_ref, w0_ref, w2_ref, a0_ref, a2_ref, g2_ref, vec_ref,
               seg_ref, y_ref, aux_ref, s_scr):
    t = RW_CHUNK
    tb = u_ref.shape[0]
    nper = tb // t
    di = 0 if fwd else 1

    u = u_ref[:, :RW_COLS].astype(F32)
    prev = up_ref[HALO_ROWS - 1:HALO_ROWS, :RW_COLS].astype(F32) * (cc > 0).astype(F32)
    nxt = un_ref[0:1, :RW_COLS].astype(F32) * (cc < nc - 1).astype(F32)
    u_m1, u_p1 = _neighbour_rows(u, prev, nxt)
    u = u + mu_ref[:, :RW_COLS] * (0.5 * (u_m1 + u_p1) - u)
    r = r_all = u[:, :RW_W]
    k = u[:, RW_W:2 * RW_W]
    v = v_all = u[:, 2 * RW_W:3 * RW_W]
    o = 3 * RW_W
    wd = _bf(jnp.tanh(u[:, o:o + 2 * RW_DECAY_R]))
    o += 2 * RW_DECAY_R
    ad = _bf(u[:, o:o + 2 * RW_ICLR_R])
    o += 2 * RW_ICLR_R
    gd = u[:, o:o + RW_GATE_R]
    kk_gain, ka, rk = vec_ref[0:1, :], vec_ref[1:2, :], vec_ref[2:3, :]
    seg = seg_ref[...]
    lw_all = -math.exp(-0.5) * _sigmoid(w0_ref[0] + _dot(wd, w2_ref[0]))
    a_all = _sigmoid(a0_ref[di:di + 1, :] + _dot(ad, a2_ref[di]))
    kk_all = k * kk_gain
    kk_all = kk_all * lax.rsqrt(jnp.maximum(_seg_sum(kk_all * kk_all, seg), 1e-24))
    kd_all = k * (1.0 + (a_all - 1.0) * ka)
    if fwd:
        a_other = _sigmoid(a0_ref[1:2, :] + _dot(ad, a2_ref[1]))
        k_bonus = k * (1.0 + (0.5 * (a_all + a_other) - 1.0) * ka)
        aux_ref[0, 0] = _bf(_seg_sum(r * k_bonus * rk, seg) * v)
    else:
        aux_ref[0, 0] = _bf(_dot(_bf(_sigmoid(gd)), g2_ref[...]))

    sign = 1 if fwd else -1
    row = lax.broadcasted_iota(jnp.int32, (t, t), 0)
    col = lax.broadcasted_iota(jnp.int32, (t, t), 1)
    tri_b = ((row - col) * sign >= 0).astype(BF16)
    row2 = lax.broadcasted_iota(jnp.int32, (t, 2 * t), 0)
    col2 = lax.broadcasted_iota(jnp.int32, (t, 2 * t), 1)
    col2 = jnp.where(col2 >= t, col2 - t, col2)
    delta2 = (row2 - col2) * sign
    strict2 = delta2 > 0
    incl2 = delta2 >= 0
    colh = lax.broadcasted_iota(jnp.int32, (1, 2 * t), 1)
    cm0 = (colh < t).astype(F32)
    cm1 = 1.0 - cm0
    lane = lax.broadcasted_iota(jnp.int32, (1, LANES), 1)
    m0 = (lane < RW_HD).astype(F32)
    m1 = 1.0 - m0
    rb = lax.broadcasted_iota(jnp.int32, (LANES, LANES), 0) // RW_HD
    cb = lax.broadcasted_iota(jnp.int32, (LANES, LANES), 1) // RW_HD
    bd = (rb == cb).astype(F32)
    eye = (lax.broadcasted_iota(jnp.int32, (2 * t, 2 * t), 0)
           == lax.broadcasted_iota(jnp.int32, (2 * t, 2 * t), 1)).astype(F32)

    def stack2(x):
        return _bf(jnp.concatenate([x * m0, x * m1], axis=0))

    nhp = RW_W // LANES
    inst = [(j, i) for i in range(nper) for j in range(nhp)]

    def rows_of(i):
        ip = i if fwd else nper - 1 - i
        return slice(ip * t, (ip + 1) * t)

    def sl_of(j):
        return slice(j * LANES, (j + 1) * LANES)

    lws = [lw_all[rows_of(i), sl_of(j)] for j, i in inst]
    cums = [_cumsum_rows(tri_b, lw) for lw in lws]
    pre = []
    for (j, i), lw, cum in zip(inst, lws, cums):
        rows, sl = rows_of(i), sl_of(j)
        r = r_all[rows, sl]
        v = v_all[rows, sl]
        kk = kk_all[rows, sl]
        kd = kd_all[rows, sl]
        a = a_all[rows, sl]
        tot = jnp.sum(lw, axis=0, keepdims=True)
        g_inv = jnp.exp(-cum)
        g_rem = jnp.exp(tot - cum)
        rt = r * jnp.exp(cum)
        kt = _bf(kk * jnp.exp(cum - lw))
        b = kk * a
        bh = b * g_inv
        kh = kd * g_inv
        ystack = _bf(jnp.concatenate([bh * m0, bh * m1, kh * m0, kh * m1], axis=0))
        pre.append(dict(v=v, rt=rt, kt=kt, ystack=ystack, g_tot=jnp.exp(tot), vst=stack2(v),
                        rhs_r=_bf(jnp.concatenate([kd * g_rem, b * g_rem], axis=0))))
    ggs = [_dot_nt(jnp.concatenate([p['kt'], _bf(p['rt'])], axis=0), p['ystack']) for p in pre]
    for p, gg in zip(pre, ggs):
        mcat = jnp.where(strict2, gg[:t, :2 * t], 0.0)
        p['ncat'] = _bf(jnp.where(strict2, gg[:t, 2 * t:], 0.0))
        p['qcat'] = _bf(jnp.where(incl2, gg[t:, :2 * t], 0.0))
        p['pcat'] = _bf(jnp.where(incl2, gg[t:, 2 * t:], 0.0))
        p['m'] = jnp.concatenate([mcat * cm0, mcat * cm1], axis=0)
    ri = lax.broadcasted_iota(jnp.int32, (2 * t, 2 * t), 0)
    ci = lax.broadcasted_iota(jnp.int32, (2 * t, 2 * t), 1)
    same_head = (ri // t) == (ci // t)
    lo_i, hi_i = (ci, ri) if fwd else (ri, ci)

    def level_mask(sz):
        return (same_head & ((ri % t) // (2 * sz) == (ci % t) // (2 * sz))
                & ((hi_i % (2 * sz)) >= sz) & ((lo_i % (2 * sz)) < sz))

    nvs = [_dot(p['ncat'], p['vst']) for p in pre]
    ws = [eye - jnp.where(level_mask(1), p['m'], 0.0) for p in pre]
    sz = 2
    while sz < t:
        lm = level_mask(sz)
        zs = [_dot(_bf(w), _bf(jnp.where(lm, p['m'], 0.0))) for w, p in zip(ws, pre)]
        ws = [w - _dot(_bf(z), _bf(w)) for w, z in zip(ws, zs)]
        sz *= 2
    wcats = [_bf(w[:t] + w[t:]) for w in ws]
    uks = [_dot(wc, jnp.concatenate([stack2(nv), stack2(p['kt'].astype(F32))], axis=1))
           for wc, nv, p in zip(wcats, nvs, pre)]
    qus = [_dot(p['qcat'], jnp.concatenate([stack2(uk[:, :LANES]), stack2(uk[:, LANES:])], axis=1))
           for p, uk in zip(pre, uks)]
    pvs = [_dot(p['pcat'], p['vst']) for p in pre]
    zeros_t = jnp.zeros((t, LANES), F32)
    dcs = [_dot_tn(p['rhs_r'], _bf(jnp.concatenate([jnp.concatenate([p['v'], -uk[:, :LANES]], axis=0),
                                                    jnp.concatenate([zeros_t, uk[:, LANES:]], axis=0)], axis=1)))
           for p, uk in zip(pre, uks)]
    for p, qu, pv, dc in zip(pre, qus, pvs, dcs):
        p['y0'] = pv - qu[:, :LANES]
        p['lhs'] = jnp.concatenate([_bf(p['rt'] - qu[:, LANES:]), _bf(bd * dc[:, LANES:])], axis=0)
        p['d0t'] = bd * dc[:, :LANES]
        p['g_col'] = jnp.broadcast_to(p['g_tot'], (LANES, LANES)).T
    states = [s_scr[j] for j in range(nhp)]
    for i in range(nper):
        ps = [pre[i * nhp + j] for j in range(nhp)]
        outs = [_dot(p['lhs'], _bf(st)) for p, st in zip(ps, states)]
        for j in range(nhp):
            y_ref[0, 0, rows_of(i), sl_of(j)] = _bf(ps[j]['y0'] + outs[j][:t])
        states = [st * p['g_col'] - o[t:] + p['d0t'] for st, p, o in zip(states, ps, outs)]
    for j in range(nhp):
        s_scr[j] = states[j]


def _rwkv_scan(proj, batch, seq_len, rwp, s0):
    n = proj.shape[0]
    tb = min(RW_BLOCK, seq_len)
    nc = seq_len // tb
    nhp = RW_W // LANES
    cur, prev, nxt = _halo_specs(RW_BLK, C_RW // RW_BLK, seq_len, tb, nc, n)
    full2 = lambda d, bi, c: (0, 0)
    full3 = lambda d, bi, c: (0, 0, 0)
    dmap = lambda d, bi, c: (d, bi, c + d * (nc - 1 - 2 * c), 0)
    smap = lambda d, bi, c: (d, bi, 0, 0, 0)
    return pl.pallas_call(
        _rwkv_kernel,
        grid=(2, batch, nc),
        in_specs=[
            cur, prev, nxt,
            pl.BlockSpec((1, RW_BLK), full2),
            pl.BlockSpec((1, 1, RW_W), lambda d, bi, c: (d, 0, 0)),
            pl.BlockSpec((1, 2 * RW_DECAY_R, RW_W), lambda d, bi, c: (d, 0, 0)),
            pl.BlockSpec((2, RW_W), full2),
            pl.BlockSpec((2, 2 * RW_ICLR_R, RW_W), full3),
            pl.BlockSpec((RW_GATE_R, RW_W), full2),
            pl.BlockSpec((3, RW_W), full2),
            pl.BlockSpec((RW_W, RW_W), full2),
            pl.BlockSpec((1, 1, nhp, LANES, LANES), smap),
        ],
        out_specs=[
            pl.BlockSpec((1, 1, tb, RW_W), dmap),
            pl.BlockSpec((1, 1, tb, RW_W), dmap),
            pl.BlockSpec((1, 1, nhp, LANES, LANES), smap),
        ],
        out_shape=[
            jax.ShapeDtypeStruct((2, batch, seq_len, RW_W), BF16),
            jax.ShapeDtypeStruct((2, batch, seq_len, RW_W), BF16),
            jax.ShapeDtypeStruct((2, batch, nhp, LANES, LANES), F32),
        ],
        scratch_shapes=[pltpu.VMEM((nhp, LANES, LANES), F32)],
        compiler_params=_cparams(("arbitrary", "arbitrary", "arbitrary")),
        name="rwkv_scan",
    )(proj, proj, proj, rwp['mu'], rwp['w0'], rwp['w2'], rwp['a0'], rwp['a2'], rwp['g2'], rwp['vec'],
      rwp['seg'], s0)


def _ssd_kernel(x_ref, xp_ref, xn_ref, dt_ref, cw_ref, cb_ref, dtb_ref, alog_ref, dskip_ref, s0_ref,
                y_ref, sf_ref, s_scr):
    d = pl.program_id(0)
    c = pl.program_id(2)
    nc = pl.num_programs(2)

    @pl.when(c == 0)
    def _():
        s_scr[...] = s0_ref[0, 0]

    refs = (x_ref, xp_ref, xn_ref, dt_ref, cw_ref, cb_ref, dtb_ref, alog_ref, dskip_ref, y_ref, s_scr)

    @pl.when(d == 0)
    def _():
        _ssd_step(True, c, nc, *refs)

    @pl.when(d == 1)
    def _():
        _ssd_step(False, nc - 1 - c, nc, *refs)

    @pl.when(c == nc - 1)
    def _():
        sf_ref[0, 0] = s_scr[...]


def _ssd_step(fwd, cc, nc, x_ref, xp_ref, xn_ref, dt_ref, cw_ref, cb_ref, dtb_ref, alog_ref, dskip_ref,
              y_ref, s_scr):
    t = SSD_CHUNK
    gw = SSM_W // SSM_GROUPS
    hpg = SSM_HEADS // SSM_GROUPS
    nh = SSM_HEADS

    x = x_ref[...].astype(F32)
    prev = xp_ref[HALO_ROWS - 1:HALO_ROWS, :].astype(F32) * (cc > 0).astype(F32)
    nxt = xn_ref[0:1, :].astype(F32) * (cc < nc - 1).astype(F32)
    x_m1, x_p1 = _neighbour_rows(x, prev, nxt)
    xc = x_m1 * cw_ref[0:1, :] + x * cw_ref[1:2, :] + x_p1 * cw_ref[2:3, :] + cb_ref[...]
    xc = xc * _sigmoid(xc)
    xs = xc[:, :SSM_W]
    bm = xc[:, SSM_W:SSM_W + SSM_GROUPS * SSM_STATE]
    cm = xc[:, SSM_W + SSM_GROUPS * SSM_STATE:]
    dt2 = _softplus(dt_ref[...].astype(F32)[:, :2 * nh] + dtb_ref[...])
    a2 = dt2 * (-jnp.exp(alog_ref[...]))
    dt_all = dt2[:, :nh] if fwd else dt2[:, nh:]
    a_all = a2[:, :nh] if fwd else a2[:, nh:]

    sign = 1 if fwd else -1
    row = lax.broadcasted_iota(jnp.int32, (t, t), 0)
    col = lax.broadcasted_iota(jnp.int32, (t, t), 1)
    delta = (row - col) * sign
    incl = delta >= 0
    tri_b = incl.astype(BF16)
    tri_t_b = (delta <= 0).astype(BF16)
    lane = lax.broadcasted_iota(jnp.int32, (1, LANES), 1)
    m0 = (lane < SSM_HD).astype(F32)
    m1 = 1.0 - m0
    lane_g = lax.broadcasted_iota(jnp.int32, (1, gw), 1) // SSM_HD

    def expand(z, g):
        out = jnp.zeros((z.shape[0], gw), F32)
        for hh in range(hpg):
            h = g * hpg + hh
            out = jnp.where(lane_g == hh, z[:, h:h + 1], out)
        return out

    nper = x_ref.shape[0] // t
    states = [s_scr[g] for g in range(SSM_GROUPS)]
    for i in range(nper):
        ip = i if fwd else nper - 1 - i
        rows = slice(ip * t, (ip + 1) * t)
        dt = dt_all[rows]
        a = a_all[rows]
        a3 = jnp.concatenate(_split3(a), axis=1)
        cs3 = _dot(tri_b, a3)
        cs = cs3[:, :nh] + cs3[:, nh:2 * nh] + cs3[:, 2 * nh:]
        cs3_t = _dot_tn(a3, tri_t_b)
        cs_t = cs3_t[:nh] + cs3_t[nh:2 * nh] + cs3_t[2 * nh:]
        tot = jnp.sum(a, axis=0, keepdims=True)
        e_cs = jnp.exp(cs)
        e_rem = jnp.exp(tot - cs)
        e_tot = jnp.exp(tot)
        for g in range(SSM_GROUPS):
            cg = cm[rows, g * SSM_STATE:(g + 1) * SSM_STATE]
            bg = bm[rows, g * SSM_STATE:(g + 1) * SSM_STATE]
            xsg = xs[rows, g * gw:(g + 1) * gw]
            xg = xsg * expand(dt, g)
            cgb = _bf(cg)
            bgb = _bf(bg)
            gm = _dot_nt(cgb, bgb)
            st = states[g]
            y_off = _dot(cgb, _bf(st)) * expand(e_cs, g)
            ys = []
            for hp in range(hpg // 2):
                scs = []
                for hh in range(2):
                    h = g * hpg + 2 * hp + hh
                    diff = jnp.where(incl, cs[:, h:h + 1] - cs_t[h:h + 1, :], 0.0)
                    scs.append(gm * jnp.where(incl, jnp.exp(diff), 0.0))
                xp = xg[:, hp * LANES:(hp + 1) * LANES]
                xst = _bf(jnp.concatenate([xp * m0, xp * m1], axis=0))
                ys.append(_dot(_bf(jnp.concatenate(scs, axis=1)), xst))
            y = jnp.concatenate(ys, axis=1) + y_off
            if fwd:
                y = y + dskip_ref[:, g * gw:(g + 1) * gw] * xsg
            y_ref[0, 0, rows, g * gw:(g + 1) * gw] = _bf(y)
            states[g] = st * expand(e_tot, g) + _dot_tn(bgb, _bf(xg * expand(e_rem, g)))
    for g in range(SSM_GROUPS):
        s_scr[g] = states[g]


def _ssd_scan(proj, batch, seq_len, ssp, s0):
    n = proj.shape[0]
    t = min(SSD_BLOCK, seq_len)
    nc = seq_len // t
    gw = SSM_W // SSM_GROUPS
    cur, prev, nxt = _halo_specs(SSM_XBC, C_XBC // SSM_XBC, seq_len, t, nc, n)
    dtspec, _, _ = _halo_specs(LANES, C_DT // LANES, seq_len, t, nc, n)
    full2 = lambda d, bi, c: (0, 0)
    dmap = lambda d, bi, c: (d, bi, c + d * (nc - 1 - 2 * c), 0)
    smap = lambda d, bi, c: (d, bi, 0, 0, 0)
    return pl.pallas_call(
        _ssd_kernel,
        grid=(2, batch, nc),
        in_specs=[
            cur, prev, nxt, dtspec,
            pl.BlockSpec((3, SSM_XBC), full2),
            pl.BlockSpec((1, SSM_XBC), full2),
            pl.BlockSpec((1, 2 * SSM_HEADS), full2),
            pl.BlockSpec((1, 2 * SSM_HEADS), full2),
            pl.BlockSpec((1, SSM_W), full2),
            pl.BlockSpec((1, 1, SSM_GROUPS, SSM_STATE, gw), smap),
        ],
        out_specs=[
            pl.BlockSpec((1, 1, t, SSM_W), dmap),
            pl.BlockSpec((1, 1, SSM_GROUPS, SSM_STATE, gw), smap),
        ],
        out_shape=[
            jax.ShapeDtypeStruct((2, batch, seq_len, SSM_W), BF16),
            jax.ShapeDtypeStruct((2, batch, SSM_GROUPS, SSM_STATE, gw), F32),
        ],
        scratch_shapes=[pltpu.VMEM((SSM_GROUPS, SSM_STATE, gw), F32)],
        compiler_params=_cparams(("arbitrary", "arbitrary", "arbitrary")),
        name="ssd_scan",
    )(proj, proj, proj, proj, ssp['conv_w'], ssp['conv_b'], ssp['dt_bias'], ssp['a_log'], ssp['dskip'], s0)


def _merge_kernel(ya_ref, yr_ref, aux_ref, ys_ref, z_ref, ga_ref, gr_ref, gs_ref, x_ref, g1_ref,
                  seg_ref, rg_ref, rb_ref, sg_ref, pa_ref, pr_ref, ps_ref, wo_ref, lg_ref, lb_ref, o_ref):
    seg = seg_ref[...]
    yr = yr_ref[0].astype(F32) + yr_ref[1].astype(F32)
    mu = _seg_sum(yr, seg) * (1.0 / RW_HD)
    dv = yr - mu
    var = _seg_sum(dv * dv, seg) * (1.0 / RW_HD)
    yr = (dv * lax.rsqrt(var + RW_GN_EPS) * rg_ref[...] + rb_ref[...] + aux_ref[0].astype(F32)) * aux_ref[1].astype(F32)
    z = z_ref[...].astype(F32)
    ys = (ys_ref[0].astype(F32) + ys_ref[1].astype(F32)) * (z * _sigmoid(z))
    gw = SSM_W // SSM_GROUPS
    parts = []
    for g in range(SSM_GROUPS):
        yg = ys[:, g * gw:(g + 1) * gw]
        parts.append(yg * lax.rsqrt(jnp.mean(yg * yg, -1, keepdims=True) + NORM_EPS))
    ys = jnp.concatenate(parts, axis=1) * sg_ref[...]
    m = (_sigmoid(ga_ref[...].astype(F32)) * _dot(ya_ref[...], pa_ref[...])
         + _sigmoid(gr_ref[...].astype(F32)) * _dot(_bf(yr), pr_ref[...])
         + _sigmoid(gs_ref[...].astype(F32)) * _dot(_bf(ys), ps_ref[...]))
    out = _dot(_bf(m), wo_ref[...])
    o_ref[...] = _layernorm_rows(ALPHA * x_ref[...] + g1_ref[0] * out, lg_ref[...], lb_ref[...])


def _merge(ya, yr2, aux, ys2, proj, x2d, g1, lp, seq_len):
    n = x2d.shape[0]
    tm = min(512, seq_len)
    per_b = seq_len // tm
    row = lambda i: (i, 0)
    row3 = lambda i: (0, i, 0)
    full = lambda i: (0, 0)
    gate = lambda col: pl.BlockSpec((tm, D_MODEL), lambda i: (i, col // D_MODEL))
    vec512 = pl.BlockSpec((1, RW_W), full)
    vec = pl.BlockSpec((1, D_MODEL), full)
    return pl.pallas_call(
        _merge_kernel,
        grid=(n // tm,),
        in_specs=[
            pl.BlockSpec((tm, DA_W), row),
            pl.BlockSpec((2, tm, RW_W), row3), pl.BlockSpec((2, tm, RW_W), row3),
            pl.BlockSpec((2, tm, SSM_W), row3),
            pl.BlockSpec((tm, SSM_W), lambda i: (i, C_Z // SSM_W)),
            gate(C_GA), gate(C_GR), gate(C_GS),
            pl.BlockSpec((tm, D_MODEL), row),
            pl.BlockSpec((1, 1, D_MODEL), lambda i: (i // per_b, 0, 0)),
            pl.BlockSpec((RW_W, RW_W), full), vec512, vec512, vec512,
            pl.BlockSpec((DA_W, D_MODEL), full), pl.BlockSpec((RW_W, D_MODEL), full),
            pl.BlockSpec((SSM_W, D_MODEL), full), pl.BlockSpec((D_MODEL, D_MODEL), full),
            vec, vec,
        ],
        out_specs=pl.BlockSpec((tm, D_MODEL), row),
        out_shape=jax.ShapeDtypeStruct((n, D_MODEL), F32),
        compiler_params=_cparams(("arbitrary",)),
        name="merge_ln",
    )(ya, yr2, aux, ys2, proj, proj, proj, proj, x2d, g1, lp['rw']['seg'], lp['rw_ln_g'], lp['rw_ln_b'],
      lp['ssm_norm_g'], lp['p_attn'], lp['p_rwkv'], lp['p_ssm'], lp['w_out'], lp['ln1_g'], lp['ln1_b'])


def _ffn_kernel(x_ref, sc_ref, sh_ref, g2_ref, w1_ref, w3_ref, w2_ref, lg_ref, lb_ref, o_ref):
    x = x_ref[...]
    h = _bf(x * (1.0 + sc_ref[0]) + sh_ref[0])
    th = FFN_HIDDEN // 2
    acc = None
    for j in range(2):
        cols = slice(j * th, (j + 1) * th)
        a = _dot(h, w1_ref[:, cols])
        u = a * _sigmoid(a) * _dot(h, w3_ref[:, cols])
        part = _dot(_bf(u), w2_ref[cols, :])
        acc = part if acc is None else acc + part
    o_ref[...] = _layernorm_rows(ALPHA * x + g2_ref[0] * acc, lg_ref[...], lb_ref[...])


def _ffn(x2d, sc, sh, g2, w1, w3, w2, lg, lb, seq_len):
    n = x2d.shape[0]
    tm = min(512, seq_len)
    per_b = seq_len // tm
    mod = pl.BlockSpec((1, 1, D_MODEL), lambda i: (i // per_b, 0, 0))
    resident = lambda shape: pl.BlockSpec(shape, lambda i: (0, 0), pipeline_mode=pl.Buffered(1))
    return pl.pallas_call(
        _ffn_kernel,
        grid=(n // tm,),
        in_specs=[
            pl.BlockSpec((tm, D_MODEL), lambda i: (i, 0)),
            mod, mod, mod,
            resident((D_MODEL, FFN_HIDDEN)), resident((D_MODEL, FFN_HIDDEN)), resident((FFN_HIDDEN, D_MODEL)),
            pl.BlockSpec((1, D_MODEL), lambda i: (0, 0)),
            pl.BlockSpec((1, D_MODEL), lambda i: (0, 0)),
        ],
        out_specs=pl.BlockSpec((tm, D_MODEL), lambda i: (i, 0)),
        out_shape=jax.ShapeDtypeStruct((n, D_MODEL), F32),
        compiler_params=_cparams(("arbitrary",)),
        name="ffn_ln",
    )(x2d, sc, sh, g2, w1, w3, w2, lg, lb)


def _rope_tables(n_tok):
    rows = n_tok // GRID_W
    row = jnp.repeat(jnp.arange(rows), GRID_W).astype(F32)
    col = jnp.tile(jnp.arange(GRID_W), rows).astype(F32)
    nf = DA_HD // 4
    inv = ROPE_BASE ** (-jnp.arange(nf, dtype=F32) / nf)
    ang = jnp.concatenate([row[:, None] * inv] * 2 + [col[:, None] * inv] * 2, axis=-1)
    cos = jnp.cos(ang)
    sin = jnp.sin(ang)
    first = (jnp.arange(DA_HD) % (2 * nf)) < nf
    s_dn = jnp.where(first, -sin, 0.0)
    s_up = jnp.where(first, 0.0, sin)
    reps = LANES // DA_HD
    return jnp.tile(cos, (1, reps)), jnp.tile(s_dn, (1, reps)), jnp.tile(s_up, (1, reps))


def _permute_w_in(w_in):
    parts, acc = [], 0
    for s in W_IN_SPLIT:
        parts.append(w_in[:, acc:acc + s])
        acc += s
    q, k, v, rw, z, xbc, dt, ga, gr, gs = parts
    pad = jnp.zeros((w_in.shape[0], RW_BLK - RW_COLS - dt.shape[1]), w_in.dtype)
    return jnp.concatenate([rw, dt, pad, ga, gr, gs, q, k, v, z, xbc], axis=1).astype(BF16)


def _pad_lora(w, rank):
    z = jnp.zeros_like(w[0])
    return jnp.stack([jnp.concatenate([w[0], z], axis=0), jnp.concatenate([z, w[1]], axis=0)]).astype(BF16)


def _mixer(x, mod, lp, layer, side, want_out, rope_tabs):
    b, l, _ = x.shape
    sh1, sc1, g1 = mod
    n = b * l
    x2d = x.reshape(n, D_MODEL)
    proj = _inproj(x2d, sc1, sh1, lp['w_in_p'], l)
    if side is None:
        proj_ctx, ctx_len = None, 0
        s0_r = jnp.zeros((2, b, RW_W // LANES, LANES, LANES), F32)
        s0_s = jnp.zeros((2, b, SSM_GROUPS, SSM_STATE, SSM_W // SSM_GROUPS), F32)
    else:
        (proj_ctx, ctx_len), s0_r, s0_s = side
    y_r2, aux, sf_r = _rwkv_scan(proj, b, l, lp['rw'], s0_r)
    y_s2, sf_s = _ssd_scan(proj, b, l, lp['ssm'], s0_s)
    new_side = ((proj, l), sf_r, sf_s)
    if not want_out:
        return None, new_side
    lam_init = 0.8 - 0.6 * math.exp(-0.3 * layer)
    lam = (jnp.exp(jnp.sum(lp['da_lq1'] * lp['da_lk1'])) - jnp.exp(jnp.sum(lp['da_lq2'] * lp['da_lk2']))
           + lam_init).reshape(1)
    y_a = _diff_attention(lam, proj, b, l, lp['da_norm_g'].reshape(1, 2 * DA_HD), 1.0 - lam_init,
                          proj_ctx, ctx_len, rope_tabs)
    x1 = _merge(y_a, y_r2.reshape(2, n, RW_W), aux.reshape(2, n, RW_W),
                y_s2.reshape(2, n, SSM_W), proj, x2d, g1, lp, l)
    return x1, new_side


def kernel(x, c, ctx, c_ctx, ada_w, ada_b, w_in, da_lq1, da_lk1, da_lq2, da_lk2, da_norm_g,
           rw_mu, rw_w0, rw_w2, rw_a0, rw_a2, rw_g2, rw_kk, rw_ka, rw_rk, rw_ln_g, rw_ln_b,
           ssm_conv_w, ssm_conv_b, ssm_dt_bias, ssm_a_log, ssm_d, ssm_norm_g,
           p_attn, p_rwkv, p_ssm, w_out, ln1_g, ln1_b, ffn_w1, ffn_w3, ffn_w2, ln2_g, ln2_b):
    b, l, _ = x.shape
    lc = ctx.shape[1]
    rope_tabs = _rope_tables(l)
    seg = (jnp.arange(RW_W)[:, None] // RW_HD == jnp.arange(RW_W)[None, :] // RW_HD).astype(BF16)
    xc = ctx
    for i in range(DEPTH):
        rw = {
            'mu': jnp.pad(rw_mu[i], (0, RW_BLK - RW_COLS)).reshape(1, RW_BLK),
            'w0': rw_w0[i].reshape(2, 1, RW_W),
            'w2': _pad_lora(rw_w2[i], RW_DECAY_R),
            'a0': rw_a0[i],
            'a2': _pad_lora(rw_a2[i], RW_ICLR_R),
            'g2': rw_g2[i].astype(BF16),
            'vec': jnp.stack([rw_kk[i], rw_ka[i], rw_rk[i].reshape(RW_W)]),
            'seg': seg,
        }
        ssm = {
            'conv_w': ssm_conv_w[i], 'conv_b': ssm_conv_b[i].reshape(1, SSM_XBC),
            'dt_bias': ssm_dt_bias[i].reshape(1, 2 * SSM_HEADS), 'a_log': ssm_a_log[i].reshape(1, 2 * SSM_HEADS),
            'dskip': jnp.repeat(ssm_d[i], SSM_HD).reshape(1, SSM_W),
        }
        lp = {
            'w_in_p': _permute_w_in(w_in[i]), 'rw': rw, 'ssm': ssm,
            'da_lq1': da_lq1[i], 'da_lk1': da_lk1[i], 'da_lq2': da_lq2[i], 'da_lk2': da_lk2[i],
            'da_norm_g': da_norm_g[i],
            'rw_ln_g': rw_ln_g[i].reshape(1, RW_W), 'rw_ln_b': rw_ln_b[i].reshape(1, RW_W),
            'ssm_norm_g': ssm_norm_g[i].reshape(1, SSM_W),
            'p_attn': p_attn[i].astype(BF16), 'p_rwkv': p_rwkv[i].astype(BF16),
            'p_ssm': p_ssm[i].astype(BF16), 'w_out': w_out[i].astype(BF16),
            'ln1_g': ln1_g[i].reshape(1, D_MODEL), 'ln1_b': ln1_b[i].reshape(1, D_MODEL),
        }
        w1, w3, w2 = ffn_w1[i].astype(BF16), ffn_w3[i].astype(BF16), ffn_w2[i].astype(BF16)
        l2g, l2b = ln2_g[i].reshape(1, D_MODEL), ln2_b[i].reshape(1, D_MODEL)
        last = i == DEPTH - 1
        mod_x = (jax.nn.silu(c) @ ada_w[i] + ada_b[i])[:, None, :]
        mod_c = jnp.broadcast_to((jax.nn.silu(c_ctx) @ ada_w[i] + ada_b[i])[None, None, :], (b, 1, 6 * D_MODEL))
        sh1, sc1, g1, sh2, sc2, g2 = jnp.split(mod_x, 6, axis=-1)
        csh1, csc1, cg1, csh2, csc2, cg2 = jnp.split(mod_c, 6, axis=-1)
        xc1, side = _mixer(xc, (csh1, csc1, cg1), lp, i, None, not last, None)
        x1, _ = _mixer(x, (sh1, sc1, g1), lp, i, side, True, rope_tabs)
        x = _ffn(x1, sc2, sh2, g2, w1, w3, w2, l2g, l2b, l).reshape(b, l, D_MODEL)
        if not last:
            xc = _ffn(xc1, csc2, csh2, cg2, w1, w3, w2, l2g, l2b, lc).reshape(b, lc, D_MODEL)
    return x
```
